```python
import math, functools
import jax, jax.numpy as jnp
from jax import lax
import numpy as np

D_MODEL = 1024
BATCH = 2
SEQ = 16384
DEPTH = 2

SB_HEADS = 8
SB_HEAD_DIM = 64
SB_WIDTH = SB_HEADS * SB_HEAD_DIM
RET_HEADS = 4
RET_HEAD_DIM = 128
RET_WIDTH = RET_HEADS * RET_HEAD_DIM
EVEN_IN = 3 * SB_WIDTH + 4 * RET_WIDTH
GLA_HEADS = 4
GLA_DK = 128
GLA_DV = 256
GLA_K_WIDTH = GLA_HEADS * GLA_DK
GLA_V_WIDTH = GLA_HEADS * GLA_DV
GLA_GATE_RANK = 16
GLA_GATE_TAU = 16.0
ODD_IN = 2 * GLA_K_WIDTH + 2 * GLA_V_WIDTH + GLA_GATE_RANK
D_FF = 2816
CONV_WIDTH = 3
Q_BLOCK = 128
RET_CHUNK = 128
GLA_CHUNK = 64
ROPE_BASE = 10000.0
EPS = 1e-6
N_EVEN = (DEPTH + 1) // 2
N_ODD = DEPTH // 2

kernel_name = "hybrid_stickbreak_retention_gla_convffn"


def rms_norm(x, g):
    xf = x.astype(jnp.float32)
    y = xf * lax.rsqrt(jnp.mean(xf * xf, axis=-1, keepdims=True) + EPS)
    return (y * g.astype(jnp.float32)).astype(x.dtype)


def to_heads(t, n):
    b, s, w = t.shape
    return t.reshape(b, s, n, w // n).transpose(0, 2, 1, 3)


def from_heads(t):
    b, h, s, d = t.shape
    return t.transpose(0, 2, 1, 3).reshape(b, s, h * d)


def rope(x, pos):
    d = x.shape[-1]
    half = d // 2
    inv = ROPE_BASE ** (-jnp.arange(half, dtype=jnp.float32) / half)
    ang = pos.astype(jnp.float32)[:, None] * inv[None, :]
    cos, sin = jnp.cos(ang), jnp.sin(ang)
    xf = x.astype(jnp.float32)
    x1, x2 = xf[..., :half], xf[..., half:]
    return jnp.concatenate([x1 * cos - x2 * sin, x1 * sin + x2 * cos], axis=-1).astype(x.dtype)


def stick_breaking_attention(q, k, v):
    b, h, s, d = q.shape
    nb = s // Q_BLOCK
    scale = d ** -0.5
    qb = q.reshape(b, h, nb, Q_BLOCK, d).transpose(2, 0, 1, 3, 4)
    kf = k.astype(jnp.float32)
    vf = v.astype(jnp.float32)
    key_pos = jnp.arange(s)

    def block(args):
        qi, i = args
        q_pos = i * Q_BLOCK + jnp.arange(Q_BLOCK)
        z = jnp.einsum('bhqd,bhkd->bhqk', qi.astype(jnp.float32), kf) * scale
        mask = key_pos[None, :] < q_pos[:, None]
        log_beta = jax.nn.log_sigmoid(z)
        log_1m_beta = jnp.where(mask, -jax.nn.softplus(z), 0.0)
        suffix = lax.cumsum(log_1m_beta, axis=3, reverse=True) - log_1m_beta
        w = jnp.where(mask, jnp.exp(log_beta + suffix), 0.0)
        return jnp.einsum('bhqk,bhkd->bhqd', w, vf)

    out = lax.map(block, (qb, jnp.arange(nb)))
    return out.transpose(1, 2, 0, 3, 4).reshape(b, h, s, d).astype(v.dtype)


def retention_chunkwise(q, k, v, log_gamma):
    b, h, s, d = q.shape
    c = RET_CHUNK
    nc = s // c
    qc = q.astype(jnp.float32).reshape(b, h, nc, c, d)
    kc = (k.astype(jnp.float32) * d ** -0.5).reshape(b, h, nc, c, d)
    vc = v.astype(jnp.float32).reshape(b, h, nc, c, d)
    pos = jnp.arange(c, dtype=jnp.float32)
    lg = log_gamma[:, None]
    diff = pos[:, None] - pos[None, :]
    decay_intra = jnp.where(diff >= 0, jnp.exp(lg[:, :, None] * jnp.maximum(diff, 0.0)), 0.0)
    scores = jnp.einsum('bhnid,bhnjd->bhnij', qc, kc) * decay_intra[None, :, None]
    inner = jnp.einsum('bhnij,bhnje->bhnie', scores, vc)
    zeta = jnp.exp(lg * (c - 1 - pos))
    contrib = jnp.einsum('bhnjd,hj,bhnje->bhnde', kc, zeta, vc)
    g_chunk = jnp.exp(lg * c)[None, :, :, None]

    def step(state, cn):
        return g_chunk * state + cn, state

    _, r_prev = lax.scan(step, jnp.zeros((b, h, d, d), jnp.float32), contrib.transpose(2, 0, 1, 3, 4))
    r_prev = r_prev.transpose(1, 2, 0, 3, 4)
    xi = jnp.exp(lg * (pos + 1.0))
    cross = jnp.einsum('bhnid,bhnde->bhnie', qc, r_prev) * xi[None, :, None, :, None]
    return (inner + cross).reshape(b, h, s, d).astype(v.dtype)


def gla_chunked(q, k, v, log_alpha):
    b, h, s, dk = q.shape
    dv = v.shape[-1]
    c = GLA_CHUNK
    nc = s // c
    qc = (q.astype(jnp.float32) * dk ** -0.5).reshape(b, h, nc, c, dk)
    kc = k.astype(jnp.float32).reshape(b, h, nc, c, dk)
    vc = v.astype(jnp.float32).reshape(b, h, nc, c, dv)
    cum = lax.cumsum(log_alpha.astype(jnp.float32).reshape(b, h, nc, c, dk), axis=3)
    cum_last = cum[:, :, :, -1:, :]
    q_t = qc * jnp.exp(cum)
    k_t = kc * jnp.exp(-cum)
    causal = jnp.tril(jnp.ones((c, c), dtype=bool))
    a = jnp.where(causal, jnp.einsum('bhnid,bhnjd->bhnij', q_t, k_t), 0.0)
    intra = jnp.einsum('bhnij,bhnje->bhnie', a, vc)
    contrib = jnp.einsum('bhnjd,bhnje->bhnde', kc * jnp.exp(cum_last - cum), vc)
    chunk_decay = jnp.exp(cum_last[:, :, :, 0, :])

    def step(state, inp):
        dec, cn = inp
        return dec[..., None] * state + cn, state

    _, s_prev = lax.scan(step, jnp.zeros((b, h, dk, dv), jnp.float32),
                         (chunk_decay.transpose(2, 0, 1, 3), contrib.transpose(2, 0, 1, 3, 4)))
    s_prev = s_prev.transpose(1, 2, 0, 3, 4)
    inter = jnp.einsum('bhnid,bhnde->bhnie', q_t, s_prev)
    return (intra + inter).reshape(b, h, s, dv).astype(v.dtype)


def even_mixer(hn, w_in, q_gain, k_gain, ret_gain, w_out):
    s = hn.shape[1]
    p = hn @ w_in
    cuts = [SB_WIDTH, 2 * SB_WIDTH, 3 * SB_WIDTH,
            3 * SB_WIDTH + RET_WIDTH, 3 * SB_WIDTH + 2 * RET_WIDTH, 3 * SB_WIDTH + 3 * RET_WIDTH]
    sb_q, sb_k, sb_v, r_q, r_k, r_v, r_g = jnp.split(p, cuts, axis=-1)
    qa = rms_norm(to_heads(sb_q, SB_HEADS), q_gain)
    ka = rms_norm(to_heads(sb_k, SB_HEADS), k_gain)
    out_a = from_heads(stick_breaking_attention(qa, ka, to_heads(sb_v, SB_HEADS)))
    pos = jnp.arange(s)
    qb = rope(to_heads(r_q, RET_HEADS), pos)
    kb = rope(to_heads(r_k, RET_HEADS), pos)
    log_gamma = jnp.log(1.0 - jnp.exp2(-5.0 - jnp.arange(RET_HEADS, dtype=jnp.float32)))
    ret = retention_chunkwise(qb, kb, to_heads(r_v, RET_HEADS), log_gamma)
    out_b = from_heads(rms_norm(ret, ret_gain)) * jax.nn.silu(r_g)
    return jnp.concatenate([out_a, out_b], axis=-1) @ w_out


def odd_mixer(hn, w_in, w_alpha, b_alpha, out_gain, w_out):
    p = hn @ w_in
    cuts = [GLA_K_WIDTH, 2 * GLA_K_WIDTH, 2 * GLA_K_WIDTH + GLA_V_WIDTH, 2 * GLA_K_WIDTH + 2 * GLA_V_WIDTH]
    g_q, g_k, g_v, g_r, g_a = jnp.split(p, cuts, axis=-1)
    log_alpha = jax.nn.log_sigmoid((g_a @ w_alpha + b_alpha).astype(jnp.float32)) / GLA_GATE_TAU
    o = gla_chunked(to_heads(g_q, GLA_HEADS), to_heads(g_k, GLA_HEADS),
                    to_heads(g_v, GLA_HEADS), to_heads(log_alpha, GLA_HEADS))
    return (from_heads(rms_norm(o, out_gain)) * jax.nn.silu(g_r)) @ w_out


def conv_ffn(hn, w_up, conv_w, conv_b, w_down):
    s = hn.shape[1]
    u, v = jnp.split(hn @ w_up, 2, axis=-1)
    up = jnp.pad(u, ((0, 0), (CONV_WIDTH - 1, 0), (0, 0)))
    uc = conv_b + conv_w[0] * up[:, 0:s]
    for j in range(1, CONV_WIDTH):
        uc = uc + conv_w[j] * up[:, j:j + s]
    return (jax.nn.silu(uc) * v) @ w_down


def setup_inputs(seed: int = 0) -> dict:
    key = jax.random.key(seed)
    ks = jax.random.split(key, 20)
    f32 = jnp.float32

    def nrm(k, shape, scale):
        return jax.random.normal(k, shape, f32) * scale

    return {
        'x': nrm(ks[0], (BATCH, SEQ, D_MODEL), 1.0),
        'mix_norm_g': 1.0 + nrm(ks[1], (DEPTH, D_MODEL), 0.02),
        'even_w_in': nrm(ks[2], (N_EVEN, D_MODEL, EVEN_IN), D_MODEL ** -0.5),
        'sb_q_gain': 1.0 + nrm(ks[3], (N_EVEN, SB_HEAD_DIM), 0.02),
        'sb_k_gain': 1.0 + nrm(ks[4], (N_EVEN, SB_HEAD_DIM), 0.02),
        'ret_out_gain': 1.0 + nrm(ks[5], (N_EVEN, RET_HEAD_DIM), 0.02),
        'even_w_out': nrm(ks[6], (N_EVEN, SB_WIDTH + RET_WIDTH, D_MODEL), (SB_WIDTH + RET_WIDTH) ** -0.5),
        'odd_w_in': nrm(ks[7], (N_ODD, D_MODEL, ODD_IN), D_MODEL ** -0.5),
        'gla_w_alpha': nrm(ks[8], (N_ODD, GLA_GATE_RANK, GLA_K_WIDTH), GLA_GATE_RANK ** -0.5),
        'gla_b_alpha': nrm(ks[9], (N_ODD, GLA_K_WIDTH), 0.1),
        'gla_out_gain': 1.0 + nrm(ks[10], (N_ODD, GLA_DV), 0.02),
        'odd_w_out': nrm(ks[11], (N_ODD, GLA_V_WIDTH, D_MODEL), GLA_V_WIDTH ** -0.5),
        'ffn_norm_g': 1.0 + nrm(ks[12], (DEPTH, D_MODEL), 0.02),
        'ffn_w_up': nrm(ks[13], (DEPTH, D_MODEL, 2 * D_FF), D_MODEL ** -0.5),
        'ffn_conv_w': nrm(ks[14], (DEPTH, CONV_WIDTH, D_FF), CONV_WIDTH ** -0.5),
        'ffn_conv_b': nrm(ks[15], (DEPTH, D_FF), 0.01),
        'ffn_w_down': nrm(ks[16], (DEPTH, D_FF, D_MODEL), D_FF ** -0.5),
    }


def reference(x, mix_norm_g, even_w_in, sb_q_gain, sb_k_gain, ret_out_gain, even_w_out,
              odd_w_in, gla_w_alpha, gla_b_alpha, gla_out_gain, odd_w_out,
              ffn_norm_g, ffn_w_up, ffn_conv_w, ffn_conv_b, ffn_w_down):
    h = x
    for layer in range(DEPTH):
        hn = rms_norm(h, mix_norm_g[layer])
        if layer % 2 == 0:
            e = layer // 2
            mix = even_mixer(hn, even_w_in[e], sb_q_gain[e], sb_k_gain[e], ret_out_gain[e], even_w_out[e])
        else:
            o = layer // 2
            mix = odd_mixer(hn, odd_w_in[o], gla_w_alpha[o], gla_b_alpha[o], gla_out_gain[o], odd_w_out[o])
        h = h + mix.astype(h.dtype)
        hn = rms_norm(h, ffn_norm_g[layer])
        h = h + conv_ffn(hn, ffn_w_up[layer], ffn_conv_w[layer], ffn_conv_b[layer], ffn_w_down[layer]).astype(h.dtype)
    return h
```

```python
import functools
import math

import jax
import jax.numpy as jnp
from jax import lax
from jax.experimental import pallas as pl
from jax.experimental.pallas import tpu as pltpu

F32 = jnp.float32
BF16 = jnp.bfloat16

EPS = 1e-6
LOG2E = 1.4426950408889634
ROPE_BASE = 10000.0

SB_HEADS = 8
SB_HEAD_DIM = 64
SB_WIDTH = SB_HEADS * SB_HEAD_DIM
RET_HEADS = 4
RET_HEAD_DIM = 128
RET_WIDTH = RET_HEADS * RET_HEAD_DIM
RET_CHUNK = 128
GLA_HEADS = 4
GLA_DK = 128
GLA_DV = 256
GLA_K_WIDTH = GLA_HEADS * GLA_DK
GLA_V_WIDTH = GLA_HEADS * GLA_DV
GLA_GATE_RANK = 16
GLA_GATE_TAU = 16.0
GLA_CHUNK = 64
CONV_WIDTH = 3

LANES = 128
SUBLANES = 8
VMEM_LIMIT = 56 * 1024 * 1024

ROW_TILE = 512
SB_BLOCK = 256
RET_GROUP = 1024
GLA_GROUP = 512
FF_CHUNK = 256


def _params(*sem):
    return pltpu.CompilerParams(dimension_semantics=sem, vmem_limit_bytes=VMEM_LIMIT)


def _const_spec(shape):
    n = len(shape)
    return pl.BlockSpec(shape, lambda *_: (0,) * n)


def _rms_rows(x, g):
    ms = jnp.mean(x * x, axis=-1, keepdims=True)
    return x * lax.rsqrt(ms + EPS) * g


def _silu(x):
    return x / (1.0 + jnp.exp(-x))


def _half_head_rms(blk):
    sq = blk * blk
    lane = lax.broadcasted_iota(jnp.int32, blk.shape, 1)
    lo = lane < SB_HEAD_DIM
    s_lo = jnp.sum(jnp.where(lo, sq, 0.0), axis=-1, keepdims=True)
    s_hi = jnp.sum(jnp.where(lo, 0.0, sq), axis=-1, keepdims=True)
    ms = jnp.where(lo, s_lo, s_hi) * (1.0 / SB_HEAD_DIM)
    return blk * lax.rsqrt(ms + EPS)


def _even_in_kernel(x_ref, g_ref, w_ref, qg_ref, kg_ref, cos_ref, sin_ref,
                    sbq_ref, sbk_ref, sbv_ref, rq_ref, rk_ref, rv_ref, rg_ref):
    hn = _rms_rows(x_ref[...], g_ref[...]).astype(BF16)

    def proj(seg):
        return jnp.dot(hn, w_ref[:, seg * SB_WIDTH:(seg + 1) * SB_WIDTH],
                       preferred_element_type=F32)

    q_scale = SB_HEAD_DIM ** -0.5 * LOG2E
    p = proj(0)
    for hp in range(SB_WIDTH // LANES):
        sl = slice(hp * LANES, (hp + 1) * LANES)
        sbq_ref[:, sl] = (_half_head_rms(p[:, sl]) * (qg_ref[...] * q_scale)).astype(BF16)
    p = proj(1)
    for hp in range(SB_WIDTH // LANES):
        sl = slice(hp * LANES, (hp + 1) * LANES)
        sbk_ref[:, sl] = (_half_head_rms(p[:, sl]) * kg_ref[...]).astype(BF16)
    sbv_ref[...] = proj(2).astype(BF16)

    cos2 = cos_ref[...]
    sin2 = sin_ref[...]

    def rope(p, out_ref, scale):
        for h in range(RET_HEADS):
            sl = slice(h * LANES, (h + 1) * LANES)
            blk = p[:, sl]
            rot = pltpu.roll(blk, RET_HEAD_DIM // 2, 1)
            out_ref[:, sl] = ((blk * cos2 + rot * sin2) * scale).astype(BF16)

    rope(proj(3), rq_ref, 1.0)
    rope(proj(4), rk_ref, RET_HEAD_DIM ** -0.5)
    rv_ref[...] = proj(5).astype(BF16)
    rg_ref[...] = proj(6).astype(BF16)


def _even_in(x2, g, w, qg, kg, cos2, sin2, seq):
    t, d = x2.shape
    tm = min(ROW_TILE, seq)
    pos_blocks = seq // tm
    row = lambda i: (i, 0)
    out = jax.ShapeDtypeStruct((t, SB_WIDTH), BF16)
    return pl.pallas_call(
        _even_in_kernel,
        grid=(t // tm,),
        in_specs=[
            pl.BlockSpec((tm, d), row),
            _const_spec((1, d)),
            _const_spec(w.shape),
            _const_spec((1, LANES)),
            _const_spec((1, LANES)),
            pl.BlockSpec((tm, LANES), lambda i: (i % pos_blocks, 0)),
            pl.BlockSpec((tm, LANES), lambda i: (i % pos_blocks, 0)),
        ],
        out_specs=[pl.BlockSpec((tm, SB_WIDTH), row)] * 7,
        out_shape=[out] * 7,
        compiler_params=_params("parallel"),
        name="even_in",
    )(x2, g, w, qg, kg, cos2, sin2)


def _softplus2(z):
    return jnp.maximum(z, 0.0) + jnp.log2(1.0 + jnp.exp2(-jnp.abs(z)))


def _sb_kernel(q_ref, kt_ref, v_ref, tri_ref, o_ref, acc_ref, carry_ref):
    i = pl.program_id(2)
    blk = q_ref.shape[0]
    q = q_ref[...]
    lane = lax.broadcasted_iota(jnp.int32, q.shape, 1)
    lo = lane < SB_HEAD_DIM
    zero = jnp.zeros_like(q)
    q_heads = (jnp.where(lo, q, zero), jnp.where(lo, zero, q))
    tri = tri_ref[...]

    def step(j, diagonal):
        kt = kt_ref[j]
        v = v_ref[pl.ds(pl.multiple_of(j * blk, blk), blk), :]
        if diagonal:
            r = lax.broadcasted_iota(jnp.int32, (blk, blk), 0)
            c = lax.broadcasted_iota(jnp.int32, (blk, blk), 1)
            mask = c < r
        for h in range(2):
            z = jnp.dot(q_heads[h], kt, preferred_element_type=F32)
            sp = _softplus2(z)
            if diagonal:
                sp = jnp.where(mask, sp, 0.0)
            csum = jnp.dot(sp.astype(BF16), tri, preferred_element_type=F32)
            tot = jnp.sum(sp, axis=-1, keepdims=True)
            if diagonal:
                w = jnp.where(mask, jnp.exp2(z - csum), 0.0)
            else:
                w = jnp.exp2(z - (csum + carry_ref[h]))
            pv = jnp.dot(w.astype(BF16), v, preferred_element_type=F32)
            if diagonal:
                acc_ref[h] = pv
                carry_ref[h] = tot
            else:
                acc_ref[h] += pv
                carry_ref[h] += tot

    step(i, True)

    def body(n, c):
        step(i - 1 - n, False)
        return c

    lax.fori_loop(0, i, body, 0)
    o_ref[...] = jnp.where(lo, acc_ref[0], acc_ref[1]).astype(o_ref.dtype)


def _sb_attention(q, kt, v, tri):
    b, s, width = q.shape
    blk = kt.shape[-1]
    pairs = width // LANES
    return pl.pallas_call(
        _sb_kernel,
        grid=(b, pairs, s // blk),
        in_specs=[
            pl.BlockSpec((None, blk, LANES), lambda bi, hp, i: (bi, i, hp)),
            pl.BlockSpec((None, None, s // blk, LANES, blk), lambda bi, hp, i: (bi, hp, 0, 0, 0)),
            pl.BlockSpec((None, s, LANES), lambda bi, hp, i: (bi, 0, hp)),
            _const_spec(tri.shape),
        ],
        out_specs=pl.BlockSpec((None, blk, LANES), lambda bi, hp, i: (bi, i, hp)),
        out_shape=jax.ShapeDtypeStruct((b, s, width), BF16),
        scratch_shapes=[pltpu.VMEM((2, blk, LANES), F32), pltpu.VMEM((2, blk, 1), F32)],
        compiler_params=_params("parallel", "parallel", "arbitrary"),
        name="sb_attention",
    )(q, kt, v, tri)


def _ret_kernel(q_ref, k_ref, v_ref, g_ref, gain_ref, decay_ref, zeta_ref, xi_ref, gc_ref,
                o_ref, state_ref):
    @pl.when(pl.program_id(2) == 0)
    def _():
        state_ref[...] = jnp.zeros_like(state_ref)

    c = RET_CHUNK
    decay = decay_ref[...]
    zeta = zeta_ref[...]
    xi = xi_ref[...]
    g_chunk = gc_ref[...]
    gain = gain_ref[...]
    for n in range(q_ref.shape[0] // c):
        rows = slice(n * c, (n + 1) * c)
        q = q_ref[rows, :]
        k = k_ref[rows, :]
        v = v_ref[rows, :]
        state = state_ref[...]
        scores = lax.dot_general(q, k, (((1,), (1,)), ((), ())), preferred_element_type=F32) * decay
        inner = jnp.dot(scores.astype(BF16), v, preferred_element_type=F32)
        cross = jnp.dot(q, state.astype(BF16), preferred_element_type=F32) * xi
        kz = (k.astype(F32) * zeta).astype(BF16)
        contrib = lax.dot_general(kz, v, (((0,), (0,)), ((), ())), preferred_element_type=F32)
        state_ref[...] = g_chunk * state + contrib
        ret = _rms_rows(inner + cross, gain)
        o_ref[rows, :] = (ret * _silu(g_ref[rows, :].astype(F32))).astype(o_ref.dtype)


def _retention(q, k, v, g, gain, decay, zeta, xi, gc):
    b, s, width = q.shape
    rows = min(RET_GROUP, s)
    blk = pl.BlockSpec((None, rows, LANES), lambda bi, h, n: (bi, n, h))
    head = lambda shape: pl.BlockSpec((None,) + shape, lambda bi, h, n: (h, 0, 0))
    return pl.pallas_call(
        _ret_kernel,
        grid=(b, width // LANES, s // rows),
        in_specs=[blk, blk, blk, blk, _const_spec((1, LANES)),
                  head((RET_CHUNK, RET_CHUNK)), head((RET_CHUNK, 1)), head((RET_CHUNK, 1)),
                  head((1, 1))],
        out_specs=blk,
        out_shape=jax.ShapeDtypeStruct((b, s, width), BF16),
        scratch_shapes=[pltpu.VMEM((RET_HEAD_DIM, RET_HEAD_DIM), F32)],
        compiler_params=_params("parallel", "parallel", "arbitrary"),
        name="retention",
    )(q, k, v, g, gain, decay, zeta, xi, gc)


def _even_out_kernel(h_ref, a_ref, b_ref, wa_ref, wb_ref, o_ref):
    mix = jnp.dot(a_ref[...], wa_ref[...], preferred_element_type=F32)
    mix += jnp.dot(b_ref[...], wb_ref[...], preferred_element_type=F32)
    o_ref[...] = h_ref[...] + mix


def _even_out(h2, a, b, wa, wb):
    t, d = h2.shape
    tm = min(ROW_TILE, t)
    row = lambda i: (i, 0)
    return pl.pallas_call(
        _even_out_kernel,
        grid=(t // tm,),
        in_specs=[pl.BlockSpec((tm, d), row), pl.BlockSpec((tm, a.shape[1]), row),
                  pl.BlockSpec((tm, b.shape[1]), row), _const_spec(wa.shape), _const_spec(wb.shape)],
        out_specs=pl.BlockSpec((tm, d), row),
        out_shape=jax.ShapeDtypeStruct((t, d), F32),
        compiler_params=_params("parallel"),
        name="even_out",
    )(h2, a, b, wa, wb)


def _odd_out_kernel(h_ref, a_ref, w_ref, o_ref):
    o_ref[...] = h_ref[...] + jnp.dot(a_ref[...], w_ref[...], preferred_element_type=F32)


def _odd_out(h2, a, w):
    t, d = h2.shape
    tm = min(ROW_TILE, t)
    row = lambda i: (i, 0)
    return pl.pallas_call(
        _odd_out_kernel,
        grid=(t // tm,),
        in_specs=[pl.BlockSpec((tm, d), row), pl.BlockSpec((tm, a.shape[1]), row), _const_spec(w.shape)],
        out_specs=pl.BlockSpec((tm, d), row),
        out_shape=jax.ShapeDtypeStruct((t, d), F32),
        compiler_params=_params("parallel"),
        name="odd_out",
    )(h2, a, w)


def _ffn_kernel(h_ref, g_ref, wu_ref, wv_ref, cw_ref, cb_ref, wd_ref, o_ref,
                ubuf_ref, prev_ref, gate_ref, *, tiles_per_seq):
    tm = h_ref.shape[0]
    d_ff = wu_ref.shape[1]
    halo = SUBLANES

    @pl.when(pl.program_id(0) % tiles_per_seq == 0)
    def _():
        prev_ref[...] = jnp.zeros_like(prev_ref)

    h = h_ref[...]
    hn = _rms_rows(h, g_ref[...]).astype(BF16)
    for c0 in range(0, d_ff, FF_CHUNK):
        cols = slice(c0, c0 + FF_CHUNK)
        u = jnp.dot(hn, wu_ref[:, cols], preferred_element_type=F32)
        v = jnp.dot(hn, wv_ref[:, cols], preferred_element_type=F32)
        ubuf_ref[0:halo, :] = prev_ref[:, cols]
        ubuf_ref[halo:halo + tm, :] = u
        prev_ref[:, cols] = u[tm - halo:, :]
        uc = (cb_ref[:, cols]
              + cw_ref[0:1, cols] * ubuf_ref[halo - 2:halo - 2 + tm, :]
              + cw_ref[1:2, cols] * ubuf_ref[halo - 1:halo - 1 + tm, :]
              + cw_ref[2:3, cols] * u)
        gate_ref[:, cols] = (_silu(uc) * v).astype(BF16)
    o_ref[...] = h + jnp.dot(gate_ref[...], wd_ref[...], preferred_element_type=F32)


def _ffn(h2, g, wu, wv, cw, cb, wd, seq):
    t, d = h2.shape
    d_ff = wu.shape[1]
    tm = min(ROW_TILE, seq)
    row = lambda i: (i, 0)
    return pl.pallas_call(
        functools.partial(_ffn_kernel, tiles_per_seq=seq // tm),
        grid=(t // tm,),
        in_specs=[pl.BlockSpec((tm, d), row), _const_spec((1, d)), _const_spec(wu.shape),
                  _const_spec(wv.shape), _const_spec(cw.shape), _const_spec(cb.shape),
                  _const_spec(wd.shape)],
        out_specs=pl.BlockSpec((tm, d), row),
        out_shape=jax.ShapeDtypeStruct((t, d), F32),
        scratch_shapes=[pltpu.VMEM((tm + SUBLANES, FF_CHUNK), F32),
                        pltpu.VMEM((SUBLANES, d_ff), F32),
                        pltpu.VMEM((tm, d_ff), BF16)],
        compiler_params=_params("arbitrary"),
        name="conv_ffn",
    )(h2, g, wu, wv, cw, cb, wd)


def _log_sigmoid(x):
    return jnp.minimum(x, 0.0) - jnp.log(1.0 + jnp.exp(-jnp.abs(x)))


def _odd_in_kernel(x_ref, g_ref, w_ref, wa_ref, walpha_ref, balpha_ref,
                   q_ref, k_ref, v_ref, r_ref, la_ref):
    hn = _rms_rows(x_ref[...], g_ref[...]).astype(BF16)

    def proj(c0, width):
        return jnp.dot(hn, w_ref[:, c0:c0 + width], preferred_element_type=F32)

    q_ref[...] = (proj(0, GLA_K_WIDTH) * GLA_DK ** -0.5).astype(BF16)
    k_ref[...] = proj(GLA_K_WIDTH, GLA_K_WIDTH).astype(BF16)
    v_ref[...] = proj(2 * GLA_K_WIDTH, GLA_V_WIDTH).astype(BF16)
    r_ref[...] = proj(2 * GLA_K_WIDTH + GLA_V_WIDTH, GLA_V_WIDTH).astype(BF16)
    ga = jnp.dot(hn, wa_ref[...], preferred_element_type=F32)
    pre = jnp.dot(ga, walpha_ref[...], preferred_element_type=F32,
                  precision=lax.Precision.HIGHEST) + balpha_ref[...]
    la_ref[...] = _log_sigmoid(pre) * (LOG2E / GLA_GATE_TAU)


def _odd_in(x2, g, w, wa, walpha, balpha):
    t, d = x2.shape
    tm = min(ROW_TILE, t)
    row = lambda i: (i, 0)
    kout = jax.ShapeDtypeStruct((t, GLA_K_WIDTH), BF16)
    vout = jax.ShapeDtypeStruct((t, GLA_V_WIDTH), BF16)
    return pl.pallas_call(
        _odd_in_kernel,
        grid=(t // tm,),
        in_specs=[pl.BlockSpec((tm, d), row), _const_spec((1, d)), _const_spec(w.shape),
                  _const_spec(wa.shape), _const_spec(walpha.shape), _const_spec(balpha.shape)],
        out_specs=[pl.BlockSpec((tm, GLA_K_WIDTH), row), pl.BlockSpec((tm, GLA_K_WIDTH), row),
                   pl.BlockSpec((tm, GLA_V_WIDTH), row), pl.BlockSpec((tm, GLA_V_WIDTH), row),
                   pl.BlockSpec((tm, GLA_K_WIDTH), row)],
        out_shape=[kout, kout, vout, vout, jax.ShapeDtypeStruct((t, GLA_K_WIDTH), F32)],
        compiler_params=_params("parallel"),
        name="odd_in",
    )(x2, g, w, wa, walpha, balpha)


def _gla_kernel(q_ref, k_ref, v_ref, r_ref, la_ref, gain_ref, tri_ref, o_ref, state_ref):
    @pl.when(pl.program_id(2) == 0)
    def _():
        state_ref[...] = jnp.zeros_like(state_ref)

    c = GLA_CHUNK
    rows_total = q_ref.shape[0]
    la = la_ref[...]
    la_hi = la.astype(BF16)
    la_lo = (la - la_hi.astype(F32)).astype(BF16)
    tri = tri_ref[...]
    cum = (jnp.dot(tri, la_hi, preferred_element_type=F32)
           + jnp.dot(tri, la_lo, preferred_element_type=F32))
    ri = lax.broadcasted_iota(jnp.int32, (c, c), 0)
    ci = lax.broadcasted_iota(jnp.int32, (c, c), 1)
    causal = ci <= ri
    gain = gain_ref[...]
    for n in range(rows_total // c):
        rows = slice(n * c, (n + 1) * c)
        cum_n = cum[n * c:(n + 1) * c, :]
        cum_last = cum_n[c - 1:c, :]
        q = q_ref[rows, :].astype(F32)
        k = k_ref[rows, :].astype(F32)
        v = v_ref[rows, :]
        q_t = (q * jnp.exp2(cum_n)).astype(BF16)
        k_t = (k * jnp.exp2(-cum_n)).astype(BF16)
        k_d = (k * jnp.exp2(cum_last - cum_n)).astype(BF16)
        a = lax.dot_general(q_t, k_t, (((1,), (1,)), ((), ())), preferred_element_type=F32)
        a = jnp.where(causal, a, 0.0)
        intra = jnp.dot(a.astype(BF16), v, preferred_element_type=F32)
        state = state_ref[...]
        inter = lax.dot_general(q_t, state.astype(BF16), (((1,), (1,)), ((), ())),
                                preferred_element_type=F32)
        contrib = lax.dot_general(v, k_d, (((0,), (0,)), ((), ())), preferred_element_type=F32)
        state_ref[...] = jnp.exp2(cum_last) * state + contrib
        o = _rms_rows(intra + inter, gain)
        o_ref[rows, :] = (o * _silu(r_ref[rows, :].astype(F32))).astype(o_ref.dtype)


def _gla(q, k, v, r, la, gain, tri):
    b, s, kw = q.shape
    vw = v.shape[-1]
    rows = min(GLA_GROUP, s)
    kblk = pl.BlockSpec((None, rows, GLA_DK), lambda bi, h, n: (bi, n, h))
    vblk = pl.BlockSpec((None, rows, GLA_DV), lambda bi, h, n: (bi, n, h))
    return pl.pallas_call(
        _gla_kernel,
        grid=(b, kw // GLA_DK, s // rows),
        in_specs=[kblk, kblk, vblk, vblk, kblk, _const_spec((1, GLA_DV)), _const_spec(tri.shape)],
        out_specs=vblk,
        out_shape=jax.ShapeDtypeStruct((b, s, vw), BF16),
        scratch_shapes=[pltpu.VMEM((GLA_DV, GLA_DK), F32)],
        compiler_params=_params("parallel", "parallel", "arbitrary"),
        name="gla",
    )(q, k, v, r, la, gain, tri)


def _rope_tables(seq):
    half = RET_HEAD_DIM // 2
    inv = ROPE_BASE ** (-jnp.arange(half, dtype=F32) / half)
    ang = jnp.arange(seq, dtype=F32)[:, None] * inv[None, :]
    cos, sin = jnp.cos(ang), jnp.sin(ang)
    return jnp.concatenate([cos, cos], axis=-1), jnp.concatenate([-sin, sin], axis=-1)


def _retention_tables():
    c = RET_CHUNK
    lg = jnp.log(1.0 - jnp.exp2(-5.0 - jnp.arange(RET_HEADS, dtype=F32)))[:, None]
    pos = jnp.arange(c, dtype=F32)
    diff = pos[:, None] - pos[None, :]
    decay = jnp.where(diff >= 0, jnp.exp(lg[:, :, None] * jnp.maximum(diff, 0.0)), 0.0)
    zeta = jnp.exp(lg * (c - 1 - pos))[:, :, None]
    xi = jnp.exp(lg * (pos + 1.0))[:, :, None]
    g_chunk = jnp.exp(lg * c)[:, :, None]
    return decay, zeta, xi, g_chunk


def _suffix_tri(n):
    idx = jnp.arange(n)
    return (idx[:, None] >= idx[None, :]).astype(BF16)


def _chunk_prefix_tri(rows, chunk):
    idx = jnp.arange(rows)
    same = (idx[:, None] // chunk) == (idx[None, :] // chunk)
    return (same & (idx[None, :] <= idx[:, None])).astype(BF16)


def kernel(x, mix_norm_g, even_w_in, sb_q_gain, sb_k_gain, ret_out_gain, even_w_out,
           odd_w_in, gla_w_alpha, gla_b_alpha, gla_out_gain, odd_w_out,
           ffn_norm_g, ffn_w_up, ffn_conv_w, ffn_conv_b, ffn_w_down):
    b, s, d = x.shape
    t = b * s
    depth = mix_norm_g.shape[0]
    d_ff = ffn_w_down.shape[1]
    h = x.reshape(t, d)

    cos2, sin2 = _rope_tables(s)
    decay, zeta, xi, g_chunk = _retention_tables()
    sb_blk = min(SB_BLOCK, s)
    sb_tri = _suffix_tri(sb_blk)
    gla_tri = _chunk_prefix_tri(min(GLA_GROUP, s), GLA_CHUNK)

    for layer in range(depth):
        g_mix = mix_norm_g[layer][None, :]
        if layer % 2 == 0:
            e = layer // 2
            qg = jnp.tile(sb_q_gain[e], 2)[None, :]
            kg = jnp.tile(sb_k_gain[e], 2)[None, :]
            sbq, sbk, sbv, rq, rk, rv, rg = _even_in(
                h, g_mix, even_w_in[e].astype(BF16), qg, kg, cos2, sin2, s)
            pairs = SB_WIDTH // LANES
            kt = sbk.reshape(b, s // sb_blk, sb_blk, pairs, LANES).transpose(0, 3, 1, 4, 2)
            out_a = _sb_attention(sbq.reshape(b, s, SB_WIDTH), kt, sbv.reshape(b, s, SB_WIDTH), sb_tri)
            shp = (b, s, RET_WIDTH)
            out_b = _retention(rq.reshape(shp), rk.reshape(shp), rv.reshape(shp), rg.reshape(shp),
                               ret_out_gain[e][None, :], decay, zeta, xi, g_chunk)
            w_out = even_w_out[e].astype(BF16)
            h = _even_out(h, out_a.reshape(t, SB_WIDTH), out_b.reshape(t, RET_WIDTH),
                          w_out[:SB_WIDTH], w_out[SB_WIDTH:])
        else:
            o = layer // 2
            w_in = odd_w_in[o]
            n_main = 2 * GLA_K_WIDTH + 2 * GLA_V_WIDTH
            wa = jnp.pad(w_in[:, n_main:], ((0, 0), (0, LANES - GLA_GATE_RANK))).astype(BF16)
            walpha = jnp.pad(gla_w_alpha[o], ((0, LANES - GLA_GATE_RANK), (0, 0)))
            gq, gk, gv, gr, la = _odd_in(h, g_mix, w_in[:, :n_main].astype(BF16), wa, walpha,
                                         gla_b_alpha[o][None, :])
            og = _gla(gq.reshape(b, s, GLA_K_WIDTH), gk.reshape(b, s, GLA_K_WIDTH),
                      gv.reshape(b, s, GLA_V_WIDTH), gr.reshape(b, s, GLA_V_WIDTH),
                      la.reshape(b, s, GLA_K_WIDTH), gla_out_gain[o][None, :], gla_tri)
            h = _odd_out(h, og.reshape(t, GLA_V_WIDTH), odd_w_out[o].astype(BF16))
        w_up = ffn_w_up[layer].astype(BF16)
        h = _ffn(h, ffn_norm_g[layer][None, :], w_up[:, :d_ff], w_up[:, d_ff:],
                 ffn_conv_w[layer], ffn_conv_b[layer][None, :], ffn_w_down[layer].astype(BF16), s)
    return h.reshape(b, s, d)
```

```python
import functools
import math

import jax
import jax.numpy as jnp
from jax import lax
from jax.experimental import pallas as pl
from jax.experimental.pallas import tpu as pltpu

F32 = jnp.float32
BF16 = jnp.bfloat16

EPS = 1e-6
LOG2E = 1.4426950408889634
ROPE_BASE = 10000.0
EXP2_UNDERFLOW = 150.0
SB_NORM_SLACK = 1.05

SB_HEADS = 8
SB_HEAD_DIM = 64
SB_WIDTH = SB_HEADS * SB_HEAD_DIM
RET_HEADS = 4
RET_HEAD_DIM = 128
RET_WIDTH = RET_HEADS * RET_HEAD_DIM
RET_CHUNK = 128
GLA_HEADS = 4
GLA_DK = 128
GLA_DV = 256
GLA_K_WIDTH = GLA_HEADS * GLA_DK
GLA_V_WIDTH = GLA_HEADS * GLA_DV
GLA_GATE_RANK = 16
GLA_GATE_TAU = 16.0
GLA_CHUNK = 64
CONV_WIDTH = 3

LANES = 128
SUBLANES = 8
VMEM_LIMIT = 56 * 1024 * 1024

ROW_TILE = 512
SB_BLOCK = 256
RET_GROUP = 1024
GLA_GROUP = 512
FF_CHUNK = 256


def _params(*sem):
    return pltpu.CompilerParams(dimension_semantics=sem, vmem_limit_bytes=VMEM_LIMIT)


def _const_spec(shape):
    n = len(shape)
    return pl.BlockSpec(shape, lambda *_: (0,) * n)


def _rms_rows(x, g):
    ms = jnp.mean(x * x, axis=-1, keepdims=True)
    return x * lax.rsqrt(ms + EPS) * g


def _silu(x):
    return x / (1.0 + jnp.exp(-x))


def _half_head_rms(blk):
    sq = blk * blk
    lane = lax.broadcasted_iota(jnp.int32, blk.shape, 1)
    lo = lane < SB_HEAD_DIM
    s_lo = jnp.sum(jnp.where(lo, sq, 0.0), axis=-1, keepdims=True)
    s_hi = jnp.sum(jnp.where(lo, 0.0, sq), axis=-1, keepdims=True)
    ms = jnp.where(lo, s_lo, s_hi) * (1.0 / SB_HEAD_DIM)
    return blk * lax.rsqrt(ms + EPS)


def _even_in_kernel(x_ref, g_ref, w_ref, qg_ref, kg_ref, cos_ref, sin_ref,
                    sbq_ref, sbk_ref, sbv_ref, rq_ref, rk_ref, rv_ref, rg_ref):
    hn = _rms_rows(x_ref[...], g_ref[...]).astype(BF16)

    def proj(seg):
        return jnp.dot(hn, w_ref[:, seg * SB_WIDTH:(seg + 1) * SB_WIDTH],
                       preferred_element_type=F32)

    q_scale = SB_HEAD_DIM ** -0.5 * LOG2E
    p = proj(0)
    for hp in range(SB_WIDTH // LANES):
        sl = slice(hp * LANES, (hp + 1) * LANES)
        sbq_ref[:, sl] = (_half_head_rms(p[:, sl]) * (qg_ref[...] * q_scale)).astype(BF16)
    p = proj(1)
    for hp in range(SB_WIDTH // LANES):
        sl = slice(hp * LANES, (hp + 1) * LANES)
        sbk_ref[:, sl] = (_half_head_rms(p[:, sl]) * kg_ref[...]).astype(BF16)
    sbv_ref[...] = proj(2).astype(BF16)

    cos2 = cos_ref[...]
    sin2 = sin_ref[...]

    def rope(p, out_ref, scale):
        for h in range(RET_HEADS):
            sl = slice(h * LANES, (h + 1) * LANES)
            blk = p[:, sl]
            rot = pltpu.roll(blk, RET_HEAD_DIM // 2, 1)
            out_ref[:, sl] = ((blk * cos2 + rot * sin2) * scale).astype(BF16)

    rope(proj(3), rq_ref, 1.0)
    rope(proj(4), rk_ref, RET_HEAD_DIM ** -0.5)
    rv_ref[...] = proj(5).astype(BF16)
    rg_ref[...] = proj(6).astype(BF16)


def _even_in(x2, g, w, qg, kg, cos2, sin2, seq):
    t, d = x2.shape
    tm = min(ROW_TILE, seq)
    pos_blocks = seq // tm
    row = lambda i: (i, 0)
    out = jax.ShapeDtypeStruct((t, SB_WIDTH), BF16)
    return pl.pallas_call(
        _even_in_kernel,
        grid=(t // tm,),
        in_specs=[
            pl.BlockSpec((tm, d), row),
            _const_spec((1, d)),
            _const_spec(w.shape),
            _const_spec((1, LANES)),
            _const_spec((1, LANES)),
            pl.BlockSpec((tm, LANES), lambda i: (i % pos_blocks, 0)),
            pl.BlockSpec((tm, LANES), lambda i: (i % pos_blocks, 0)),
        ],
        out_specs=[pl.BlockSpec((tm, SB_WIDTH), row)] * 7,
        out_shape=[out] * 7,
        compiler_params=_params("parallel"),
        name="even_in",
    )(x2, g, w, qg, kg, cos2, sin2)


def _softplus2(z):
    return jnp.maximum(z, 0.0) + jnp.log2(1.0 + jnp.exp2(-jnp.abs(z)))


def _sb_kernel(q_ref, kt_ref, v_ref, tri_ref, qg_ref, kg_ref, o_ref, acc_ref, carry_ref):
    i = pl.program_id(2)
    blk = q_ref.shape[0]
    z_bound = (SB_NORM_SLACK * SB_HEAD_DIM * SB_HEAD_DIM ** -0.5 * LOG2E
               * jnp.max(jnp.abs(qg_ref[...])) * jnp.max(jnp.abs(kg_ref[...])))
    dead_above = z_bound + EXP2_UNDERFLOW
    q = q_ref[...]
    lane = lax.broadcasted_iota(jnp.int32, q.shape, 1)
    lo = lane < SB_HEAD_DIM
    zero = jnp.zeros_like(q)
    q_heads = (jnp.where(lo, q, zero), jnp.where(lo, zero, q))
    tri = tri_ref[...]

    def step(j, diagonal):
        kt = kt_ref[j]
        v = v_ref[pl.ds(pl.multiple_of(j * blk, blk), blk), :]
        if diagonal:
            r = lax.broadcasted_iota(jnp.int32, (blk, blk), 0)
            c = lax.broadcasted_iota(jnp.int32, (blk, blk), 1)
            mask = c < r
        for h in range(2):
            z = jnp.dot(q_heads[h], kt, preferred_element_type=F32)
            sp = _softplus2(z)
            if diagonal:
                sp = jnp.where(mask, sp, 0.0)
            csum = jnp.dot(sp.astype(BF16), tri, preferred_element_type=F32)
            tot = jnp.sum(sp, axis=-1, keepdims=True)
            if diagonal:
                w = jnp.where(mask, jnp.exp2(z - csum), 0.0)
            else:
                w = jnp.exp2(z - (csum + carry_ref[h]))
            pv = jnp.dot(w.astype(BF16), v, preferred_element_type=F32)
            if diagonal:
                acc_ref[h] = pv
                carry_ref[h] = tot
            else:
                acc_ref[h] += pv
                carry_ref[h] += tot

    def live():
        return jnp.minimum(jnp.min(carry_ref[0]), jnp.min(carry_ref[1])) <= dead_above

    step(i, True)

    def body(c):
        n, _ = c
        step(i - 1 - n, False)
        return n + 1, live()

    lax.while_loop(lambda c: jnp.logical_and(c[0] < i, c[1]), body, (jnp.int32(0), live()))
    o_ref[...] = jnp.where(lo, acc_ref[0], acc_ref[1]).astype(o_ref.dtype)


def _sb_attention(q, kt, v, tri, qg, kg):
    b, s, width = q.shape
    blk = kt.shape[-1]
    pairs = width // LANES
    return pl.pallas_call(
        _sb_kernel,
        grid=(b, pairs, s // blk),
        in_specs=[
            pl.BlockSpec((None, blk, LANES), lambda bi, hp, i: (bi, i, hp)),
            pl.BlockSpec((None, None, s // blk, LANES, blk), lambda bi, hp, i: (bi, hp, 0, 0, 0)),
            pl.BlockSpec((None, s, LANES), lambda bi, hp, i: (bi, 0, hp)),
            _const_spec(tri.shape),
            _const_spec((1, LANES)),
            _const_spec((1, LANES)),
        ],
        out_specs=pl.BlockSpec((None, blk, LANES), lambda bi, hp, i: (bi, i, hp)),
        out_shape=jax.ShapeDtypeStruct((b, s, width), BF16),
        scratch_shapes=[pltpu.VMEM((2, blk, LANES), F32), pltpu.VMEM((2, blk, 1), F32)],
        compiler_params=_params("parallel", "parallel", "arbitrary"),
        name="sb_attention",
    )(q, kt, v, tri, qg, kg)


def _ret_kernel(q_ref, k_ref, v_ref, g_ref, gain_ref, decay_ref, zeta_ref, xi_ref, gc_ref,
                o_ref, state_ref):
    @pl.when(pl.program_id(2) == 0)
    def _():
        state_ref[...] = jnp.zeros_like(state_ref)

    c = RET_CHUNK
    decay = decay_ref[...]
    zeta = zeta_ref[...]
    xi = xi_ref[...]
    g_chunk = gc_ref[...]
    gain = gain_ref[...]
    for n in range(q_ref.shape[0] // c):
        rows = slice(n * c, (n + 1) * c)
        q = q_ref[rows, :]
        k = k_ref[rows, :]
        v = v_ref[rows, :]
        state = state_ref[...]
        scores = lax.dot_general(q, k, (((1,), (1,)), ((), ())), preferred_element_type=F32) * decay
        inner = jnp.dot(scores.astype(BF16), v, preferred_element_type=F32)
        cross = jnp.dot(q, state.astype(BF16), preferred_element_type=F32) * xi
        kz = (k.astype(F32) * zeta).astype(BF16)
        contrib = lax.dot_general(kz, v, (((0,), (0,)), ((), ())), preferred_element_type=F32)
        state_ref[...] = g_chunk * state + contrib
        ret = _rms_rows(inner + cross, gain)
        o_ref[rows, :] = (ret * _silu(g_ref[rows, :].astype(F32))).astype(o_ref.dtype)


def _retention(q, k, v, g, gain, decay, zeta, xi, gc):
    b, s, width = q.shape
    rows = min(RET_GROUP, s)
    blk = pl.BlockSpec((None, rows, LANES), lambda bi, h, n: (bi, n, h))
    head = lambda shape: pl.BlockSpec((None,) + shape, lambda bi, h, n: (h, 0, 0))
    return pl.pallas_call(
        _ret_kernel,
        grid=(b, width // LANES, s // rows),
        in_specs=[blk, blk, blk, blk, _const_spec((1, LANES)),
                  head((RET_CHUNK, RET_CHUNK)), head((RET_CHUNK, 1)), head((RET_CHUNK, 1)),
                  head((1, 1))],
        out_specs=blk,
        out_shape=jax.ShapeDtypeStruct((b, s, width), BF16),
        scratch_shapes=[pltpu.VMEM((RET_HEAD_DIM, RET_HEAD_DIM), F32)],
        compiler_params=_params("parallel", "parallel", "arbitrary"),
        name="retention",
    )(q, k, v, g, gain, decay, zeta, xi, gc)


def _even_out_kernel(h_ref, a_ref, b_ref, wa_ref, wb_ref, o_ref):
    mix = jnp.dot(a_ref[...], wa_ref[...], preferred_element_type=F32)
    mix += jnp.dot(b_ref[...], wb_ref[...], preferred_element_type=F32)
    o_ref[...] = h_ref[...] + mix


def _even_out(h2, a, b, wa, wb):
    t, d = h2.shape
    tm = min(ROW_TILE, t)
    row = lambda i: (i, 0)
    return pl.pallas_call(
        _even_out_kernel,
        grid=(t // tm,),
        in_specs=[pl.BlockSpec((tm, d), row), pl.BlockSpec((tm, a.shape[1]), row),
                  pl.BlockSpec((tm, b.shape[1]), row), _const_spec(wa.shape), _const_spec(wb.shape)],
        out_specs=pl.BlockSpec((tm, d), row),
        out_shape=jax.ShapeDtypeStruct((t, d), F32),
        compiler_params=_params("parallel"),
        name="even_out",
    )(h2, a, b, wa, wb)


def _odd_out_kernel(h_ref, a_ref, w_ref, o_ref):
    o_ref[...] = h_ref[...] + jnp.dot(a_ref[...], w_ref[...], preferred_element_type=F32)


def _odd_out(h2, a, w):
    t, d = h2.shape
    tm = min(ROW_TILE, t)
    row = lambda i: (i, 0)
    return pl.pallas_call(
        _odd_out_kernel,
        grid=(t // tm,),
        in_specs=[pl.BlockSpec((tm, d), row), pl.BlockSpec((tm, a.shape[1]), row), _const_spec(w.shape)],
        out_specs=pl.BlockSpec((tm, d), row),
        out_shape=jax.ShapeDtypeStruct((t, d), F32),
        compiler_params=_params("parallel"),
        name="odd_out",
    )(h2, a, w)


def _ffn_kernel(h_ref, g_ref, wu_ref, wv_ref, cw_ref, cb_ref, wd_ref, o_ref,
                ubuf_ref, prev_ref, gate_ref, *, tiles_per_seq):
    tm = h_ref.shape[0]
    d_ff = wu_ref.shape[1]
    halo = SUBLANES

    @pl.when(pl.program_id(0) % tiles_per_seq == 0)
    def _():
        prev_ref[...] = jnp.zeros_like(prev_ref)

    h = h_ref[...]
    hn = _rms_rows(h, g_ref[...]).astype(BF16)
    for c0 in range(0, d_ff, FF_CHUNK):
        cols = slice(c0, c0 + FF_CHUNK)
        u = jnp.dot(hn, wu_ref[:, cols], preferred_element_type=F32)
        v = jnp.dot(hn, wv_ref[:, cols], preferred_element_type=F32)
        ubuf_ref[0:halo, :] = prev_ref[:, cols]
        ubuf_ref[halo:halo + tm, :] = u
        prev_ref[:, cols] = u[tm - halo:, :]
        uc = (cb_ref[:, cols]
              + cw_ref[0:1, cols] * ubuf_ref[halo - 2:halo - 2 + tm, :]
              + cw_ref[1:2, cols] * ubuf_ref[halo - 1:halo - 1 + tm, :]
              + cw_ref[2:3, cols] * u)
        gate_ref[:, cols] = (_silu(uc) * v).astype(BF16)
    o_ref[...] = h + jnp.dot(gate_ref[...], wd_ref[...], preferred_element_type=F32)


def _ffn(h2, g, wu, wv, cw, cb, wd, seq):
    t, d = h2.shape
    d_ff = wu.shape[1]
    tm = min(ROW_TILE, seq)
    row = lambda i: (i, 0)
    return pl.pallas_call(
        functools.partial(_ffn_kernel, tiles_per_seq=seq // tm),
        grid=(t // tm,),
        in_specs=[pl.BlockSpec((tm, d), row), _const_spec((1, d)), _const_spec(wu.shape),
                  _const_spec(wv.shape), _const_spec(cw.shape), _const_spec(cb.shape),
                  _const_spec(wd.shape)],
        out_specs=pl.BlockSpec((tm, d), row),
        out_shape=jax.ShapeDtypeStruct((t, d), F32),
        scratch_shapes=[pltpu.VMEM((tm + SUBLANES, FF_CHUNK), F32),
                        pltpu.VMEM((SUBLANES, d_ff), F32),
                        pltpu.VMEM((tm, d_ff), BF16)],
        compiler_params=_params("arbitrary"),
        name="conv_ffn",
    )(h2, g, wu, wv, cw, cb, wd)


def _log_sigmoid(x):
    return jnp.minimum(x, 0.0) - jnp.log(1.0 + jnp.exp(-jnp.abs(x)))


def _odd_in_kernel(x_ref, g_ref, w_ref, wa_ref, walpha_ref, balpha_ref,
                   q_ref, k_ref, v_ref, r_ref, la_ref):
    hn = _rms_rows(x_ref[...], g_ref[...]).astype(BF16)

    def proj(c0, width):
        return jnp.dot(hn, w_ref[:, c0:c0 + width], preferred_element_type=F32)

    q_ref[...] = (proj(0, GLA_K_WIDTH) * GLA_DK ** -0.5).astype(BF16)
    k_ref[...] = proj(GLA_K_WIDTH, GLA_K_WIDTH).astype(BF16)
    v_ref[...] = proj(2 * GLA_K_WIDTH, GLA_V_WIDTH).astype(BF16)
    r_ref[...] = proj(2 * GLA_K_WIDTH + GLA_V_WIDTH, GLA_V_WIDTH).astype(BF16)
    ga = jnp.dot(hn, wa_ref[...], preferred_element_type=F32)
    pre = jnp.dot(ga, walpha_ref[...], preferred_element_type=F32,
                  precision=lax.Precision.HIGHEST) + balpha_ref[...]
    la_ref[...] = _log_sigmoid(pre) * (LOG2E / GLA_GATE_TAU)


def _odd_in(x2, g, w, wa, walpha, balpha):
    t, d = x2.shape
    tm = min(ROW_TILE, t)
    row = lambda i: (i, 0)
    kout = jax.ShapeDtypeStruct((t, GLA_K_WIDTH), BF16)
    vout = jax.ShapeDtypeStruct((t, GLA_V_WIDTH), BF16)
    return pl.pallas_call(
        _odd_in_kernel,
        grid=(t // tm,),
        in_specs=[pl.BlockSpec((tm, d), row), _const_spec((1, d)), _const_spec(w.shape),
                  _const_spec(wa.shape), _const_spec(walpha.shape), _const_spec(balpha.shape)],
        out_specs=[pl.BlockSpec((tm, GLA_K_WIDTH), row), pl.BlockSpec((tm, GLA_K_WIDTH), row),
                   pl.BlockSpec((tm, GLA_V_WIDTH), row), pl.BlockSpec((tm, GLA_V_WIDTH), row),
                   pl.BlockSpec((tm, GLA_K_WIDTH), row)],
        out_shape=[kout, kout, vout, vout, jax.ShapeDtypeStruct((t, GLA_K_WIDTH), F32)],
        compiler_params=_params("parallel"),
        name="odd_in",
    )(x2, g, w, wa, walpha, balpha)


def _gla_kernel(q_ref, k_ref, v_ref, r_ref, la_ref, gain_ref, tri_ref, o_ref, state_ref):
    @pl.when(pl.program_id(2) == 0)
    def _():
        state_ref[...] = jnp.zeros_like(state_ref)

    c = GLA_CHUNK
    rows_total = q_ref.shape[0]
    la = la_ref[...]
    la_hi = la.astype(BF16)
    la_lo = (la - la_hi.astype(F32)).astype(BF16)
    tri = tri_ref[...]
    cum = (jnp.dot(tri, la_hi, preferred_element_type=F32)
           + jnp.dot(tri, la_lo, preferred_element_type=F32))
    ri = lax.broadcasted_iota(jnp.int32, (c, c), 0)
    ci = lax.broadcasted_iota(jnp.int32, (c, c), 1)
    causal = ci <= ri
    gain = gain_ref[...]
    for n in range(rows_total // c):
        rows = slice(n * c, (n + 1) * c)
        cum_n = cum[n * c:(n + 1) * c, :]
        cum_last = cum_n[c - 1:c, :]
        q = q_ref[rows, :].astype(F32)
        k = k_ref[rows, :].astype(F32)
        v = v_ref[rows, :]
        q_t = (q * jnp.exp2(cum_n)).astype(BF16)
        k_t = (k * jnp.exp2(-cum_n)).astype(BF16)
        k_d = (k * jnp.exp2(cum_last - cum_n)).astype(BF16)
        a = lax.dot_general(q_t, k_t, (((1,), (1,)), ((), ())), preferred_element_type=F32)
        a = jnp.where(causal, a, 0.0)
        intra = jnp.dot(a.astype(BF16), v, preferred_element_type=F32)
        state = state_ref[...]
        inter = lax.dot_general(q_t, state.astype(BF16), (((1,), (1,)), ((), ())),
                                preferred_element_type=F32)
        contrib = lax.dot_general(v, k_d, (((0,), (0,)), ((), ())), preferred_element_type=F32)
        state_ref[...] = jnp.exp2(cum_last) * state + contrib
        o = _rms_rows(intra + inter, gain)
        o_ref[rows, :] = (o * _silu(r_ref[rows, :].astype(F32))).astype(o_ref.dtype)


def _gla(q, k, v, r, la, gain, tri):
    b, s, kw = q.shape
    vw = v.shape[-1]
    rows = min(GLA_GROUP, s)
    kblk = pl.BlockSpec((None, rows, GLA_DK), lambda bi, h, n: (bi, n, h))
    vblk = pl.BlockSpec((None, rows, GLA_DV), lambda bi, h, n: (bi, n, h))
    return pl.pallas_call(
        _gla_kernel,
        grid=(b, kw // GLA_DK, s // rows),
        in_specs=[kblk, kblk, vblk, vblk, kblk, _const_spec((1, GLA_DV)), _const_spec(tri.shape)],
        out_specs=vblk,
        out_shape=jax.ShapeDtypeStruct((b, s, vw), BF16),
        scratch_shapes=[pltpu.VMEM((GLA_DV, GLA_DK), F32)],
        compiler_params=_params("parallel", "parallel", "arbitrary"),
        name="gla",
    )(q, k, v, r, la, gain, tri)


def _rope_tables(seq):
    half = RET_HEAD_DIM // 2
    inv = ROPE_BASE ** (-jnp.arange(half, dtype=F32) / half)
    ang = jnp.arange(seq, dtype=F32)[:, None] * inv[None, :]
    cos, sin = jnp.cos(ang), jnp.sin(ang)
    return jnp.concatenate([cos, cos], axis=-1), jnp.concatenate([-sin, sin], axis=-1)


def _retention_tables():
    c = RET_CHUNK
    lg = jnp.log(1.0 - jnp.exp2(-5.0 - jnp.arange(RET_HEADS, dtype=F32)))[:, None]
    pos = jnp.arange(c, dtype=F32)
    diff = pos[:, None] - pos[None, :]
    decay = jnp.where(diff >= 0, jnp.exp(lg[:, :, None] * jnp.maximum(diff, 0.0)), 0.0)
    zeta = jnp.exp(lg * (c - 1 - pos))[:, :, None]
    xi = jnp.exp(lg * (pos + 1.0))[:, :, None]
    g_chunk = jnp.exp(lg * c)[:, :, None]
    return decay, zeta, xi, g_chunk


def _suffix_tri(n):
    idx = jnp.arange(n)
    return (idx[:, None] >= idx[None, :]).astype(BF16)


def _chunk_prefix_tri(rows, chunk):
    idx = jnp.arange(rows)
    same = (idx[:, None] // chunk) == (idx[None, :] // chunk)
    return (same & (idx[None, :] <= idx[:, None])).astype(BF16)


def kernel(x, mix_norm_g, even_w_in, sb_q_gain, sb_k_gain, ret_out_gain, even_w_out,
           odd_w_in, gla_w_alpha, gla_b_alpha, gla_out_gain, odd_w_out,
           ffn_norm_g, ffn_w_up, ffn_conv_w, ffn_conv_b, ffn_w_down):
    b, s, d = x.shape
    t = b * s
    depth = mix_norm_g.shape[0]
    d_ff = ffn_w_down.shape[1]
    h = x.reshape(t, d)

    cos2, sin2 = _rope_tables(s)
    decay, zeta, xi, g_chunk = _retention_tables()
    sb_blk = min(SB_BLOCK, s)
    sb_tri = _suffix_tri(sb_blk)
    gla_tri = _chunk_prefix_tri(min(GLA_GROUP, s), GLA_CHUNK)

    for layer in range(depth):
        g_mix = mix_norm_g[layer][None, :]
        if layer % 2 == 0:
            e = layer // 2
            qg = jnp.tile(sb_q_gain[e], 2)[None, :]
            kg = jnp.tile(sb_k_gain[e], 2)[None, :]
            sbq, sbk, sbv, rq, rk, rv, rg = _even_in(
                h, g_mix, even_w_in[e].astype(BF16), qg, kg, cos2, sin2, s)
            pairs = SB_WIDTH // LANES
            kt = sbk.reshape(b, s // sb_blk, sb_blk, pairs, LANES).transpose(0, 3, 1, 4, 2)
            out_a = _sb_attention(sbq.reshape(b, s, SB_WIDTH), kt, sbv.reshape(b, s, SB_WIDTH), sb_tri,
                                  qg, kg)
            shp = (b, s, RET_WIDTH)
            out_b = _retention(rq.reshape(shp), rk.reshape(shp), rv.reshape(shp), rg.reshape(shp),
                               ret_out_gain[e][None, :], decay, zeta, xi, g_chunk)
            w_out = even_w_out[e].astype(BF16)
            h = _even_out(h, out_a.reshape(t, SB_WIDTH), out_b.reshape(t, RET_WIDTH),
                          w_out[:SB_WIDTH], w_out[SB_WIDTH:])
        else:
            o = layer // 2
            w_in = odd_w_in[o]
            n_main = 2 * GLA_K_WIDTH + 2 * GLA_V_WIDTH
            wa = jnp.pad(w_in[:, n_main:], ((0, 0), (0, LANES - GLA_GATE_RANK))).astype(BF16)
            walpha = jnp.pad(gla_w_alpha[o], ((0, LANES - GLA_GATE_RANK), (0, 0)))
            gq, gk, gv, gr, la = _odd_in(h, g_mix, w_in[:, :n_main].astype(BF16), wa, walpha,
                                         gla_b_alpha[o][None, :])
            og = _gla(gq.reshape(b, s, GLA_K_WIDTH), gk.reshape(b, s, GLA_K_WIDTH),
                      gv.reshape(b, s, GLA_V_WIDTH), gr.reshape(b, s, GLA_V_WIDTH),
                      la.reshape(b, s, GLA_K_WIDTH), gla_out_gain[o][None, :], gla_tri)
            h = _odd_out(h, og.reshape(t, GLA_V_WIDTH), odd_w_out[o].astype(BF16))
        w_up = ffn_w_up[layer].astype(BF16)
        h = _ffn(h, ffn_norm_g[layer][None, :], w_up[:, :d_ff], w_up[:, d_ff:],
                 ffn_conv_w[layer], ffn_conv_b[layer][None, :], ffn_w_down[layer].astype(BF16), s)
    return h.reshape(b, s, d)
```

```python
import functools

import jax
import jax.numpy as jnp
from jax import lax
from jax.experimental import pallas as pl
from jax.experimental.pallas import tpu as pltpu

F32 = jnp.float32
BF16 = jnp.bfloat16

EPS = 1e-6
LOG2E = 1.4426950408889634
ROPE_BASE = 10000.0
EXP2_UNDERFLOW = 150.0
SB_NORM_SLACK = 1.05

SB_HEADS = 8
SB_HEAD_DIM = 64
SB_WIDTH = SB_HEADS * SB_HEAD_DIM
RET_HEADS = 4
RET_HEAD_DIM = 128
RET_WIDTH = RET_HEADS * RET_HEAD_DIM
RET_CHUNK = 128
GLA_HEADS = 4
GLA_DK = 128
GLA_DV = 256
GLA_K_WIDTH = GLA_HEADS * GLA_DK
GLA_V_WIDTH = GLA_HEADS * GLA_DV
GLA_GATE_RANK = 16
GLA_GATE_TAU = 16.0
GLA_CHUNK = 64
CONV_WIDTH = 3

LANES = 128
SUBLANES = 8
MXU_DIM = 256
VMEM_LIMIT = 56 * 1024 * 1024

ROW_TILE = 512
SB_BLOCK = MXU_DIM
RET_GROUP = 1024
RET_SPAN = 2 * RET_CHUNK
GLA_GROUP = ROW_TILE
GLA_SPAN = 4 * GLA_CHUNK
FF_CHUNK = 256


def _params(*sem):
    return pltpu.CompilerParams(dimension_semantics=sem, vmem_limit_bytes=VMEM_LIMIT)


def _const_spec(shape):
    n = len(shape)
    return pl.BlockSpec(shape, lambda *_: (0,) * n)


def _rms_rows(x, g):
    ms = jnp.mean(x * x, axis=-1, keepdims=True)
    return x * lax.rsqrt(ms + EPS) * g


def _silu(x):
    return x / (1.0 + jnp.exp(-x))


def _half_head_rms(blk):
    sq = blk * blk
    lane = lax.broadcasted_iota(jnp.int32, blk.shape, 1)
    lo = lane < SB_HEAD_DIM
    s_lo = jnp.sum(jnp.where(lo, sq, 0.0), axis=-1, keepdims=True)
    s_hi = jnp.sum(jnp.where(lo, 0.0, sq), axis=-1, keepdims=True)
    ms = jnp.where(lo, s_lo, s_hi) * (1.0 / SB_HEAD_DIM)
    return blk * lax.rsqrt(ms + EPS)


def _even_in_kernel(x_ref, g_ref, w_ref, qg_ref, kg_ref, cos_ref, sin_ref,
                    sbq_ref, sbkt_ref, sbv_ref, rq_ref, rkt_ref, rv_ref, rg_ref):
    tm = x_ref.shape[0]
    hn = _rms_rows(x_ref[...], g_ref[...]).astype(BF16)

    def proj(seg):
        return jnp.dot(hn, w_ref[:, seg * SB_WIDTH:(seg + 1) * SB_WIDTH],
                       preferred_element_type=F32)

    q_scale = SB_HEAD_DIM ** -0.5 * LOG2E
    p = proj(0)
    for hp in range(SB_WIDTH // LANES):
        sl = slice(hp * LANES, (hp + 1) * LANES)
        sbq_ref[:, sl] = (_half_head_rms(p[:, sl]) * (qg_ref[...] * q_scale)).astype(BF16)
    p = proj(1)
    blk = sbkt_ref.shape[-1]
    for hp in range(SB_WIDTH // LANES):
        sl = slice(hp * LANES, (hp + 1) * LANES)
        kt = (_half_head_rms(p[:, sl]) * kg_ref[...]).T
        for c in range(tm // blk):
            sbkt_ref[hp, c] = kt[:, c * blk:(c + 1) * blk].astype(BF16)
    sbv_ref[...] = proj(2).astype(BF16)

    cos2 = cos_ref[...]
    sin2 = sin_ref[...]

    def rope(blk):
        return blk * cos2 + pltpu.roll(blk, RET_HEAD_DIM // 2, 1) * sin2

    p = proj(3)
    for h in range(RET_HEADS):
        sl = slice(h * LANES, (h + 1) * LANES)
        rq_ref[:, sl] = rope(p[:, sl]).astype(BF16)
    p = proj(4)
    for h in range(RET_HEADS):
        sl = slice(h * LANES, (h + 1) * LANES)
        rkt_ref[h] = (rope(p[:, sl]) * RET_HEAD_DIM ** -0.5).T.astype(BF16)
    rv_ref[...] = proj(5).astype(BF16)
    rg_ref[...] = proj(6).astype(BF16)


def _even_in(x2, g, w, qg, kg, cos2, sin2, batch, seq, sb_blk):
    t, d = x2.shape
    tm = min(ROW_TILE, seq)
    tps = seq // tm
    row = lambda i: (i, 0)
    pairs = SB_WIDTH // LANES
    flat = jax.ShapeDtypeStruct((t, SB_WIDTH), BF16)
    flat_spec = pl.BlockSpec((tm, SB_WIDTH), row)
    sbkt = jax.ShapeDtypeStruct((batch, pairs, seq // sb_blk, LANES, sb_blk), BF16)
    sbkt_spec = pl.BlockSpec((None, pairs, tm // sb_blk, LANES, sb_blk),
                             lambda i: (i // tps, 0, i % tps, 0, 0))
    rkt = jax.ShapeDtypeStruct((batch, RET_HEADS, RET_HEAD_DIM, seq), BF16)
    rkt_spec = pl.BlockSpec((None, RET_HEADS, RET_HEAD_DIM, tm), lambda i: (i // tps, 0, 0, i % tps))
    return pl.pallas_call(
        _even_in_kernel,
        grid=(t // tm,),
        in_specs=[
            pl.BlockSpec((tm, d), row),
            _const_spec((1, d)),
            _const_spec(w.shape),
            _const_spec((1, LANES)),
            _const_spec((1, LANES)),
            pl.BlockSpec((tm, LANES), lambda i: (i % tps, 0)),
            pl.BlockSpec((tm, LANES), lambda i: (i % tps, 0)),
        ],
        out_specs=[flat_spec, sbkt_spec, flat_spec, flat_spec, rkt_spec, flat_spec, flat_spec],
        out_shape=[flat, sbkt, flat, flat, rkt, flat, flat],
        compiler_params=_params("parallel"),
        name="even_in",
    )(x2, g, w, qg, kg, cos2, sin2)


def _softplus2(z):
    return jnp.maximum(z, 0.0) + jnp.log2(1.0 + jnp.exp2(-jnp.abs(z)))


def _sb_kernel(q_ref, kt_ref, v_ref, tri_ref, qg_ref, kg_ref, o_ref, acc_ref, carry_ref):
    i = pl.program_id(2)
    blk = q_ref.shape[0]
    z_bound = (SB_NORM_SLACK * SB_HEAD_DIM * SB_HEAD_DIM ** -0.5 * LOG2E
               * jnp.max(jnp.abs(qg_ref[...])) * jnp.max(jnp.abs(kg_ref[...])))
    dead_above = z_bound + EXP2_UNDERFLOW
    q = q_ref[...]
    lane = lax.broadcasted_iota(jnp.int32, q.shape, 1)
    lo = lane < SB_HEAD_DIM
    zero = jnp.zeros_like(q)
    q_heads = (jnp.where(lo, q, zero), jnp.where(lo, zero, q))
    tri = tri_ref[...]

    def pair(j_a, diagonal):
        has_b = j_a > 0
        j_b = jnp.maximum(j_a - 1, 0)
        kt_a = kt_ref[j_a]
        kt_b = kt_ref[j_b]
        v_a = v_ref[pl.ds(pl.multiple_of(j_a * blk, blk), blk), :]
        v_b = v_ref[pl.ds(pl.multiple_of(j_b * blk, blk), blk), :]
        if diagonal:
            r = lax.broadcasted_iota(jnp.int32, (blk, blk), 0)
            c = lax.broadcasted_iota(jnp.int32, (blk, blk), 1)
            mask = c < r
        for h in range(2):
            z_a = jnp.dot(q_heads[h], kt_a, preferred_element_type=F32)
            z_b = jnp.dot(q_heads[h], kt_b, preferred_element_type=F32)
            sp_a = _softplus2(z_a)
            sp_b = _softplus2(z_b)
            if diagonal:
                sp_a = jnp.where(mask, sp_a, 0.0)
            cs_a = jnp.dot(sp_a.astype(BF16), tri, preferred_element_type=F32)
            cs_b = jnp.dot(sp_b.astype(BF16), tri, preferred_element_type=F32)
            tot_a = jnp.sum(sp_a, axis=-1, keepdims=True)
            tot_b = jnp.where(has_b, jnp.sum(sp_b, axis=-1, keepdims=True), 0.0)
            if diagonal:
                w_a = jnp.where(mask, jnp.exp2(z_a - cs_a), 0.0)
                before_b = tot_a
            else:
                carry = carry_ref[h]
                w_a = jnp.exp2(z_a - (cs_a + carry))
                before_b = carry + tot_a
            w_b = jnp.where(has_b, jnp.exp2(z_b - (cs_b + before_b)), 0.0)
            pv = (jnp.dot(w_a.astype(BF16), v_a, preferred_element_type=F32)
                  + jnp.dot(w_b.astype(BF16), v_b, preferred_element_type=F32))
            if diagonal:
                acc_ref[h] = pv
            else:
                acc_ref[h] += pv
            carry_ref[h] = before_b + tot_b

    def live():
        return jnp.minimum(jnp.min(carry_ref[0]), jnp.min(carry_ref[1])) <= dead_above

    pair(i, True)

    def body(c):
        j_a, _ = c
        pair(j_a, False)
        return j_a - 2, live()

    lax.while_loop(lambda c: jnp.logical_and(c[0] >= 0, c[1]), body, (i - 2, live()))
    o_ref[...] = jnp.where(lo, acc_ref[0], acc_ref[1]).astype(o_ref.dtype)


def _sb_attention(q, kt, v, tri, qg, kg):
    b, s, width = q.shape
    blk = kt.shape[-1]
    pairs = width // LANES
    return pl.pallas_call(
        _sb_kernel,
        grid=(b, pairs, s // blk),
        in_specs=[
            pl.BlockSpec((None, blk, LANES), lambda bi, hp, i: (bi, i, hp)),
            pl.BlockSpec((None, None, s // blk, LANES, blk), lambda bi, hp, i: (bi, hp, 0, 0, 0)),
            pl.BlockSpec((None, s, LANES), lambda bi, hp, i: (bi, 0, hp)),
            _const_spec(tri.shape),
            _const_spec((1, LANES)),
            _const_spec((1, LANES)),
        ],
        out_specs=pl.BlockSpec((None, blk, LANES), lambda bi, hp, i: (bi, i, hp)),
        out_shape=jax.ShapeDtypeStruct((b, s, width), BF16),
        scratch_shapes=[pltpu.VMEM((2, blk, LANES), F32), pltpu.VMEM((2, blk, 1), F32)],
        compiler_params=_params("parallel", "parallel", "arbitrary"),
        name="sb_attention",
    )(q, kt, v, tri, qg, kg)


def _ret_kernel(q_ref, kt_ref, v_ref, g_ref, gain_ref, decay_ref, zeta_ref, xi_ref, gc_ref,
                o_ref, state_ref):
    @pl.when(pl.program_id(2) == 0)
    def _():
        state_ref[...] = jnp.zeros_like(state_ref)

    c = RET_CHUNK
    span = decay_ref.shape[0]
    decay = decay_ref[...]
    zeta = zeta_ref[...]
    xi = xi_ref[...]
    g_chunk = gc_ref[...]
    gain = gain_ref[...]
    state = state_ref[...]
    for m in range(q_ref.shape[0] // span):
        rows = slice(m * span, (m + 1) * span)
        q = q_ref[rows, :]
        kt = kt_ref[:, rows]
        v = v_ref[rows, :]
        scores = jnp.dot(q, kt, preferred_element_type=F32) * decay
        inner = jnp.dot(scores.astype(BF16), v, preferred_element_type=F32)
        for n in range(span // c):
            sub = slice(n * c, (n + 1) * c)
            cross = jnp.dot(q[sub], state.astype(BF16), preferred_element_type=F32) * xi
            kz = (kt[:, sub].astype(F32) * zeta).astype(BF16)
            state = g_chunk * state + jnp.dot(kz, v[sub], preferred_element_type=F32)
            ret = _rms_rows(inner[sub] + cross, gain)
            out_rows = slice(m * span + n * c, m * span + (n + 1) * c)
            o_ref[out_rows, :] = (ret * _silu(g_ref[out_rows, :].astype(F32))).astype(o_ref.dtype)
    state_ref[...] = state


def _retention(q, kt, v, g, gain, decay, zeta, xi, gc):
    b, s, width = q.shape
    rows = min(RET_GROUP, s)
    span = decay.shape[-1]
    blk = pl.BlockSpec((None, rows, LANES), lambda bi, h, n: (bi, n, h))
    head = lambda shape: pl.BlockSpec((None,) + shape, lambda bi, h, n: (h, 0, 0))
    return pl.pallas_call(
        _ret_kernel,
        grid=(b, width // LANES, s // rows),
        in_specs=[blk,
                  pl.BlockSpec((None, None, RET_HEAD_DIM, rows), lambda bi, h, n: (bi, h, 0, n)),
                  blk, blk, _const_spec((1, LANES)),
                  head((span, span)), head((1, RET_CHUNK)), head((RET_CHUNK, 1)), head((1, 1))],
        out_specs=blk,
        out_shape=jax.ShapeDtypeStruct((b, s, width), BF16),
        scratch_shapes=[pltpu.VMEM((RET_HEAD_DIM, RET_HEAD_DIM), F32)],
        compiler_params=_params("parallel", "parallel", "arbitrary"),
        name="retention",
    )(q, kt, v, g, gain, decay, zeta, xi, gc)


def _even_out_kernel(h_ref, a_ref, b_ref, wa_ref, wb_ref, o_ref):
    mix = jnp.dot(a_ref[...], wa_ref[...], preferred_element_type=F32)
    mix += jnp.dot(b_ref[...], wb_ref[...], preferred_element_type=F32)
    o_ref[...] = h_ref[...] + mix


def _even_out(h2, a, b, wa, wb):
    t, d = h2.shape
    tm = min(ROW_TILE, t)
    row = lambda i: (i, 0)
    return pl.pallas_call(
        _even_out_kernel,
        grid=(t // tm,),
        in_specs=[pl.BlockSpec((tm, d), row), pl.BlockSpec((tm, a.shape[1]), row),
                  pl.BlockSpec((tm, b.shape[1]), row), _const_spec(wa.shape), _const_spec(wb.shape)],
        out_specs=pl.BlockSpec((tm, d), row),
        out_shape=jax.ShapeDtypeStruct((t, d), F32),
        compiler_params=_params("parallel"),
        name="even_out",
    )(h2, a, b, wa, wb)


def _odd_out_kernel(h_ref, a_ref, w_ref, o_ref):
    o_ref[...] = h_ref[...] + jnp.dot(a_ref[...], w_ref[...], preferred_element_type=F32)


def _odd_out(h2, a, w):
    t, d = h2.shape
    tm = min(ROW_TILE, t)
    row = lambda i: (i, 0)
    return pl.pallas_call(
        _odd_out_kernel,
        grid=(t // tm,),
        in_specs=[pl.BlockSpec((tm, d), row), pl.BlockSpec((tm, a.shape[1]), row), _const_spec(w.shape)],
        out_specs=pl.BlockSpec((tm, d), row),
        out_shape=jax.ShapeDtypeStruct((t, d), F32),
        compiler_params=_params("parallel"),
        name="odd_out",
    )(h2, a, w)


def _ffn_kernel(h_ref, g_ref, wu_ref, wv_ref, cw_ref, cb_ref, wd_ref, o_ref,
                ubuf_ref, prev_ref, gate_ref, *, tiles_per_seq):
    tm = h_ref.shape[0]
    d_ff = wu_ref.shape[1]
    halo = SUBLANES

    @pl.when(pl.program_id(0) % tiles_per_seq == 0)
    def _():
        prev_ref[...] = jnp.zeros_like(prev_ref)

    h = h_ref[...]
    hn = _rms_rows(h, g_ref[...]).astype(BF16)
    for c0 in range(0, d_ff, FF_CHUNK):
        cols = slice(c0, c0 + FF_CHUNK)
        u = jnp.dot(hn, wu_ref[:, cols], preferred_element_type=F32)
        v = jnp.dot(hn, wv_ref[:, cols], preferred_element_type=F32)
        ubuf_ref[0:halo, :] = prev_ref[:, cols]
        ubuf_ref[halo:halo + tm, :] = u
        prev_ref[:, cols] = u[tm - halo:, :]
        uc = (cb_ref[:, cols]
              + cw_ref[0:1, cols] * ubuf_ref[halo - 2:halo - 2 + tm, :]
              + cw_ref[1:2, cols] * ubuf_ref[halo - 1:halo - 1 + tm, :]
              + cw_ref[2:3, cols] * u)
        gate_ref[:, cols] = (_silu(uc) * v).astype(BF16)
    o_ref[...] = h + jnp.dot(gate_ref[...], wd_ref[...], preferred_element_type=F32)


def _ffn(h2, g, wu, wv, cw, cb, wd, seq):
    t, d = h2.shape
    d_ff = wu.shape[1]
    tm = min(ROW_TILE, seq)
    row = lambda i: (i, 0)
    return pl.pallas_call(
        functools.partial(_ffn_kernel, tiles_per_seq=seq // tm),
        grid=(t // tm,),
        in_specs=[pl.BlockSpec((tm, d), row), _const_spec((1, d)), _const_spec(wu.shape),
                  _const_spec(wv.shape), _const_spec(cw.shape), _const_spec(cb.shape),
                  _const_spec(wd.shape)],
        out_specs=pl.BlockSpec((tm, d), row),
        out_shape=jax.ShapeDtypeStruct((t, d), F32),
        scratch_shapes=[pltpu.VMEM((tm + SUBLANES, FF_CHUNK), F32),
                        pltpu.VMEM((SUBLANES, d_ff), F32),
                        pltpu.VMEM((tm, d_ff), BF16)],
        compiler_params=_params("arbitrary"),
        name="conv_ffn",
    )(h2, g, wu, wv, cw, cb, wd)


def _log_sigmoid(x):
    return jnp.minimum(x, 0.0) - jnp.log(1.0 + jnp.exp(-jnp.abs(x)))


def _odd_in_kernel(x_ref, g_ref, w_ref, wa_ref, walpha_ref, balpha_ref, tri_ref,
                   qt_ref, ktt_ref, kdt_ref, v_ref, r_ref, dec_ref):
    tm = x_ref.shape[0]
    hn = _rms_rows(x_ref[...], g_ref[...]).astype(BF16)

    def proj(c0, width):
        return jnp.dot(hn, w_ref[:, c0:c0 + width], preferred_element_type=F32)

    ga = jnp.dot(hn, wa_ref[...], preferred_element_type=F32)
    pre = jnp.dot(ga, walpha_ref[...], preferred_element_type=F32,
                  precision=lax.Precision.HIGHEST) + balpha_ref[...]
    la = _log_sigmoid(pre) * (LOG2E / GLA_GATE_TAU)
    la_hi = la.astype(BF16)
    la_lo = (la - la_hi.astype(F32)).astype(BF16)
    tri = tri_ref[...]
    cum = (jnp.dot(tri, la_hi, preferred_element_type=F32)
           + jnp.dot(tri, la_lo, preferred_element_type=F32))
    chunks = tm // GLA_CHUNK
    last = cum.reshape(chunks, GLA_CHUNK, GLA_K_WIDTH)[:, GLA_CHUNK - 1:GLA_CHUNK, :]
    dec_ref[...] = jnp.exp2(last).reshape(chunks, GLA_K_WIDTH)
    to_end = (jnp.broadcast_to(last, (chunks, GLA_CHUNK, GLA_K_WIDTH)).reshape(tm, GLA_K_WIDTH)
              - cum)

    qt_ref[...] = (proj(0, GLA_K_WIDTH) * GLA_DK ** -0.5 * jnp.exp2(cum)).astype(BF16)
    k = proj(GLA_K_WIDTH, GLA_K_WIDTH)
    kt = k * jnp.exp2(-cum)
    kd = k * jnp.exp2(to_end)
    for h in range(GLA_HEADS):
        sl = slice(h * GLA_DK, (h + 1) * GLA_DK)
        ktt_ref[h] = kt[:, sl].T.astype(BF16)
        kdt_ref[h] = kd[:, sl].T.astype(BF16)
    v_ref[...] = proj(2 * GLA_K_WIDTH, GLA_V_WIDTH).astype(BF16)
    r_ref[...] = proj(2 * GLA_K_WIDTH + GLA_V_WIDTH, GLA_V_WIDTH).astype(BF16)


def _odd_in(x2, g, w, wa, walpha, balpha, tri, batch, seq):
    t, d = x2.shape
    tm = tri.shape[0]
    tps = seq // tm
    row = lambda i: (i, 0)
    kt_shape = jax.ShapeDtypeStruct((batch, GLA_HEADS, GLA_DK, seq), BF16)
    kt_spec = pl.BlockSpec((None, GLA_HEADS, GLA_DK, tm), lambda i: (i // tps, 0, 0, i % tps))
    vout = jax.ShapeDtypeStruct((t, GLA_V_WIDTH), BF16)
    return pl.pallas_call(
        _odd_in_kernel,
        grid=(t // tm,),
        in_specs=[pl.BlockSpec((tm, d), row), _const_spec((1, d)), _const_spec(w.shape),
                  _const_spec(wa.shape), _const_spec(walpha.shape), _const_spec(balpha.shape),
                  _const_spec(tri.shape)],
        out_specs=[pl.BlockSpec((tm, GLA_K_WIDTH), row), kt_spec, kt_spec,
                   pl.BlockSpec((tm, GLA_V_WIDTH), row), pl.BlockSpec((tm, GLA_V_WIDTH), row),
                   pl.BlockSpec((tm // GLA_CHUNK, GLA_K_WIDTH), row)],
        out_shape=[jax.ShapeDtypeStruct((t, GLA_K_WIDTH), BF16), kt_shape, kt_shape, vout, vout,
                   jax.ShapeDtypeStruct((t // GLA_CHUNK, GLA_K_WIDTH), F32)],
        compiler_params=_params("parallel"),
        name="odd_in",
    )(x2, g, w, wa, walpha, balpha, tri)


def _gla_kernel(qt_ref, ktt_ref, kdt_ref, v_ref, r_ref, dec_ref, gain_ref, o_ref, state_ref):
    @pl.when(pl.program_id(2) == 0)
    def _():
        state_ref[...] = jnp.zeros_like(state_ref)

    c = GLA_CHUNK
    span = GLA_SPAN
    ri = lax.broadcasted_iota(jnp.int32, (span, span), 0)
    ci = lax.broadcasted_iota(jnp.int32, (span, span), 1)
    causal = jnp.logical_and(ci <= ri, ci // c == ri // c)
    col_chunk = lax.broadcasted_iota(jnp.int32, (1, span), 1) // c
    gain = gain_ref[...]
    dec_t = dec_ref[...].T
    state = state_ref[...]
    for m in range(qt_ref.shape[0] // span):
        rows = slice(m * span, (m + 1) * span)
        qt = qt_ref[rows, :]
        ktt = ktt_ref[:, rows]
        kdt = kdt_ref[:, rows]
        v = v_ref[rows, :]
        a = jnp.where(causal, jnp.dot(qt, ktt, preferred_element_type=F32), 0.0)
        intra = jnp.dot(a.astype(BF16), v, preferred_element_type=F32)
        for n in range(span // c):
            sub = slice(n * c, (n + 1) * c)
            idx = m * (span // c) + n
            inter = jnp.dot(qt[sub], state.astype(BF16), preferred_element_type=F32)
            kd_n = jnp.where(col_chunk == n, kdt, jnp.zeros_like(kdt))
            state = dec_t[:, idx:idx + 1] * state + jnp.dot(kd_n, v, preferred_element_type=F32)
            o = _rms_rows(intra[sub] + inter, gain)
            out_rows = slice(m * span + n * c, m * span + (n + 1) * c)
            o_ref[out_rows, :] = (o * _silu(r_ref[out_rows, :].astype(F32))).astype(o_ref.dtype)
    state_ref[...] = state


def _gla(qt, ktt, kdt, v, r, dec, gain):
    b, s, kw = qt.shape
    vw = v.shape[-1]
    rows = min(GLA_GROUP, s)
    qblk = pl.BlockSpec((None, rows, GLA_DK), lambda bi, h, n: (bi, n, h))
    tblk = pl.BlockSpec((None, None, GLA_DK, rows), lambda bi, h, n: (bi, h, 0, n))
    vblk = pl.BlockSpec((None, rows, GLA_DV), lambda bi, h, n: (bi, n, h))
    dblk = pl.BlockSpec((None, rows // GLA_CHUNK, GLA_DK), lambda bi, h, n: (bi, n, h))
    return pl.pallas_call(
        _gla_kernel,
        grid=(b, kw // GLA_DK, s // rows),
        in_specs=[qblk, tblk, tblk, vblk, vblk, dblk, _const_spec((1, GLA_DV))],
        out_specs=vblk,
        out_shape=jax.ShapeDtypeStruct((b, s, vw), BF16),
        scratch_shapes=[pltpu.VMEM((GLA_DK, GLA_DV), F32)],
        compiler_params=_params("parallel", "parallel", "arbitrary"),
        name="gla",
    )(qt, ktt, kdt, v, r, dec, gain)


def _rope_tables(seq):
    half = RET_HEAD_DIM // 2
    inv = ROPE_BASE ** (-jnp.arange(half, dtype=F32) / half)
    ang = jnp.arange(seq, dtype=F32)[:, None] * inv[None, :]
    cos, sin = jnp.cos(ang), jnp.sin(ang)
    return jnp.concatenate([cos, cos], axis=-1), jnp.concatenate([-sin, sin], axis=-1)


def _retention_tables(span):
    c = RET_CHUNK
    lg = jnp.log(1.0 - jnp.exp2(-5.0 - jnp.arange(RET_HEADS, dtype=F32)))[:, None]
    pos = jnp.arange(span, dtype=F32)
    diff = pos[:, None] - pos[None, :]
    same = (pos[:, None] // c) == (pos[None, :] // c)
    decay = jnp.where((diff >= 0) & same, jnp.exp(lg[:, :, None] * jnp.maximum(diff, 0.0)), 0.0)
    pos = jnp.arange(c, dtype=F32)
    zeta = jnp.exp(lg * (c - 1 - pos))[:, None, :]
    xi = jnp.exp(lg * (pos + 1.0))[:, :, None]
    g_chunk = jnp.exp(lg * c)[:, :, None]
    return decay, zeta, xi, g_chunk


def _suffix_tri(n):
    idx = jnp.arange(n)
    return (idx[:, None] >= idx[None, :]).astype(BF16)


def _chunk_prefix_tri(rows, chunk):
    idx = jnp.arange(rows)
    same = (idx[:, None] // chunk) == (idx[None, :] // chunk)
    return (same & (idx[None, :] <= idx[:, None])).astype(BF16)


def kernel(x, mix_norm_g, even_w_in, sb_q_gain, sb_k_gain, ret_out_gain, even_w_out,
           odd_w_in, gla_w_alpha, gla_b_alpha, gla_out_gain, odd_w_out,
           ffn_norm_g, ffn_w_up, ffn_conv_w, ffn_conv_b, ffn_w_down):
    b, s, d = x.shape
    t = b * s
    depth = mix_norm_g.shape[0]
    d_ff = ffn_w_down.shape[1]
    h = x.reshape(t, d)

    cos2, sin2 = _rope_tables(s)
    decay, zeta, xi, g_chunk = _retention_tables(min(RET_SPAN, s))
    sb_blk = min(SB_BLOCK, s)
    sb_tri = _suffix_tri(sb_blk)
    gla_tri = _chunk_prefix_tri(min(GLA_GROUP, s), GLA_CHUNK)

    for layer in range(depth):
        g_mix = mix_norm_g[layer][None, :]
        if layer % 2 == 0:
            e = layer // 2
            qg = jnp.tile(sb_q_gain[e], 2)[None, :]
            kg = jnp.tile(sb_k_gain[e], 2)[None, :]
            sbq, sbkt, sbv, rq, rkt, rv, rg = _even_in(
                h, g_mix, even_w_in[e].astype(BF16), qg, kg, cos2, sin2, b, s, sb_blk)
            out_a = _sb_attention(sbq.reshape(b, s, SB_WIDTH), sbkt, sbv.reshape(b, s, SB_WIDTH),
                                  sb_tri, qg, kg)
            shp = (b, s, RET_WIDTH)
            out_b = _retention(rq.reshape(shp), rkt, rv.reshape(shp), rg.reshape(shp),
                               ret_out_gain[e][None, :], decay, zeta, xi, g_chunk)
            w_out = even_w_out[e].astype(BF16)
            h = _even_out(h, out_a.reshape(t, SB_WIDTH), out_b.reshape(t, RET_WIDTH),
                          w_out[:SB_WIDTH], w_out[SB_WIDTH:])
        else:
            o = layer // 2
            w_in = odd_w_in[o]
            n_main = 2 * GLA_K_WIDTH + 2 * GLA_V_WIDTH
            wa = jnp.pad(w_in[:, n_main:], ((0, 0), (0, LANES - GLA_GATE_RANK))).astype(BF16)
            walpha = jnp.pad(gla_w_alpha[o], ((0, LANES - GLA_GATE_RANK), (0, 0)))
            qt, ktt, kdt, gv, gr, dec = _odd_in(h, g_mix, w_in[:, :n_main].astype(BF16), wa, walpha,
                                                gla_b_alpha[o][None, :], gla_tri, b, s)
            og = _gla(qt.reshape(b, s, GLA_K_WIDTH), ktt, kdt,
                      gv.reshape(b, s, GLA_V_WIDTH), gr.reshape(b, s, GLA_V_WIDTH),
                      dec.reshape(b, s // GLA_CHUNK, GLA_K_WIDTH), gla_out_gain[o][None, :])
            h = _odd_out(h, og.reshape(t, GLA_V_WIDTH), odd_w_out[o].astype(BF16))
        w_up = ffn_w_up[layer].astype(BF16)
        h = _ffn(h, ffn_norm_g[layer][None, :], w_up[:, :d_ff], w_up[:, d_ff:],
                 ffn_conv_w[layer], ffn_conv_b[layer][None, :], ffn_w_down[layer].astype(BF16), s)
    return h.reshape(b, s, d)
```

```python
import functools

import numpy as np

import jax
import jax.numpy as jnp
from jax import lax
from jax.experimental import pallas as pl
from jax.experimental.pallas import tpu as pltpu

F32 = jnp.float32
BF16 = jnp.bfloat16

EPS = 1e-6
LOG2E = 1.4426950408889634
ROPE_BASE = 10000.0
EXP2_UNDERFLOW = 150.0
SB_NORM_SLACK = 1.05

SB_HEADS = 8
SB_HEAD_DIM = 64
SB_WIDTH = SB_HEADS * SB_HEAD_DIM
RET_HEADS = 4
RET_HEAD_DIM = 128
RET_WIDTH = RET_HEADS * RET_HEAD_DIM
RET_CHUNK = 128
GLA_HEADS = 4
GLA_DK = 128
GLA_DV = 256
GLA_K_WIDTH = GLA_HEADS * GLA_DK
GLA_V_WIDTH = GLA_HEADS * GLA_DV
GLA_GATE_RANK = 16
GLA_GATE_TAU = 16.0
GLA_CHUNK = 64
CONV_WIDTH = 3

LANES = 128
SUBLANES = 8
MXU_DIM = 256
VMEM_LIMIT = 56 * 1024 * 1024

ROW_TILE = 512
SB_BLOCK = MXU_DIM
HEADS_PER_STEP = 2
RET_GROUP = 1024
RET_SPAN = 2 * RET_CHUNK
GLA_GROUP = ROW_TILE
GLA_SPAN = 4 * GLA_CHUNK
FF_CHUNK = 256


def _params(*sem):
    return pltpu.CompilerParams(dimension_semantics=sem, vmem_limit_bytes=VMEM_LIMIT)


def _const_spec(shape):
    n = len(shape)
    return pl.BlockSpec(shape, lambda *_: (0,) * n)


def _rms_rows(x, g):
    ms = jnp.mean(x * x, axis=-1, keepdims=True)
    return x * lax.rsqrt(ms + EPS) * g


def _silu(x):
    return x / (1.0 + jnp.exp(-x))


def _half_head_rms(blk):
    sq = blk * blk
    lane = lax.broadcasted_iota(jnp.int32, blk.shape, 1)
    lo = lane < SB_HEAD_DIM
    s_lo = jnp.sum(jnp.where(lo, sq, 0.0), axis=-1, keepdims=True)
    s_hi = jnp.sum(jnp.where(lo, 0.0, sq), axis=-1, keepdims=True)
    ms = jnp.where(lo, s_lo, s_hi) * (1.0 / SB_HEAD_DIM)
    return blk * lax.rsqrt(ms + EPS)


def _even_in_kernel(x_ref, g_ref, w_ref, qg_ref, kg_ref, ca_ref, sa_ref, cb_ref, sb_ref,
                    sbq_ref, sbkt_ref, sbv_ref, rq_ref, rkt_ref, rv_ref, rg_ref):
    tm = x_ref.shape[0]
    hn = _rms_rows(x_ref[...], g_ref[...]).astype(BF16)

    def proj(seg):
        return jnp.dot(hn, w_ref[:, seg * SB_WIDTH:(seg + 1) * SB_WIDTH],
                       preferred_element_type=F32)

    q_scale = SB_HEAD_DIM ** -0.5 * LOG2E
    p = proj(0)
    for hp in range(SB_WIDTH // LANES):
        sl = slice(hp * LANES, (hp + 1) * LANES)
        sbq_ref[:, sl] = (_half_head_rms(p[:, sl]) * (qg_ref[...] * q_scale)).astype(BF16)
    p = proj(1)
    blk = sbkt_ref.shape[-1]
    for hp in range(SB_WIDTH // LANES):
        sl = slice(hp * LANES, (hp + 1) * LANES)
        kt = (_half_head_rms(p[:, sl]) * kg_ref[...]).T
        for c in range(tm // blk):
            sbkt_ref[hp, c] = kt[:, c * blk:(c + 1) * blk].astype(BF16)
    sbv_ref[...] = proj(2).astype(BF16)

    cos_a, sin_a, cos_b, sin_b = ca_ref[...], sa_ref[...], cb_ref[...], sb_ref[...]
    sign = jnp.where(lax.broadcasted_iota(jnp.int32, (1, LANES), 1) < RET_HEAD_DIM // 2, -1.0, 1.0)
    cos2 = cos_a * cos_b - sin_a * sin_b
    sin2 = sign * (sin_a * cos_b + cos_a * sin_b)

    def rope(blk):
        return blk * cos2 + pltpu.roll(blk, RET_HEAD_DIM // 2, 1) * sin2

    p = proj(3)
    for h in range(RET_HEADS):
        sl = slice(h * LANES, (h + 1) * LANES)
        rq_ref[:, sl] = rope(p[:, sl]).astype(BF16)
    p = proj(4)
    for h in range(RET_HEADS):
        sl = slice(h * LANES, (h + 1) * LANES)
        rkt_ref[h] = (rope(p[:, sl]) * RET_HEAD_DIM ** -0.5).T.astype(BF16)
    rv_ref[...] = proj(5).astype(BF16)
    rg_ref[...] = proj(6).astype(BF16)


def _even_in(x2, g, w, qg, kg, rope_tables, batch, seq, sb_blk):
    t, d = x2.shape
    tm = min(ROW_TILE, seq)
    tps = seq // tm
    tile_start = pl.BlockSpec((None, 1, LANES), lambda i: (i % tps, 0, 0))
    row = lambda i: (i, 0)
    pairs = SB_WIDTH // LANES
    flat = jax.ShapeDtypeStruct((t, SB_WIDTH), BF16)
    flat_spec = pl.BlockSpec((tm, SB_WIDTH), row)
    sbkt = jax.ShapeDtypeStruct((batch, pairs, seq // sb_blk, LANES, sb_blk), BF16)
    sbkt_spec = pl.BlockSpec((None, pairs, tm // sb_blk, LANES, sb_blk),
                             lambda i: (i // tps, 0, i % tps, 0, 0))
    rkt = jax.ShapeDtypeStruct((batch, RET_HEADS, RET_HEAD_DIM, seq), BF16)
    rkt_spec = pl.BlockSpec((None, RET_HEADS, RET_HEAD_DIM, tm), lambda i: (i // tps, 0, 0, i % tps))
    return pl.pallas_call(
        _even_in_kernel,
        grid=(t // tm,),
        in_specs=[
            pl.BlockSpec((tm, d), row),
            _const_spec((1, d)),
            _const_spec(w.shape),
            _const_spec((1, LANES)),
            _const_spec((1, LANES)),
            tile_start,
            tile_start,
            _const_spec((tm, LANES)),
            _const_spec((tm, LANES)),
        ],
        out_specs=[flat_spec, sbkt_spec, flat_spec, flat_spec, rkt_spec, flat_spec, flat_spec],
        out_shape=[flat, sbkt, flat, flat, rkt, flat, flat],
        compiler_params=_params("parallel"),
        name="even_in",
    )(x2, g, w, qg, kg, *rope_tables)


def _softplus2(z):
    return jnp.maximum(z, 0.0) + jnp.log2(1.0 + jnp.exp2(-jnp.abs(z)))


def _sb_kernel(q_ref, kt_ref, v_ref, tri_ref, qg_ref, kg_ref, o_ref, acc_ref, carry_ref):
    i = pl.program_id(2)
    blk = q_ref.shape[0]
    z_bound = (SB_NORM_SLACK * SB_HEAD_DIM * SB_HEAD_DIM ** -0.5 * LOG2E
               * jnp.max(jnp.abs(qg_ref[...])) * jnp.max(jnp.abs(kg_ref[...])))
    dead_above = z_bound + EXP2_UNDERFLOW
    q = q_ref[...]
    lane = lax.broadcasted_iota(jnp.int32, q.shape, 1)
    lo = lane < SB_HEAD_DIM
    zero = jnp.zeros_like(q)
    q_heads = (jnp.where(lo, q, zero), jnp.where(lo, zero, q))
    tri = tri_ref[...]

    def pair(j_a, diagonal):
        has_b = j_a > 0
        j_b = jnp.maximum(j_a - 1, 0)
        kt_a = kt_ref[j_a]
        kt_b = kt_ref[j_b]
        v_a = v_ref[pl.ds(pl.multiple_of(j_a * blk, blk), blk), :]
        v_b = v_ref[pl.ds(pl.multiple_of(j_b * blk, blk), blk), :]
        if diagonal:
            r = lax.broadcasted_iota(jnp.int32, (blk, blk), 0)
            c = lax.broadcasted_iota(jnp.int32, (blk, blk), 1)
            mask = c < r
        heads = range(2)
        z_a = [jnp.dot(q_heads[h], kt_a, preferred_element_type=F32) for h in heads]
        z_b = [jnp.dot(q_heads[h], kt_b, preferred_element_type=F32) for h in heads]
        sp_a = [_softplus2(z) for z in z_a]
        sp_b = [_softplus2(z) for z in z_b]
        if diagonal:
            sp_a = [jnp.where(mask, sp, 0.0) for sp in sp_a]
        cs_a = [jnp.dot(sp.astype(BF16), tri, preferred_element_type=F32) for sp in sp_a]
        cs_b = [jnp.dot(sp.astype(BF16), tri, preferred_element_type=F32) for sp in sp_b]
        tot_a = [jnp.sum(sp, axis=-1, keepdims=True) for sp in sp_a]
        tot_b = [jnp.where(has_b, jnp.sum(sp, axis=-1, keepdims=True), 0.0) for sp in sp_b]
        if diagonal:
            w_a = [jnp.where(mask, jnp.exp2(z_a[h] - cs_a[h]), 0.0) for h in heads]
            before_b = tot_a
        else:
            carry = [carry_ref[h] for h in heads]
            w_a = [jnp.exp2(z_a[h] - (cs_a[h] + carry[h])) for h in heads]
            before_b = [carry[h] + tot_a[h] for h in heads]
        w_b = [jnp.where(has_b, jnp.exp2(z_b[h] - (cs_b[h] + before_b[h])), 0.0) for h in heads]
        for h in heads:
            pv = (jnp.dot(w_a[h].astype(BF16), v_a, preferred_element_type=F32)
                  + jnp.dot(w_b[h].astype(BF16), v_b, preferred_element_type=F32))
            if diagonal:
                acc_ref[h] = pv
            else:
                acc_ref[h] += pv
            carry_ref[h] = before_b[h] + tot_b[h]

    def live():
        return jnp.minimum(jnp.min(carry_ref[0]), jnp.min(carry_ref[1])) <= dead_above

    pair(i, True)

    def body(c):
        j_a, _ = c
        pair(j_a, False)
        return j_a - 2, live()

    lax.while_loop(lambda c: jnp.logical_and(c[0] >= 0, c[1]), body, (i - 2, live()))
    o_ref[...] = jnp.where(lo, acc_ref[0], acc_ref[1]).astype(o_ref.dtype)


def _sb_attention(q, kt, v, tri, qg, kg):
    b, s, width = q.shape
    blk = kt.shape[-1]
    pairs = width // LANES
    return pl.pallas_call(
        _sb_kernel,
        grid=(b, pairs, s // blk),
        in_specs=[
            pl.BlockSpec((None, blk, LANES), lambda bi, hp, i: (bi, i, hp)),
            pl.BlockSpec((None, None, s // blk, LANES, blk), lambda bi, hp, i: (bi, hp, 0, 0, 0)),
            pl.BlockSpec((None, s, LANES), lambda bi, hp, i: (bi, 0, hp)),
            _const_spec(tri.shape),
            _const_spec((1, LANES)),
            _const_spec((1, LANES)),
        ],
        out_specs=pl.BlockSpec((None, blk, LANES), lambda bi, hp, i: (bi, i, hp)),
        out_shape=jax.ShapeDtypeStruct((b, s, width), BF16),
        scratch_shapes=[pltpu.VMEM((2, blk, LANES), F32), pltpu.VMEM((2, blk, 1), F32)],
        compiler_params=_params("parallel", "parallel", "arbitrary"),
        name="sb_attention",
    )(q, kt, v, tri, qg, kg)


def _ret_kernel(q_ref, kt_ref, v_ref, g_ref, gain_ref, decay_ref, zeta_ref, xi_ref, gc_ref,
                o_ref, state_ref):
    @pl.when(pl.program_id(2) == 0)
    def _():
        state_ref[...] = jnp.zeros_like(state_ref)

    c = RET_CHUNK
    d = RET_HEAD_DIM
    span = decay_ref.shape[-1]
    rows = q_ref.shape[0]
    gain = gain_ref[...]
    for hh in range(HEADS_PER_STEP):
        lanes = slice(hh * d, (hh + 1) * d)
        decay = decay_ref[hh]
        zeta = zeta_ref[hh]
        xi = xi_ref[hh]
        g_chunk = gc_ref[hh]
        q = [q_ref[m * span:(m + 1) * span, lanes] for m in range(rows // span)]
        kt = [kt_ref[hh, :, m * span:(m + 1) * span] for m in range(rows // span)]
        v = [v_ref[m * span:(m + 1) * span, lanes] for m in range(rows // span)]
        scores = [jnp.dot(q[m], kt[m], preferred_element_type=F32) * decay
                  for m in range(rows // span)]
        inner = [jnp.dot(scores[m].astype(BF16), v[m], preferred_element_type=F32)
                 for m in range(rows // span)]
        contrib = []
        for n in range(rows // c):
            m, sub = divmod(n * c, span)
            kz = (kt[m][:, sub:sub + c].astype(F32) * zeta).astype(BF16)
            contrib.append(jnp.dot(kz, v[m][sub:sub + c], preferred_element_type=F32))
        state = state_ref[hh]
        for n in range(rows // c):
            m, sub = divmod(n * c, span)
            cross = jnp.dot(q[m][sub:sub + c], state.astype(BF16), preferred_element_type=F32) * xi
            state = g_chunk * state + contrib[n]
            ret = _rms_rows(inner[m][sub:sub + c] + cross, gain)
            gate = _silu(g_ref[n * c:(n + 1) * c, lanes].astype(F32))
            o_ref[n * c:(n + 1) * c, lanes] = (ret * gate).astype(o_ref.dtype)
        state_ref[hh] = state


def _retention(q, kt, v, g, gain, decay, zeta, xi, gc):
    b, s, width = q.shape
    rows = min(RET_GROUP, s)
    span = decay.shape[-1]
    hps = HEADS_PER_STEP
    blk = pl.BlockSpec((None, rows, hps * RET_HEAD_DIM), lambda bi, h, n: (bi, n, h))
    head = lambda shape: pl.BlockSpec((hps,) + shape, lambda bi, h, n: (h, 0, 0))
    return pl.pallas_call(
        _ret_kernel,
        grid=(b, RET_HEADS // hps, s // rows),
        in_specs=[blk,
                  pl.BlockSpec((None, hps, RET_HEAD_DIM, rows), lambda bi, h, n: (bi, h, 0, n)),
                  blk, blk, _const_spec((1, RET_HEAD_DIM)),
                  head((span, span)), head((1, RET_CHUNK)), head((RET_CHUNK, 1)), head((1, 1))],
        out_specs=blk,
        out_shape=jax.ShapeDtypeStruct((b, s, width), BF16),
        scratch_shapes=[pltpu.VMEM((hps, RET_HEAD_DIM, RET_HEAD_DIM), F32)],
        compiler_params=_params("parallel", "parallel", "arbitrary"),
        name="retention",
    )(q, kt, v, g, gain, decay, zeta, xi, gc)


def _ffn_kernel(*refs, n_mix, tiles_per_seq):
    h_ref = refs[0]
    mix_refs = refs[1:1 + n_mix]
    (wo_ref, g_ref, wup_ref, cw_ref, cb_ref, wd_ref, o_ref,
     ubuf_ref, prev_ref, gate_ref) = refs[1 + n_mix:]
    tm = h_ref.shape[0]
    d_ff = wd_ref.shape[0]
    halo = SUBLANES

    @pl.when(pl.program_id(0) % tiles_per_seq == 0)
    def _():
        prev_ref[...] = jnp.zeros_like(prev_ref)

    h = h_ref[...]
    row0 = 0
    for a_ref in mix_refs:
        width = a_ref.shape[1]
        h = h + jnp.dot(a_ref[...], wo_ref[row0:row0 + width, :], preferred_element_type=F32)
        row0 += width
    hn = _rms_rows(h, g_ref[...]).astype(BF16)
    for c0 in range(0, d_ff, FF_CHUNK):
        cols = slice(c0, c0 + FF_CHUNK)
        u = jnp.dot(hn, wup_ref[:, cols], preferred_element_type=F32)
        v = jnp.dot(hn, wup_ref[:, d_ff + c0:d_ff + c0 + FF_CHUNK], preferred_element_type=F32)
        ubuf_ref[0:halo, :] = prev_ref[:, cols]
        ubuf_ref[halo:halo + tm, :] = u
        prev_ref[:, cols] = u[tm - halo:, :]
        uc = (cb_ref[:, cols]
              + cw_ref[0:1, cols] * ubuf_ref[halo - 2:halo - 2 + tm, :]
              + cw_ref[1:2, cols] * ubuf_ref[halo - 1:halo - 1 + tm, :]
              + cw_ref[2:3, cols] * u)
        gate_ref[:, cols] = (_silu(uc) * v).astype(BF16)
    o_ref[...] = h + jnp.dot(gate_ref[...], wd_ref[...], preferred_element_type=F32)


def _ffn(h2, mix, wo, g, wup, cw, cb, wd, seq):
    t, d = h2.shape
    d_ff = wd.shape[0]
    tm = min(ROW_TILE, seq)
    row = lambda i: (i, 0)
    return pl.pallas_call(
        functools.partial(_ffn_kernel, n_mix=len(mix), tiles_per_seq=seq // tm),
        grid=(t // tm,),
        in_specs=([pl.BlockSpec((tm, d), row)]
                  + [pl.BlockSpec((tm, a.shape[1]), row) for a in mix]
                  + [_const_spec(wo.shape), _const_spec((1, d)), _const_spec(wup.shape),
                     _const_spec(cw.shape), _const_spec(cb.shape), _const_spec(wd.shape)]),
        out_specs=pl.BlockSpec((tm, d), row),
        out_shape=jax.ShapeDtypeStruct((t, d), F32),
        scratch_shapes=[pltpu.VMEM((tm + SUBLANES, FF_CHUNK), F32),
                        pltpu.VMEM((SUBLANES, d_ff), F32),
                        pltpu.VMEM((tm, d_ff), BF16)],
        compiler_params=_params("arbitrary"),
        name="conv_ffn",
    )(h2, *mix, wo, g, wup, cw, cb, wd)


def _log_sigmoid(x):
    return jnp.minimum(x, 0.0) - jnp.log(1.0 + jnp.exp(-jnp.abs(x)))


def _odd_in_kernel(x_ref, g_ref, w_ref, wa_ref, walpha_ref, balpha_ref, tri_ref,
                   qt_ref, ktt_ref, kdt_ref, v_ref, r_ref, dec_ref):
    tm = x_ref.shape[0]
    span = tri_ref.shape[0]
    hn = _rms_rows(x_ref[...], g_ref[...]).astype(BF16)

    def proj(c0, width):
        return jnp.dot(hn, w_ref[:, c0:c0 + width], preferred_element_type=F32)

    ga = jnp.dot(hn, wa_ref[...], preferred_element_type=F32)
    ga_hi = ga.astype(BF16)
    ga_lo = (ga - ga_hi.astype(F32)).astype(BF16)
    lane = lax.broadcasted_iota(jnp.int32, ga.shape, 1)
    middle = jnp.logical_and(lane >= GLA_GATE_RANK, lane < 2 * GLA_GATE_RANK)
    pre = jnp.dot(jnp.where(middle, ga_lo, ga_hi), walpha_ref[...],
                  preferred_element_type=F32) + balpha_ref[...]
    la = _log_sigmoid(pre) * (LOG2E / GLA_GATE_TAU)
    la_hi = la.astype(BF16)
    la_lo = (la - la_hi.astype(F32)).astype(BF16)
    tri = tri_ref[...]
    cum = jnp.concatenate(
        [jnp.dot(tri, la_hi[r0:r0 + span], preferred_element_type=F32)
         + jnp.dot(tri, la_lo[r0:r0 + span], preferred_element_type=F32)
         for r0 in range(0, tm, span)], axis=0)
    chunks = tm // GLA_CHUNK
    last = cum.reshape(chunks, GLA_CHUNK, GLA_K_WIDTH)[:, GLA_CHUNK - 1:GLA_CHUNK, :]
    dec_ref[...] = jnp.exp2(last).reshape(chunks, GLA_K_WIDTH)
    to_end = (jnp.broadcast_to(last, (chunks, GLA_CHUNK, GLA_K_WIDTH)).reshape(tm, GLA_K_WIDTH)
              - cum)

    qt_ref[...] = (proj(0, GLA_K_WIDTH) * GLA_DK ** -0.5 * jnp.exp2(cum)).astype(BF16)
    k = proj(GLA_K_WIDTH, GLA_K_WIDTH)
    kt = k * jnp.exp2(-cum)
    kd = k * jnp.exp2(to_end)
    for h in range(GLA_HEADS):
        sl = slice(h * GLA_DK, (h + 1) * GLA_DK)
        ktt_ref[h] = kt[:, sl].T.astype(BF16)
        kdt_ref[h] = kd[:, sl].T.astype(BF16)
    v_ref[...] = proj(2 * GLA_K_WIDTH, GLA_V_WIDTH).astype(BF16)
    r_ref[...] = proj(2 * GLA_K_WIDTH + GLA_V_WIDTH, GLA_V_WIDTH).astype(BF16)


def _odd_in(x2, g, w, wa, walpha, balpha, tri, batch, seq):
    t, d = x2.shape
    tm = min(ROW_TILE, seq)
    tps = seq // tm
    row = lambda i: (i, 0)
    kt_shape = jax.ShapeDtypeStruct((batch, GLA_HEADS, GLA_DK, seq), BF16)
    kt_spec = pl.BlockSpec((None, GLA_HEADS, GLA_DK, tm), lambda i: (i // tps, 0, 0, i % tps))
    vout = jax.ShapeDtypeStruct((t, GLA_V_WIDTH), BF16)
    return pl.pallas_call(
        _odd_in_kernel,
        grid=(t // tm,),
        in_specs=[pl.BlockSpec((tm, d), row), _const_spec((1, d)), _const_spec(w.shape),
                  _const_spec(wa.shape), _const_spec(walpha.shape), _const_spec(balpha.shape),
                  _const_spec(tri.shape)],
        out_specs=[pl.BlockSpec((tm, GLA_K_WIDTH), row), kt_spec, kt_spec,
                   pl.BlockSpec((tm, GLA_V_WIDTH), row), pl.BlockSpec((tm, GLA_V_WIDTH), row),
                   pl.BlockSpec((tm // GLA_CHUNK, GLA_K_WIDTH), row)],
        out_shape=[jax.ShapeDtypeStruct((t, GLA_K_WIDTH), BF16), kt_shape, kt_shape, vout, vout,
                   jax.ShapeDtypeStruct((t // GLA_CHUNK, GLA_K_WIDTH), F32)],
        compiler_params=_params("parallel"),
        name="odd_in",
    )(x2, g, w, wa, walpha, balpha, tri)


def _gla_kernel(qt_ref, ktt_ref, kdt_ref, v_ref, r_ref, dec_ref, gain_ref, o_ref, state_ref):
    @pl.when(pl.program_id(2) == 0)
    def _():
        state_ref[...] = jnp.zeros_like(state_ref)

    c = GLA_CHUNK
    rows = qt_ref.shape[0]
    span = min(GLA_SPAN, rows)
    ri = lax.broadcasted_iota(jnp.int32, (span, span), 0)
    ci = lax.broadcasted_iota(jnp.int32, (span, span), 1)
    causal = jnp.logical_and(ci <= ri, ci // c == ri // c)
    col_chunk = lax.broadcasted_iota(jnp.int32, (1, span), 1) // c
    gain = gain_ref[...]
    for hh in range(HEADS_PER_STEP):
        klanes = slice(hh * GLA_DK, (hh + 1) * GLA_DK)
        vlanes = slice(hh * GLA_DV, (hh + 1) * GLA_DV)
        dec_t = dec_ref[:, klanes].T
        spans = range(rows // span)
        qt = [qt_ref[m * span:(m + 1) * span, klanes] for m in spans]
        ktt = [ktt_ref[hh, :, m * span:(m + 1) * span] for m in spans]
        kdt = [kdt_ref[hh, :, m * span:(m + 1) * span] for m in spans]
        v = [v_ref[m * span:(m + 1) * span, vlanes] for m in spans]
        a = [jnp.where(causal, jnp.dot(qt[m], ktt[m], preferred_element_type=F32), 0.0)
             for m in spans]
        intra = [jnp.dot(a[m].astype(BF16), v[m], preferred_element_type=F32) for m in spans]
        contrib = []
        for n in range(rows // c):
            m, sub = divmod(n * c, span)
            kd_n = jnp.where(col_chunk == sub // c, kdt[m], jnp.zeros_like(kdt[m]))
            contrib.append(jnp.dot(kd_n, v[m], preferred_element_type=F32))
        state = state_ref[hh]
        for n in range(rows // c):
            m, sub = divmod(n * c, span)
            inter = jnp.dot(qt[m][sub:sub + c], state.astype(BF16), preferred_element_type=F32)
            state = dec_t[:, n:n + 1] * state + contrib[n]
            o = _rms_rows(intra[m][sub:sub + c] + inter, gain)
            gate = _silu(r_ref[n * c:(n + 1) * c, vlanes].astype(F32))
            o_ref[n * c:(n + 1) * c, vlanes] = (o * gate).astype(o_ref.dtype)
        state_ref[hh] = state


def _gla(qt, ktt, kdt, v, r, dec, gain):
    b, s, kw = qt.shape
    vw = v.shape[-1]
    rows = min(GLA_GROUP, s)
    hps = HEADS_PER_STEP
    qblk = pl.BlockSpec((None, rows, hps * GLA_DK), lambda bi, h, n: (bi, n, h))
    tblk = pl.BlockSpec((None, hps, GLA_DK, rows), lambda bi, h, n: (bi, h, 0, n))
    vblk = pl.BlockSpec((None, rows, hps * GLA_DV), lambda bi, h, n: (bi, n, h))
    dblk = pl.BlockSpec((None, rows // GLA_CHUNK, hps * GLA_DK), lambda bi, h, n: (bi, n, h))
    return pl.pallas_call(
        _gla_kernel,
        grid=(b, GLA_HEADS // hps, s // rows),
        in_specs=[qblk, tblk, tblk, vblk, vblk, dblk, _const_spec((1, GLA_DV))],
        out_specs=vblk,
        out_shape=jax.ShapeDtypeStruct((b, s, vw), BF16),
        scratch_shapes=[pltpu.VMEM((hps, GLA_DK, GLA_DV), F32)],
        compiler_params=_params("parallel", "parallel", "arbitrary"),
        name="gla",
    )(qt, ktt, kdt, v, r, dec, gain)


def _rope_tables(seq, tile):
    half = RET_HEAD_DIM // 2
    inv = ROPE_BASE ** (-np.arange(half, dtype=np.float64) / half)

    def both(pos):
        ang = pos[:, None] * inv[None, :]
        dup = lambda t: np.concatenate([t, t], axis=-1).astype(np.float32)
        return dup(np.cos(ang)), dup(np.sin(ang))

    cos_a, sin_a = both(np.arange(0, seq, tile, dtype=np.float64))
    cos_b, sin_b = both(np.arange(tile, dtype=np.float64))
    return (jnp.asarray(cos_a[:, None, :]), jnp.asarray(sin_a[:, None, :]),
            jnp.asarray(cos_b), jnp.asarray(sin_b))


def _retention_tables(span):
    c = RET_CHUNK
    lg = np.log(1.0 - np.exp2(-5.0 - np.arange(RET_HEADS, dtype=np.float64)))[:, None]
    pos = np.arange(span, dtype=np.float64)
    diff = pos[:, None] - pos[None, :]
    same = (pos[:, None] // c) == (pos[None, :] // c)
    decay = np.where((diff >= 0) & same, np.exp(lg[:, :, None] * np.maximum(diff, 0.0)), 0.0)
    pos = np.arange(c, dtype=np.float64)
    zeta = np.exp(lg * (c - 1 - pos))[:, None, :]
    xi = np.exp(lg * (pos + 1.0))[:, :, None]
    g_chunk = np.exp(lg * c)[:, :, None]
    return tuple(jnp.asarray(t.astype(np.float32)) for t in (decay, zeta, xi, g_chunk))


def _suffix_tri(n):
    idx = np.arange(n)
    return jnp.asarray(idx[:, None] >= idx[None, :], dtype=BF16)


def _chunk_prefix_tri(rows, chunk):
    idx = np.arange(rows)
    same = (idx[:, None] // chunk) == (idx[None, :] // chunk)
    return jnp.asarray(same & (idx[None, :] <= idx[:, None]), dtype=BF16)


def _gate_operands(w_gate_in, w_alpha):
    rank = GLA_GATE_RANK
    wa = jnp.pad(jnp.tile(w_gate_in, (1, 3)), ((0, 0), (0, LANES - 3 * rank))).astype(BF16)
    w_hi = w_alpha.astype(BF16)
    w_lo = (w_alpha - w_hi.astype(F32)).astype(BF16)
    walpha = jnp.pad(jnp.concatenate([w_hi, w_hi, w_lo], axis=0), ((0, LANES - 3 * rank), (0, 0)))
    return wa, walpha


def kernel(x, mix_norm_g, even_w_in, sb_q_gain, sb_k_gain, ret_out_gain, even_w_out,
           odd_w_in, gla_w_alpha, gla_b_alpha, gla_out_gain, odd_w_out,
           ffn_norm_g, ffn_w_up, ffn_conv_w, ffn_conv_b, ffn_w_down):
    b, s, d = x.shape
    t = b * s
    depth = mix_norm_g.shape[0]
    h = x.reshape(t, d)

    rope_tables = _rope_tables(s, min(ROW_TILE, s))
    decay, zeta, xi, g_chunk = _retention_tables(min(RET_SPAN, s))
    sb_blk = min(SB_BLOCK, s)
    sb_tri = _suffix_tri(sb_blk)
    gla_tri = _chunk_prefix_tri(min(GLA_SPAN, s), GLA_CHUNK)

    for layer in range(depth):
        g_mix = mix_norm_g[layer][None, :]
        if layer % 2 == 0:
            e = layer // 2
            qg = jnp.tile(sb_q_gain[e], 2)[None, :]
            kg = jnp.tile(sb_k_gain[e], 2)[None, :]
            sbq, sbkt, sbv, rq, rkt, rv, rg = _even_in(
                h, g_mix, even_w_in[e].astype(BF16), qg, kg, rope_tables, b, s, sb_blk)
            out_a = _sb_attention(sbq.reshape(b, s, SB_WIDTH), sbkt, sbv.reshape(b, s, SB_WIDTH),
                                  sb_tri, qg, kg)
            shp = (b, s, RET_WIDTH)
            out_b = _retention(rq.reshape(shp), rkt, rv.reshape(shp), rg.reshape(shp),
                               ret_out_gain[e][None, :], decay, zeta, xi, g_chunk)
            mix = (out_a.reshape(t, SB_WIDTH), out_b.reshape(t, RET_WIDTH))
            w_out = even_w_out[e]
        else:
            o = layer // 2
            w_in = odd_w_in[o]
            n_main = 2 * GLA_K_WIDTH + 2 * GLA_V_WIDTH
            wa, walpha = _gate_operands(w_in[:, n_main:], gla_w_alpha[o])
            qt, ktt, kdt, gv, gr, dec = _odd_in(h, g_mix, w_in[:, :n_main].astype(BF16), wa, walpha,
                                                gla_b_alpha[o][None, :], gla_tri, b, s)
            og = _gla(qt.reshape(b, s, GLA_K_WIDTH), ktt, kdt,
                      gv.reshape(b, s, GLA_V_WIDTH), gr.reshape(b, s, GLA_V_WIDTH),
                      dec.reshape(b, s // GLA_CHUNK, GLA_K_WIDTH), gla_out_gain[o][None, :])
            mix = (og.reshape(t, GLA_V_WIDTH),)
            w_out = odd_w_out[o]
        h = _ffn(h, mix, w_out.astype(BF16), ffn_norm_g[layer][None, :], ffn_w_up[layer].astype(BF16),
                 ffn_conv_w[layer], ffn_conv_b[layer][None, :], ffn_w_down[layer].astype(BF16), s)
    return h.reshape(b, s, d)
```

```python
import functools

import numpy as np

import jax
import jax.numpy as jnp
from jax import lax
from jax.experimental import pallas as pl
from jax.experimental.pallas import tpu as pltpu

F32 = jnp.float32
BF16 = jnp.bfloat16

EPS = 1e-6
LOG2E = 1.4426950408889634
ROPE_BASE = 10000.0
EXP2_UNDERFLOW = 150.0
SB_NORM_SLACK = 1.05
SOFTPLUS2_LINEAR_ABOVE = 32.0
MASKED_SCORE = -1e30

SB_HEADS = 8
SB_HEAD_DIM = 64
SB_WIDTH = SB_HEADS * SB_HEAD_DIM
RET_HEADS = 4
RET_HEAD_DIM = 128
RET_WIDTH = RET_HEADS * RET_HEAD_DIM
RET_CHUNK = 128
GLA_HEADS = 4
GLA_DK = 128
GLA_DV = 256
GLA_K_WIDTH = GLA_HEADS * GLA_DK
GLA_V_WIDTH = GLA_HEADS * GLA_DV
GLA_GATE_RANK = 16
GLA_GATE_TAU = 16.0
GLA_CHUNK = 64
CONV_WIDTH = 3

LANES = 128
SUBLANES = 8
MXU_DIM = 256
VMEM_LIMIT = 56 * 1024 * 1024

ROW_TILE = 512
SB_BLOCK = MXU_DIM
HEADS_PER_STEP = 2
RET_GROUP = 1024
RET_SPAN = 2 * RET_CHUNK
GLA_GROUP = ROW_TILE
GLA_SPAN = 4 * GLA_CHUNK
FF_CHUNK = 256


def _params(*sem):
    return pltpu.CompilerParams(dimension_semantics=sem, vmem_limit_bytes=VMEM_LIMIT)


def _const_spec(shape):
    n = len(shape)
    return pl.BlockSpec(shape, lambda *_: (0,) * n)


def _rms_rows(x, g):
    ms = jnp.mean(x * x, axis=-1, keepdims=True)
    return x * lax.rsqrt(ms + EPS) * g


def _silu(x):
    return x / (1.0 + jnp.exp(-x))


def _half_head_rms(blk):
    sq = blk * blk
    lane = lax.broadcasted_iota(jnp.int32, blk.shape, 1)
    lo = lane < SB_HEAD_DIM
    s_lo = jnp.sum(jnp.where(lo, sq, 0.0), axis=-1, keepdims=True)
    s_hi = jnp.sum(jnp.where(lo, 0.0, sq), axis=-1, keepdims=True)
    ms = jnp.where(lo, s_lo, s_hi) * (1.0 / SB_HEAD_DIM)
    return blk * lax.rsqrt(ms + EPS)


def _even_in_kernel(x_ref, g_ref, w_ref, qg_ref, kg_ref, ca_ref, sa_ref, cb_ref, sb_ref,
                    sbq_ref, sbkt_ref, sbv_ref, rq_ref, rkt_ref, rv_ref, rg_ref):
    tm = x_ref.shape[0]
    hn = _rms_rows(x_ref[...], g_ref[...]).astype(BF16)

    def proj(seg):
        return jnp.dot(hn, w_ref[:, seg * SB_WIDTH:(seg + 1) * SB_WIDTH],
                       preferred_element_type=F32)

    q_scale = SB_HEAD_DIM ** -0.5 * LOG2E
    p = proj(0)
    for hp in range(SB_WIDTH // LANES):
        sl = slice(hp * LANES, (hp + 1) * LANES)
        sbq_ref[:, sl] = (_half_head_rms(p[:, sl]) * (qg_ref[...] * q_scale)).astype(BF16)
    p = proj(1)
    blk = sbkt_ref.shape[-1]
    for hp in range(SB_WIDTH // LANES):
        sl = slice(hp * LANES, (hp + 1) * LANES)
        kt = (_half_head_rms(p[:, sl]) * kg_ref[...]).T
        for c in range(tm // blk):
            sbkt_ref[hp, c] = kt[:, c * blk:(c + 1) * blk].astype(BF16)
    sbv_ref[...] = proj(2).astype(BF16)

    cos_a, sin_a, cos_b, sin_b = ca_ref[...], sa_ref[...], cb_ref[...], sb_ref[...]
    sign = jnp.where(lax.broadcasted_iota(jnp.int32, (1, LANES), 1) < RET_HEAD_DIM // 2, -1.0, 1.0)
    cos2 = cos_a * cos_b - sin_a * sin_b
    sin2 = sign * (sin_a * cos_b + cos_a * sin_b)

    def rope(blk):
        return blk * cos2 + pltpu.roll(blk, RET_HEAD_DIM // 2, 1) * sin2

    p = proj(3)
    for h in range(RET_HEADS):
        sl = slice(h * LANES, (h + 1) * LANES)
        rq_ref[:, sl] = rope(p[:, sl]).astype(BF16)
    p = proj(4)
    for h in range(RET_HEADS):
        sl = slice(h * LANES, (h + 1) * LANES)
        rkt_ref[h] = (rope(p[:, sl]) * RET_HEAD_DIM ** -0.5).T.astype(BF16)
    rv_ref[...] = proj(5).astype(BF16)
    rg_ref[...] = proj(6).astype(BF16)


def _even_in(x2, g, w, qg, kg, rope_tables, batch, seq, sb_blk):
    t, d = x2.shape
    tm = min(ROW_TILE, seq)
    tps = seq // tm
    tile_start = pl.BlockSpec((None, 1, LANES), lambda i: (i % tps, 0, 0))
    row = lambda i: (i, 0)
    pairs = SB_WIDTH // LANES
    flat = jax.ShapeDtypeStruct((t, SB_WIDTH), BF16)
    flat_spec = pl.BlockSpec((tm, SB_WIDTH), row)
    sbkt = jax.ShapeDtypeStruct((batch, pairs, seq // sb_blk, LANES, sb_blk), BF16)
    sbkt_spec = pl.BlockSpec((None, pairs, tm // sb_blk, LANES, sb_blk),
                             lambda i: (i // tps, 0, i % tps, 0, 0))
    rkt = jax.ShapeDtypeStruct((batch, RET_HEADS, RET_HEAD_DIM, seq), BF16)
    rkt_spec = pl.BlockSpec((None, RET_HEADS, RET_HEAD_DIM, tm), lambda i: (i // tps, 0, 0, i % tps))
    return pl.pallas_call(
        _even_in_kernel,
        grid=(t // tm,),
        in_specs=[
            pl.BlockSpec((tm, d), row),
            _const_spec((1, d)),
            _const_spec(w.shape),
            _const_spec((1, LANES)),
            _const_spec((1, LANES)),
            tile_start,
            tile_start,
            _const_spec((tm, LANES)),
            _const_spec((tm, LANES)),
        ],
        out_specs=[flat_spec, sbkt_spec, flat_spec, flat_spec, rkt_spec, flat_spec, flat_spec],
        out_shape=[flat, sbkt, flat, flat, rkt, flat, flat],
        compiler_params=_params("parallel"),
        name="even_in",
    )(x2, g, w, qg, kg, *rope_tables)


def _softplus2(z):
    return jnp.where(z > SOFTPLUS2_LINEAR_ABOVE, z, jnp.log2(1.0 + jnp.exp2(z)))


def _sb_kernel(q_ref, qold_ref, kt_ref, v_ref, tri_ref, qg_ref, kg_ref, o_ref,
               acc_ref, carry_ref, z_ref, sp_ref, tot_ref, w_ref, carry0_ref):
    i = pl.program_id(2)
    blk = q_ref.shape[0]
    n_blocks = kt_ref.shape[0]
    last = n_blocks - 1
    blk_scores = jnp.minimum(i, last)
    blk_weights = jnp.clip(i - 1, 0, last)
    blk_values = jnp.clip(i - 2, 0, last)
    cur = i % 2
    prev = 1 - cur
    z_bound = (SB_NORM_SLACK * SB_HEAD_DIM * SB_HEAD_DIM ** -0.5 * LOG2E
               * jnp.max(jnp.abs(qg_ref[...])) * jnp.max(jnp.abs(kg_ref[...])))
    dead_above = z_bound + EXP2_UNDERFLOW
    lane = lax.broadcasted_iota(jnp.int32, (blk, LANES), 1)
    lo = lane < SB_HEAD_DIM
    row_id = lax.broadcasted_iota(jnp.int32, (blk, blk), 0)
    col_id = lax.broadcasted_iota(jnp.int32, (blk, blk), 1)
    below_diag = col_id < row_id
    tri = tri_ref[...]
    heads = range(2)

    def split_heads(q):
        zero = jnp.zeros_like(q)
        return (jnp.where(lo, q, zero), jnp.where(lo, zero, q))

    def key_blocks(j_a):
        j_b = jnp.maximum(j_a - 1, 0)
        return j_a > 0, j_b

    def values(j):
        return v_ref[pl.ds(pl.multiple_of(j * blk, blk), blk), :]

    def scores_stage(q_heads, i, slot):
        has_b, j_b = key_blocks(i)
        kt_a, kt_b = kt_ref[i], kt_ref[j_b]
        for h in heads:
            z_a = jnp.where(below_diag, jnp.dot(q_heads[h], kt_a, preferred_element_type=F32),
                            MASKED_SCORE)
            z_b = jnp.dot(q_heads[h], kt_b, preferred_element_type=F32)
            sp_a = _softplus2(z_a)
            sp_b = _softplus2(z_b)
            z_ref[slot, h, 0] = z_a
            z_ref[slot, h, 1] = z_b
            sp_ref[slot, h, 0] = sp_a.astype(BF16)
            sp_ref[slot, h, 1] = sp_b.astype(BF16)
            tot_ref[slot, h, 0] = jnp.sum(sp_a, axis=-1, keepdims=True)
            tot_ref[slot, h, 1] = jnp.where(has_b, jnp.sum(sp_b, axis=-1, keepdims=True), 0.0)

    def weights_stage(i, slot):
        has_b, _ = key_blocks(i)
        for h in heads:
            cs_a = jnp.dot(sp_ref[slot, h, 0], tri, preferred_element_type=F32)
            cs_b = jnp.dot(sp_ref[slot, h, 1], tri, preferred_element_type=F32)
            tot_a = tot_ref[slot, h, 0]
            before_b = jnp.where(has_b, tot_a, -MASKED_SCORE)
            w_ref[slot, h, 0] = jnp.exp2(z_ref[slot, h, 0] - cs_a).astype(BF16)
            w_ref[slot, h, 1] = jnp.exp2(z_ref[slot, h, 1] - (cs_b + before_b)).astype(BF16)
            carry0_ref[slot, h] = tot_a + tot_ref[slot, h, 1]

    def values_stage(i, slot):
        _, j_b = key_blocks(i)
        v_a, v_b = values(i), values(j_b)
        for h in heads:
            acc_ref[h] = (jnp.dot(w_ref[slot, h, 0], v_a, preferred_element_type=F32)
                          + jnp.dot(w_ref[slot, h, 1], v_b, preferred_element_type=F32))
            carry_ref[h] = carry0_ref[slot, h]

    def older_pair(q_heads, j_a):
        has_b, j_b = key_blocks(j_a)
        kt_a, kt_b = kt_ref[j_a], kt_ref[j_b]
        v_a, v_b = values(j_a), values(j_b)
        z_a = [jnp.dot(q_heads[h], kt_a, preferred_element_type=F32) for h in heads]
        z_b = [jnp.dot(q_heads[h], kt_b, preferred_element_type=F32) for h in heads]
        sp_a = [_softplus2(z) for z in z_a]
        sp_b = [_softplus2(z) for z in z_b]
        cs_a = [jnp.dot(sp.astype(BF16), tri, preferred_element_type=F32) for sp in sp_a]
        cs_b = [jnp.dot(sp.astype(BF16), tri, preferred_element_type=F32) for sp in sp_b]
        for h in heads:
            carry = carry_ref[h]
            after_a = carry + jnp.sum(sp_a[h], axis=-1, keepdims=True)
            tot_b = jnp.where(has_b, jnp.sum(sp_b[h], axis=-1, keepdims=True), 0.0)
            before_b = jnp.where(has_b, after_a, -MASKED_SCORE)
            w_a = jnp.exp2(z_a[h] - (cs_a[h] + carry))
            w_b = jnp.exp2(z_b[h] - (cs_b[h] + before_b))
            acc_ref[h] += (jnp.dot(w_a.astype(BF16), v_a, preferred_element_type=F32)
                           + jnp.dot(w_b.astype(BF16), v_b, preferred_element_type=F32))
            carry_ref[h] = after_a + tot_b

    def live():
        return jnp.minimum(jnp.min(carry_ref[0]), jnp.min(carry_ref[1])) <= dead_above

    @pl.when(i == 0)
    def _():
        z_ref[1] = jnp.zeros(z_ref.shape[1:], z_ref.dtype)
        sp_ref[1] = jnp.zeros(sp_ref.shape[1:], sp_ref.dtype)
        tot_ref[1] = jnp.zeros(tot_ref.shape[1:], tot_ref.dtype)
        w_ref[0] = jnp.zeros(w_ref.shape[1:], w_ref.dtype)
        carry0_ref[0] = jnp.zeros(carry0_ref.shape[1:], carry0_ref.dtype)

    weights_stage(blk_weights, prev)
    scores_stage(split_heads(q_ref[...]), blk_scores, cur)
    values_stage(blk_values, cur)

    q_old = split_heads(qold_ref[...])

    def body(c):
        j_a, _ = c
        older_pair(q_old, j_a)
        return j_a - 2, live()

    first_older = jnp.where(i >= 2, blk_values - 2, -1)
    lax.while_loop(lambda c: jnp.logical_and(c[0] >= 0, c[1]), body, (first_older, live()))
    o_ref[...] = jnp.where(lo, acc_ref[0], acc_ref[1]).astype(o_ref.dtype)


def _sb_attention(q, kt, v, tri, qg, kg):
    b, s, width = q.shape
    blk = kt.shape[-1]
    pairs = width // LANES
    n_blocks = s // blk
    stage = (2, 2, 2, blk)
    scores_blk = lambda bi, hp, i: (bi, jnp.minimum(i, n_blocks - 1), hp)
    finish_blk = lambda bi, hp, i: (bi, jnp.clip(i - 2, 0, n_blocks - 1), hp)
    return pl.pallas_call(
        _sb_kernel,
        grid=(b, pairs, n_blocks + 2),
        in_specs=[
            pl.BlockSpec((None, blk, LANES), scores_blk),
            pl.BlockSpec((None, blk, LANES), finish_blk),
            pl.BlockSpec((None, None, n_blocks, LANES, blk), lambda bi, hp, i: (bi, hp, 0, 0, 0)),
            pl.BlockSpec((None, s, LANES), lambda bi, hp, i: (bi, 0, hp)),
            _const_spec(tri.shape),
            _const_spec((1, LANES)),
            _const_spec((1, LANES)),
        ],
        out_specs=pl.BlockSpec((None, blk, LANES), finish_blk),
        out_shape=jax.ShapeDtypeStruct((b, s, width), BF16),
        scratch_shapes=[pltpu.VMEM((2, blk, LANES), F32), pltpu.VMEM((2, blk, 1), F32),
                        pltpu.VMEM(stage + (blk,), F32), pltpu.VMEM(stage + (blk,), BF16),
                        pltpu.VMEM(stage + (1,), F32), pltpu.VMEM(stage + (blk,), BF16),
                        pltpu.VMEM((2, 2, blk, 1), F32)],
        compiler_params=_params("parallel", "parallel", "arbitrary"),
        name="sb_attention",
    )(q, q, kt, v, tri, qg, kg)


def _ret_kernel(q_ref, kt_ref, v_ref, g_ref, gain_ref, decay_ref, zeta_ref, xi_ref, gc_ref,
                o_ref, state_ref):
    @pl.when(pl.program_id(2) == 0)
    def _():
        state_ref[...] = jnp.zeros_like(state_ref)

    c = RET_CHUNK
    d = RET_HEAD_DIM
    span = decay_ref.shape[-1]
    rows = q_ref.shape[0]
    gain = gain_ref[...]
    for hh in range(HEADS_PER_STEP):
        lanes = slice(hh * d, (hh + 1) * d)
        decay = decay_ref[hh]
        zeta = zeta_ref[hh]
        xi = xi_ref[hh]
        g_chunk = gc_ref[hh]
        q = [q_ref[m * span:(m + 1) * span, lanes] for m in range(rows // span)]
        kt = [kt_ref[hh, :, m * span:(m + 1) * span] for m in range(rows // span)]
        v = [v_ref[m * span:(m + 1) * span, lanes] for m in range(rows // span)]
        scores = [jnp.dot(q[m], kt[m], preferred_element_type=F32) * decay
                  for m in range(rows // span)]
        inner = [jnp.dot(scores[m].astype(BF16), v[m], preferred_element_type=F32)
                 for m in range(rows // span)]
        contrib = []
        for n in range(rows // c):
            m, sub = divmod(n * c, span)
            kz = (kt[m][:, sub:sub + c].astype(F32) * zeta).astype(BF16)
            contrib.append(jnp.dot(kz, v[m][sub:sub + c], preferred_element_type=F32))
        state = state_ref[hh]
        for n in range(rows // c):
            m, sub = divmod(n * c, span)
            cross = jnp.dot(q[m][sub:sub + c], state.astype(BF16), preferred_element_type=F32) * xi
            state = g_chunk * state + contrib[n]
            ret = _rms_rows(inner[m][sub:sub + c] + cross, gain)
            gate = _silu(g_ref[n * c:(n + 1) * c, lanes].astype(F32))
            o_ref[n * c:(n + 1) * c, lanes] = (ret * gate).astype(o_ref.dtype)
        state_ref[hh] = state


def _retention(q, kt, v, g, gain, decay, zeta, xi, gc):
    b, s, width = q.shape
    rows = min(RET_GROUP, s)
    span = decay.shape[-1]
    hps = HEADS_PER_STEP
    blk = pl.BlockSpec((None, rows, hps * RET_HEAD_DIM), lambda bi, h, n: (bi, n, h))
    head = lambda shape: pl.BlockSpec((hps,) + shape, lambda bi, h, n: (h, 0, 0))
    return pl.pallas_call(
        _ret_kernel,
        grid=(b, RET_HEADS // hps, s // rows),
        in_specs=[blk,
                  pl.BlockSpec((None, hps, RET_HEAD_DIM, rows), lambda bi, h, n: (bi, h, 0, n)),
                  blk, blk, _const_spec((1, RET_HEAD_DIM)),
                  head((span, span)), head((1, RET_CHUNK)), head((RET_CHUNK, 1)), head((1, 1))],
        out_specs=blk,
        out_shape=jax.ShapeDtypeStruct((b, s, width), BF16),
        scratch_shapes=[pltpu.VMEM((hps, RET_HEAD_DIM, RET_HEAD_DIM), F32)],
        compiler_params=_params("parallel", "parallel", "arbitrary"),
        name="retention",
    )(q, kt, v, g, gain, decay, zeta, xi, gc)


def _ffn_kernel(*refs, n_mix, tiles_per_seq):
    h_ref = refs[0]
    mix_refs = refs[1:1 + n_mix]
    (wo_ref, g_ref, wup_ref, cw_ref, cb_ref, wd_ref, o_ref,
     ubuf_ref, prev_ref, gate_ref) = refs[1 + n_mix:]
    tm = h_ref.shape[0]
    d_ff = wd_ref.shape[0]
    halo = SUBLANES

    @pl.when(pl.program_id(0) % tiles_per_seq == 0)
    def _():
        prev_ref[...] = jnp.zeros_like(prev_ref)

    h = h_ref[...]
    row0 = 0
    for a_ref in mix_refs:
        width = a_ref.shape[1]
        h = h + jnp.dot(a_ref[...], wo_ref[row0:row0 + width, :], preferred_element_type=F32)
        row0 += width
    hn = _rms_rows(h, g_ref[...]).astype(BF16)
    for c0 in range(0, d_ff, FF_CHUNK):
        cols = slice(c0, c0 + FF_CHUNK)
        u = jnp.dot(hn, wup_ref[:, cols], preferred_element_type=F32)
        v = jnp.dot(hn, wup_ref[:, d_ff + c0:d_ff + c0 + FF_CHUNK], preferred_element_type=F32)
        ubuf_ref[0:halo, :] = prev_ref[:, cols]
        ubuf_ref[halo:halo + tm, :] = u
        prev_ref[:, cols] = u[tm - halo:, :]
        uc = (cb_ref[:, cols]
              + cw_ref[0:1, cols] * ubuf_ref[halo - 2:halo - 2 + tm, :]
              + cw_ref[1:2, cols] * ubuf_ref[halo - 1:halo - 1 + tm, :]
              + cw_ref[2:3, cols] * u)
        gate_ref[:, cols] = (_silu(uc) * v).astype(BF16)
    o_ref[...] = h + jnp.dot(gate_ref[...], wd_ref[...], preferred_element_type=F32)


def _ffn(h2, mix, wo, g, wup, cw, cb, wd, seq):
    t, d = h2.shape
    d_ff = wd.shape[0]
    tm = min(ROW_TILE, seq)
    row = lambda i: (i, 0)
    return pl.pallas_call(
        functools.partial(_ffn_kernel, n_mix=len(mix), tiles_per_seq=seq // tm),
        grid=(t // tm,),
        in_specs=([pl.BlockSpec((tm, d), row)]
                  + [pl.BlockSpec((tm, a.shape[1]), row) for a in mix]
                  + [_const_spec(wo.shape), _const_spec((1, d)), _const_spec(wup.shape),
                     _const_spec(cw.shape), _const_spec(cb.shape), _const_spec(wd.shape)]),
        out_specs=pl.BlockSpec((tm, d), row),
        out_shape=jax.ShapeDtypeStruct((t, d), F32),
        scratch_shapes=[pltpu.VMEM((tm + SUBLANES, FF_CHUNK), F32),
                        pltpu.VMEM((SUBLANES, d_ff), F32),
                        pltpu.VMEM((tm, d_ff), BF16)],
        compiler_params=_params("arbitrary"),
        name="conv_ffn",
    )(h2, *mix, wo, g, wup, cw, cb, wd)


def _log_sigmoid(x):
    return jnp.minimum(x, 0.0) - jnp.log(1.0 + jnp.exp(-jnp.abs(x)))


def _odd_in_kernel(x_ref, g_ref, w_ref, wa_ref, walpha_ref, balpha_ref, tri_ref,
                   qt_ref, ktt_ref, kdt_ref, v_ref, r_ref, dec_ref):
    tm = x_ref.shape[0]
    span = tri_ref.shape[0]
    hn = _rms_rows(x_ref[...], g_ref[...]).astype(BF16)

    def proj(c0, width):
        return jnp.dot(hn, w_ref[:, c0:c0 + width], preferred_element_type=F32)

    ga = jnp.dot(hn, wa_ref[...], preferred_element_type=F32)
    ga_hi = ga.astype(BF16)
    ga_lo = (ga - ga_hi.astype(F32)).astype(BF16)
    lane = lax.broadcasted_iota(jnp.int32, ga.shape, 1)
    middle = jnp.logical_and(lane >= GLA_GATE_RANK, lane < 2 * GLA_GATE_RANK)
    pre = jnp.dot(jnp.where(middle, ga_lo, ga_hi), walpha_ref[...],
                  preferred_element_type=F32) + balpha_ref[...]
    la = _log_sigmoid(pre) * (LOG2E / GLA_GATE_TAU)
    la_hi = la.astype(BF16)
    la_lo = (la - la_hi.astype(F32)).astype(BF16)
    tri = tri_ref[...]
    cum = jnp.concatenate(
        [jnp.dot(tri, la_hi[r0:r0 + span], preferred_element_type=F32)
         + jnp.dot(tri, la_lo[r0:r0 + span], preferred_element_type=F32)
         for r0 in range(0, tm, span)], axis=0)
    chunks = tm // GLA_CHUNK
    last = cum.reshape(chunks, GLA_CHUNK, GLA_K_WIDTH)[:, GLA_CHUNK - 1:GLA_CHUNK, :]
    dec_ref[...] = jnp.exp2(last).reshape(chunks, GLA_K_WIDTH)
    to_end = (jnp.broadcast_to(last, (chunks, GLA_CHUNK, GLA_K_WIDTH)).reshape(tm, GLA_K_WIDTH)
              - cum)

    qt_ref[...] = (proj(0, GLA_K_WIDTH) * GLA_DK ** -0.5 * jnp.exp2(cum)).astype(BF16)
    k = proj(GLA_K_WIDTH, GLA_K_WIDTH)
    kt = k * jnp.exp2(-cum)
    kd = k * jnp.exp2(to_end)
    for h in range(GLA_HEADS):
        sl = slice(h * GLA_DK, (h + 1) * GLA_DK)
        ktt_ref[h] = kt[:, sl].T.astype(BF16)
        kdt_ref[h] = kd[:, sl].T.astype(BF16)
    v_ref[...] = proj(2 * GLA_K_WIDTH, GLA_V_WIDTH).astype(BF16)
    r_ref[...] = proj(2 * GLA_K_WIDTH + GLA_V_WIDTH, GLA_V_WIDTH).astype(BF16)


def _odd_in(x2, g, w, wa, walpha, balpha, tri, batch, seq):
    t, d = x2.shape
    tm = min(ROW_TILE, seq)
    tps = seq // tm
    row = lambda i: (i, 0)
    kt_shape = jax.ShapeDtypeStruct((batch, GLA_HEADS, GLA_DK, seq), BF16)
    kt_spec = pl.BlockSpec((None, GLA_HEADS, GLA_DK, tm), lambda i: (i // tps, 0, 0, i % tps))
    vout = jax.ShapeDtypeStruct((t, GLA_V_WIDTH), BF16)
    return pl.pallas_call(
        _odd_in_kernel,
        grid=(t // tm,),
        in_specs=[pl.BlockSpec((tm, d), row), _const_spec((1, d)), _const_spec(w.shape),
                  _const_spec(wa.shape), _const_spec(walpha.shape), _const_spec(balpha.shape),
                  _const_spec(tri.shape)],
        out_specs=[pl.BlockSpec((tm, GLA_K_WIDTH), row), kt_spec, kt_spec,
                   pl.BlockSpec((tm, GLA_V_WIDTH), row), pl.BlockSpec((tm, GLA_V_WIDTH), row),
                   pl.BlockSpec((tm // GLA_CHUNK, GLA_K_WIDTH), row)],
        out_shape=[jax.ShapeDtypeStruct((t, GLA_K_WIDTH), BF16), kt_shape, kt_shape, vout, vout,
                   jax.ShapeDtypeStruct((t // GLA_CHUNK, GLA_K_WIDTH), F32)],
        compiler_params=_params("parallel"),
        name="odd_in",
    )(x2, g, w, wa, walpha, balpha, tri)


def _gla_kernel(qt_ref, ktt_ref, kdt_ref, v_ref, r_ref, dec_ref, gain_ref, o_ref, state_ref):
    @pl.when(pl.program_id(2) == 0)
    def _():
        state_ref[...] = jnp.zeros_like(state_ref)

    c = GLA_CHUNK
    rows = qt_ref.shape[0]
    span = min(GLA_SPAN, rows)
    ri = lax.broadcasted_iota(jnp.int32, (span, span), 0)
    ci = lax.broadcasted_iota(jnp.int32, (span, span), 1)
    causal = jnp.logical_and(ci <= ri, ci // c == ri // c)
    col_chunk = lax.broadcasted_iota(jnp.int32, (1, span), 1) // c
    gain = gain_ref[...]
    for hh in range(HEADS_PER_STEP):
        klanes = slice(hh * GLA_DK, (hh + 1) * GLA_DK)
        vlanes = slice(hh * GLA_DV, (hh + 1) * GLA_DV)
        dec_t = dec_ref[:, klanes].T
        spans = range(rows // span)
        qt = [qt_ref[m * span:(m + 1) * span, klanes] for m in spans]
        ktt = [ktt_ref[hh, :, m * span:(m + 1) * span] for m in spans]
        kdt = [kdt_ref[hh, :, m * span:(m + 1) * span] for m in spans]
        v = [v_ref[m * span:(m + 1) * span, vlanes] for m in spans]
        a = [jnp.where(causal, jnp.dot(qt[m], ktt[m], preferred_element_type=F32), 0.0)
             for m in spans]
        intra = [jnp.dot(a[m].astype(BF16), v[m], preferred_element_type=F32) for m in spans]
        contrib = []
        for n in range(rows // c):
            m, sub = divmod(n * c, span)
            kd_n = jnp.where(col_chunk == sub // c, kdt[m], jnp.zeros_like(kdt[m]))
            contrib.append(jnp.dot(kd_n, v[m], preferred_element_type=F32))
        state = state_ref[hh]
        for n in range(rows // c):
            m, sub = divmod(n * c, span)
            inter = jnp.dot(qt[m][sub:sub + c], state.astype(BF16), preferred_element_type=F32)
            state = dec_t[:, n:n + 1] * state + contrib[n]
            o = _rms_rows(intra[m][sub:sub + c] + inter, gain)
            gate = _silu(r_ref[n * c:(n + 1) * c, vlanes].astype(F32))
            o_ref[n * c:(n + 1) * c, vlanes] = (o * gate).astype(o_ref.dtype)
        state_ref[hh] = state


def _gla(qt, ktt, kdt, v, r, dec, gain):
    b, s, kw = qt.shape
    vw = v.shape[-1]
    rows = min(GLA_GROUP, s)
    hps = HEADS_PER_STEP
    qblk = pl.BlockSpec((None, rows, hps * GLA_DK), lambda bi, h, n: (bi, n, h))
    tblk = pl.BlockSpec((None, hps, GLA_DK, rows), lambda bi, h, n: (bi, h, 0, n))
    vblk = pl.BlockSpec((None, rows, hps * GLA_DV), lambda bi, h, n: (bi, n, h))
    dblk = pl.BlockSpec((None, rows // GLA_CHUNK, hps * GLA_DK), lambda bi, h, n: (bi, n, h))
    return pl.pallas_call(
        _gla_kernel,
        grid=(b, GLA_HEADS // hps, s // rows),
        in_specs=[qblk, tblk, tblk, vblk, vblk, dblk, _const_spec((1, GLA_DV))],
        out_specs=vblk,
        out_shape=jax.ShapeDtypeStruct((b, s, vw), BF16),
        scratch_shapes=[pltpu.VMEM((hps, GLA_DK, GLA_DV), F32)],
        compiler_params=_params("parallel", "parallel", "arbitrary"),
        name="gla",
    )(qt, ktt, kdt, v, r, dec, gain)


def _rope_tables(seq, tile):
    half = RET_HEAD_DIM // 2
    inv = ROPE_BASE ** (-np.arange(half, dtype=np.float64) / half)

    def both(pos):
        ang = pos[:, None] * inv[None, :]
        dup = lambda t: np.concatenate([t, t], axis=-1).astype(np.float32)
        return dup(np.cos(ang)), dup(np.sin(ang))

    cos_a, sin_a = both(np.arange(0, seq, tile, dtype=np.float64))
    cos_b, sin_b = both(np.arange(tile, dtype=np.float64))
    return (jnp.asarray(cos_a[:, None, :]), jnp.asarray(sin_a[:, None, :]),
            jnp.asarray(cos_b), jnp.asarray(sin_b))


def _retention_tables(span):
    c = RET_CHUNK
    lg = np.log(1.0 - np.exp2(-5.0 - np.arange(RET_HEADS, dtype=np.float64)))[:, None]
    pos = np.arange(span, dtype=np.float64)
    diff = pos[:, None] - pos[None, :]
    same = (pos[:, None] // c) == (pos[None, :] // c)
    decay = np.where((diff >= 0) & same, np.exp(lg[:, :, None] * np.maximum(diff, 0.0)), 0.0)
    pos = np.arange(c, dtype=np.float64)
    zeta = np.exp(lg * (c - 1 - pos))[:, None, :]
    xi = np.exp(lg * (pos + 1.0))[:, :, None]
    g_chunk = np.exp(lg * c)[:, :, None]
    return tuple(jnp.asarray(t.astype(np.float32)) for t in (decay, zeta, xi, g_chunk))


def _suffix_tri(n):
    idx = np.arange(n)
    return jnp.asarray(idx[:, None] >= idx[None, :], dtype=BF16)


def _chunk_prefix_tri(rows, chunk):
    idx = np.arange(rows)
    same = (idx[:, None] // chunk) == (idx[None, :] // chunk)
    return jnp.asarray(same & (idx[None, :] <= idx[:, None]), dtype=BF16)


def _gate_operands(w_gate_in, w_alpha):
    rank = GLA_GATE_RANK
    wa = jnp.pad(jnp.tile(w_gate_in, (1, 3)), ((0, 0), (0, LANES - 3 * rank))).astype(BF16)
    w_hi = w_alpha.astype(BF16)
    w_lo = (w_alpha - w_hi.astype(F32)).astype(BF16)
    walpha = jnp.pad(jnp.concatenate([w_hi, w_hi, w_lo], axis=0), ((0, LANES - 3 * rank), (0, 0)))
    return wa, walpha


def kernel(x, mix_norm_g, even_w_in, sb_q_gain, sb_k_gain, ret_out_gain, even_w_out,
           odd_w_in, gla_w_alpha, gla_b_alpha, gla_out_gain, odd_w_out,
           ffn_norm_g, ffn_w_up, ffn_conv_w, ffn_conv_b, ffn_w_down):
    b, s, d = x.shape
    t = b * s
    depth = mix_norm_g.shape[0]
    h = x.reshape(t, d)

    rope_tables = _rope_tables(s, min(ROW_TILE, s))
    decay, zeta, xi, g_chunk = _retention_tables(min(RET_SPAN, s))
    sb_blk = min(SB_BLOCK, s)
    sb_tri = _suffix_tri(sb_blk)
    gla_tri = _chunk_prefix_tri(min(GLA_SPAN, s), GLA_CHUNK)

    for layer in range(depth):
        g_mix = mix_norm_g[layer][None, :]
        if layer % 2 == 0:
            e = layer // 2
            qg = jnp.tile(sb_q_gain[e], 2)[None, :]
            kg = jnp.tile(sb_k_gain[e], 2)[None, :]
            sbq, sbkt, sbv, rq, rkt, rv, rg = _even_in(
                h, g_mix, even_w_in[e].astype(BF16), qg, kg, rope_tables, b, s, sb_blk)
            out_a = _sb_attention(sbq.reshape(b, s, SB_WIDTH), sbkt, sbv.reshape(b, s, SB_WIDTH),
                                  sb_tri, qg, kg)
            shp = (b, s, RET_WIDTH)
            out_b = _retention(rq.reshape(shp), rkt, rv.reshape(shp), rg.reshape(shp),
                               ret_out_gain[e][None, :], decay, zeta, xi, g_chunk)
            mix = (out_a.reshape(t, SB_WIDTH), out_b.reshape(t, RET_WIDTH))
            w_out = even_w_out[e]
        else:
            o = layer // 2
            w_in = odd_w_in[o]
            n_main = 2 * GLA_K_WIDTH + 2 * GLA_V_WIDTH
            wa, walpha = _gate_operands(w_in[:, n_main:], gla_w_alpha[o])
            qt, ktt, kdt, gv, gr, dec = _odd_in(h, g_mix, w_in[:, :n_main].astype(BF16), wa, walpha,
                                                gla_b_alpha[o][None, :], gla_tri, b, s)
            og = _gla(qt.reshape(b, s, GLA_K_WIDTH), ktt, kdt,
                      gv.reshape(b, s, GLA_V_WIDTH), gr.reshape(b, s, GLA_V_WIDTH),
                      dec.reshape(b, s // GLA_CHUNK, GLA_K_WIDTH), gla_out_gain[o][None, :])
            mix = (og.reshape(t, GLA_V_WIDTH),)
            w_out = odd_w_out[o]
        h = _ffn(h, mix, w_out.astype(BF16), ffn_norm_g[layer][None, :], ffn_w_up[layer].astype(BF16),
                 ffn_conv_w[layer], ffn_conv_b[layer][None, :], ffn_w_down[layer].astype(BF16), s)
    return h.reshape(b, s, d)
```

```python
import functools

import numpy as np

import jax
import jax.numpy as jnp
from jax import lax
from jax.experimental import pallas as pl
from jax.experimental.pallas import tpu as pltpu

F32 = jnp.float32
BF16 = jnp.bfloat16

EPS = 1e-6
LOG2E = 1.4426950408889634
ROPE_BASE = 10000.0
EXP2_UNDERFLOW = 150.0
SB_NORM_SLACK = 1.05
SOFTPLUS2_LINEAR_ABOVE = 32.0
MASKED_SCORE = -1e30

SB_HEADS = 8
SB_HEAD_DIM = 64
SB_WIDTH = SB_HEADS * SB_HEAD_DIM
RET_HEADS = 4
RET_HEAD_DIM = 128
RET_WIDTH = RET_HEADS * RET_HEAD_DIM
RET_CHUNK = 128
GLA_HEADS = 4
GLA_DK = 128
GLA_DV = 256
GLA_K_WIDTH = GLA_HEADS * GLA_DK
GLA_V_WIDTH = GLA_HEADS * GLA_DV
GLA_GATE_RANK = 16
GLA_GATE_TAU = 16.0
GLA_CHUNK = 64
CONV_WIDTH = 3

LANES = 128
SUBLANES = 8
MXU_DIM = 256
VMEM_LIMIT = 56 * 1024 * 1024

ROW_TILE = 512
SB_BLOCK = MXU_DIM
SB_QBLOCKS_PER_STEP = 2
HEADS_PER_STEP = 2
RET_GROUP = 1024
RET_SPAN = 2 * RET_CHUNK
GLA_GROUP = ROW_TILE
GLA_SPAN = 4 * GLA_CHUNK
FF_CHUNK = 256


def _params(*sem):
    return pltpu.CompilerParams(dimension_semantics=sem, vmem_limit_bytes=VMEM_LIMIT)


def _const_spec(shape):
    n = len(shape)
    return pl.BlockSpec(shape, lambda *_: (0,) * n)


def _rms_rows(x, g):
    ms = jnp.mean(x * x, axis=-1, keepdims=True)
    return x * lax.rsqrt(ms + EPS) * g


def _silu(x):
    return x / (1.0 + jnp.exp(-x))


def _half_head_rms(blk):
    sq = blk * blk
    lane = lax.broadcasted_iota(jnp.int32, blk.shape, 1)
    lo = lane < SB_HEAD_DIM
    s_lo = jnp.sum(jnp.where(lo, sq, 0.0), axis=-1, keepdims=True)
    s_hi = jnp.sum(jnp.where(lo, 0.0, sq), axis=-1, keepdims=True)
    ms = jnp.where(lo, s_lo, s_hi) * (1.0 / SB_HEAD_DIM)
    return blk * lax.rsqrt(ms + EPS)


def _even_in_kernel(x_ref, g_ref, w_ref, qg_ref, kg_ref, ca_ref, sa_ref, cb_ref, sb_ref,
                    sbq_ref, sbkt_ref, sbv_ref, rq_ref, rkt_ref, rv_ref, rg_ref):
    tm = x_ref.shape[0]
    hn = _rms_rows(x_ref[...], g_ref[...]).astype(BF16)

    def proj(seg):
        return jnp.dot(hn, w_ref[:, seg * SB_WIDTH:(seg + 1) * SB_WIDTH],
                       preferred_element_type=F32)

    q_scale = SB_HEAD_DIM ** -0.5 * LOG2E
    p = proj(0)
    for hp in range(SB_WIDTH // LANES):
        sl = slice(hp * LANES, (hp + 1) * LANES)
        sbq_ref[:, sl] = (_half_head_rms(p[:, sl]) * (qg_ref[...] * q_scale)).astype(BF16)
    p = proj(1)
    blk = sbkt_ref.shape[-1]
    for hp in range(SB_WIDTH // LANES):
        sl = slice(hp * LANES, (hp + 1) * LANES)
        kt = (_half_head_rms(p[:, sl]) * kg_ref[...]).T
        for c in range(tm // blk):
            sbkt_ref[hp, c] = kt[:, c * blk:(c + 1) * blk].astype(BF16)
    sbv_ref[...] = proj(2).astype(BF16)

    cos_a, sin_a, cos_b, sin_b = ca_ref[...], sa_ref[...], cb_ref[...], sb_ref[...]
    sign = jnp.where(lax.broadcasted_iota(jnp.int32, (1, LANES), 1) < RET_HEAD_DIM // 2, -1.0, 1.0)
    cos2 = cos_a * cos_b - sin_a * sin_b
    sin2 = sign * (sin_a * cos_b + cos_a * sin_b)

    def rope(blk):
        return blk * cos2 + pltpu.roll(blk, RET_HEAD_DIM // 2, 1) * sin2

    p = proj(3)
    for h in range(RET_HEADS):
        sl = slice(h * LANES, (h + 1) * LANES)
        rq_ref[:, sl] = rope(p[:, sl]).astype(BF16)
    p = proj(4)
    for h in range(RET_HEADS):
        sl = slice(h * LANES, (h + 1) * LANES)
        rkt_ref[h] = (rope(p[:, sl]) * RET_HEAD_DIM ** -0.5).T.astype(BF16)
    rv_ref[...] = proj(5).astype(BF16)
    rg_ref[...] = proj(6).astype(BF16)


def _even_in(x2, g, w, qg, kg, rope_tables, batch, seq, sb_blk):
    t, d = x2.shape
    tm = min(ROW_TILE, seq)
    tps = seq // tm
    tile_start = pl.BlockSpec((None, 1, LANES), lambda i: (i % tps, 0, 0))
    row = lambda i: (i, 0)
    pairs = SB_WIDTH // LANES
    flat = jax.ShapeDtypeStruct((t, SB_WIDTH), BF16)
    flat_spec = pl.BlockSpec((tm, SB_WIDTH), row)
    sbkt = jax.ShapeDtypeStruct((batch, pairs, seq // sb_blk, LANES, sb_blk), BF16)
    sbkt_spec = pl.BlockSpec((None, pairs, tm // sb_blk, LANES, sb_blk),
                             lambda i: (i // tps, 0, i % tps, 0, 0))
    rkt = jax.ShapeDtypeStruct((batch, RET_HEADS, RET_HEAD_DIM, seq), BF16)
    rkt_spec = pl.BlockSpec((None, RET_HEADS, RET_HEAD_DIM, tm), lambda i: (i // tps, 0, 0, i % tps))
    return pl.pallas_call(
        _even_in_kernel,
        grid=(t // tm,),
        in_specs=[
            pl.BlockSpec((tm, d), row),
            _const_spec((1, d)),
            _const_spec(w.shape),
            _const_spec((1, LANES)),
            _const_spec((1, LANES)),
            tile_start,
            tile_start,
            _const_spec((tm, LANES)),
            _const_spec((tm, LANES)),
        ],
        out_specs=[flat_spec, sbkt_spec, flat_spec, flat_spec, rkt_spec, flat_spec, flat_spec],
        out_shape=[flat, sbkt, flat, flat, rkt, flat, flat],
        compiler_params=_params("parallel"),
        name="even_in",
    )(x2, g, w, qg, kg, *rope_tables)


def _softplus2(z):
    return jnp.where(z > SOFTPLUS2_LINEAR_ABOVE, z, jnp.log2(1.0 + jnp.exp2(z)))


def _sb_kernel(q_ref, kt_ref, v_ref, tri_ref, qg_ref, kg_ref, o_ref, acc_ref, carry_ref):
    step = pl.program_id(2)
    blk = kt_ref.shape[-1]
    per_step = q_ref.shape[0] // blk
    z_bound = (SB_NORM_SLACK * SB_HEAD_DIM * SB_HEAD_DIM ** -0.5 * LOG2E
               * jnp.max(jnp.abs(qg_ref[...])) * jnp.max(jnp.abs(kg_ref[...])))
    dead_above = z_bound + EXP2_UNDERFLOW
    lane = lax.broadcasted_iota(jnp.int32, (blk, LANES), 1)
    lo = lane < SB_HEAD_DIM
    row_id = lax.broadcasted_iota(jnp.int32, (blk, blk), 0)
    col_id = lax.broadcasted_iota(jnp.int32, (blk, blk), 1)
    below_diag = col_id < row_id
    tri = tri_ref[...]
    heads = range(2)

    def split_heads(q):
        zero = jnp.zeros_like(q)
        return (jnp.where(lo, q, zero), jnp.where(lo, zero, q))

    def key_blocks(j_a):
        j_b = jnp.maximum(j_a - 1, 0)
        return j_a > 0, j_b

    def values(j):
        return v_ref[pl.ds(pl.multiple_of(j * blk, blk), blk), :]

    def diagonal_pairs(q_heads, j_as):
        chains = [(n, h) for n in range(len(j_as)) for h in heads]
        has_b, kt_a, kt_b, v_a, v_b = [], [], [], [], []
        for j_a in j_as:
            flag, j_b = key_blocks(j_a)
            has_b.append(flag)
            kt_a.append(kt_ref[j_a])
            kt_b.append(kt_ref[j_b])
            v_a.append(values(j_a))
            v_b.append(values(j_b))
        z_a = {(n, h): jnp.dot(q_heads[n][h], kt_a[n], preferred_element_type=F32) for n, h in chains}
        z_b = {(n, h): jnp.dot(q_heads[n][h], kt_b[n], preferred_element_type=F32) for n, h in chains}
        z_a = {c: jnp.where(below_diag, z, MASKED_SCORE) for c, z in z_a.items()}
        sp_a = {c: _softplus2(z) for c, z in z_a.items()}
        sp_b = {c: _softplus2(z) for c, z in z_b.items()}
        cs_a = {c: jnp.dot(sp.astype(BF16), tri, preferred_element_type=F32) for c, sp in sp_a.items()}
        cs_b = {c: jnp.dot(sp.astype(BF16), tri, preferred_element_type=F32) for c, sp in sp_b.items()}
        w_a, w_b, carry_out = {}, {}, {}
        for n, h in chains:
            c = (n, h)
            tot_a = jnp.sum(sp_a[c], axis=-1, keepdims=True)
            tot_b = jnp.where(has_b[n], jnp.sum(sp_b[c], axis=-1, keepdims=True), 0.0)
            before_b = jnp.where(has_b[n], tot_a, -MASKED_SCORE)
            w_a[c] = jnp.exp2(z_a[c] - cs_a[c])
            w_b[c] = jnp.exp2(z_b[c] - (cs_b[c] + before_b))
            carry_out[c] = tot_a + tot_b
        for n, h in chains:
            c = (n, h)
            acc_ref[n, h] = (jnp.dot(w_a[c].astype(BF16), v_a[n], preferred_element_type=F32)
                             + jnp.dot(w_b[c].astype(BF16), v_b[n], preferred_element_type=F32))
            carry_ref[n, h] = carry_out[c]

    first = step * per_step
    q_heads = [split_heads(q_ref[n * blk:(n + 1) * blk, :]) for n in range(per_step)]
    diagonal_pairs(q_heads, [first + n for n in range(per_step)])

    def older_pair(n, j_a):
        has_b, j_b = key_blocks(j_a)
        kt_a, kt_b = kt_ref[j_a], kt_ref[j_b]
        v_a, v_b = values(j_a), values(j_b)
        z_a = [jnp.dot(q_heads[n][h], kt_a, preferred_element_type=F32) for h in heads]
        z_b = [jnp.dot(q_heads[n][h], kt_b, preferred_element_type=F32) for h in heads]
        sp_a = [_softplus2(z) for z in z_a]
        sp_b = [_softplus2(z) for z in z_b]
        cs_a = [jnp.dot(sp.astype(BF16), tri, preferred_element_type=F32) for sp in sp_a]
        cs_b = [jnp.dot(sp.astype(BF16), tri, preferred_element_type=F32) for sp in sp_b]
        for h in heads:
            carry = carry_ref[n, h]
            after_a = carry + jnp.sum(sp_a[h], axis=-1, keepdims=True)
            tot_b = jnp.where(has_b, jnp.sum(sp_b[h], axis=-1, keepdims=True), 0.0)
            before_b = jnp.where(has_b, after_a, -MASKED_SCORE)
            w_a = jnp.exp2(z_a[h] - (cs_a[h] + carry))
            w_b = jnp.exp2(z_b[h] - (cs_b[h] + before_b))
            acc_ref[n, h] += (jnp.dot(w_a.astype(BF16), v_a, preferred_element_type=F32)
                              + jnp.dot(w_b.astype(BF16), v_b, preferred_element_type=F32))
            carry_ref[n, h] = after_a + tot_b

    def live(n):
        return jnp.minimum(jnp.min(carry_ref[n, 0]), jnp.min(carry_ref[n, 1])) <= dead_above

    for n in range(per_step):
        def body(c, n=n):
            older_pair(n, c[0])
            return c[0] - 2, live(n)

        lax.while_loop(lambda c: jnp.logical_and(c[0] >= 0, c[1]), body, (first + n - 2, live(n)))
        o_ref[n * blk:(n + 1) * blk, :] = jnp.where(lo, acc_ref[n, 0], acc_ref[n, 1]).astype(o_ref.dtype)


def _sb_attention(q, kt, v, tri, qg, kg):
    b, s, width = q.shape
    blk = kt.shape[-1]
    n_blocks = s // blk
    per_step = min(SB_QBLOCKS_PER_STEP, n_blocks)
    assert n_blocks % per_step == 0
    rows = per_step * blk
    return pl.pallas_call(
        _sb_kernel,
        grid=(b, width // LANES, n_blocks // per_step),
        in_specs=[
            pl.BlockSpec((None, rows, LANES), lambda bi, hp, i: (bi, i, hp)),
            pl.BlockSpec((None, None, n_blocks, LANES, blk), lambda bi, hp, i: (bi, hp, 0, 0, 0)),
            pl.BlockSpec((None, s, LANES), lambda bi, hp, i: (bi, 0, hp)),
            _const_spec(tri.shape),
            _const_spec((1, LANES)),
            _const_spec((1, LANES)),
        ],
        out_specs=pl.BlockSpec((None, rows, LANES), lambda bi, hp, i: (bi, i, hp)),
        out_shape=jax.ShapeDtypeStruct((b, s, width), BF16),
        scratch_shapes=[pltpu.VMEM((per_step, 2, blk, LANES), F32),
                        pltpu.VMEM((per_step, 2, blk, 1), F32)],
        compiler_params=_params("parallel", "parallel", "arbitrary"),
        name="sb_attention",
    )(q, kt, v, tri, qg, kg)


def _ret_kernel(q_ref, kt_ref, v_ref, g_ref, gain_ref, decay_ref, zeta_ref, xi_ref, gc_ref,
                o_ref, state_ref):
    @pl.when(pl.program_id(2) == 0)
    def _():
        state_ref[...] = jnp.zeros_like(state_ref)

    c = RET_CHUNK
    d = RET_HEAD_DIM
    span = decay_ref.shape[-1]
    rows = q_ref.shape[0]
    gain = gain_ref[...]
    for hh in range(HEADS_PER_STEP):
        lanes = slice(hh * d, (hh + 1) * d)
        decay = decay_ref[hh]
        zeta = zeta_ref[hh]
        xi = xi_ref[hh]
        g_chunk = gc_ref[hh]
        q = [q_ref[m * span:(m + 1) * span, lanes] for m in range(rows // span)]
        kt = [kt_ref[hh, :, m * span:(m + 1) * span] for m in range(rows // span)]
        v = [v_ref[m * span:(m + 1) * span, lanes] for m in range(rows // span)]
        scores = [jnp.dot(q[m], kt[m], preferred_element_type=F32) * decay
                  for m in range(rows // span)]
        inner = [jnp.dot(scores[m].astype(BF16), v[m], preferred_element_type=F32)
                 for m in range(rows // span)]
        contrib = []
        for n in range(rows // c):
            m, sub = divmod(n * c, span)
            kz = (kt[m][:, sub:sub + c].astype(F32) * zeta).astype(BF16)
            contrib.append(jnp.dot(kz, v[m][sub:sub + c], preferred_element_type=F32))
        state = state_ref[hh]
        for n in range(rows // c):
            m, sub = divmod(n * c, span)
            cross = jnp.dot(q[m][sub:sub + c], state.astype(BF16), preferred_element_type=F32) * xi
            state = g_chunk * state + contrib[n]
            ret = _rms_rows(inner[m][sub:sub + c] + cross, gain)
            gate = _silu(g_ref[n * c:(n + 1) * c, lanes].astype(F32))
            o_ref[n * c:(n + 1) * c, lanes] = (ret * gate).astype(o_ref.dtype)
        state_ref[hh] = state


def _retention(q, kt, v, g, gain, decay, zeta, xi, gc):
    b, s, width = q.shape
    rows = min(RET_GROUP, s)
    span = decay.shape[-1]
    hps = HEADS_PER_STEP
    blk = pl.BlockSpec((None, rows, hps * RET_HEAD_DIM), lambda bi, h, n: (bi, n, h))
    head = lambda shape: pl.BlockSpec((hps,) + shape, lambda bi, h, n: (h, 0, 0))
    return pl.pallas_call(
        _ret_kernel,
        grid=(b, RET_HEADS // hps, s // rows),
        in_specs=[blk,
                  pl.BlockSpec((None, hps, RET_HEAD_DIM, rows), lambda bi, h, n: (bi, h, 0, n)),
                  blk, blk, _const_spec((1, RET_HEAD_DIM)),
                  head((span, span)), head((1, RET_CHUNK)), head((RET_CHUNK, 1)), head((1, 1))],
        out_specs=blk,
        out_shape=jax.ShapeDtypeStruct((b, s, width), BF16),
        scratch_shapes=[pltpu.VMEM((hps, RET_HEAD_DIM, RET_HEAD_DIM), F32)],
        compiler_params=_params("parallel", "parallel", "arbitrary"),
        name="retention",
    )(q, kt, v, g, gain, decay, zeta, xi, gc)


def _ffn_kernel(*refs, n_mix, tiles_per_seq):
    h_ref = refs[0]
    mix_refs = refs[1:1 + n_mix]
    (wo_ref, g_ref, wup_ref, cw_ref, cb_ref, wd_ref, o_ref,
     ubuf_ref, prev_ref, gate_ref) = refs[1 + n_mix:]
    tm = h_ref.shape[0]
    d_ff = wd_ref.shape[0]
    halo = SUBLANES

    @pl.when(pl.program_id(0) % tiles_per_seq == 0)
    def _():
        prev_ref[...] = jnp.zeros_like(prev_ref)

    h = h_ref[...]
    row0 = 0
    for a_ref in mix_refs:
        width = a_ref.shape[1]
        h = h + jnp.dot(a_ref[...], wo_ref[row0:row0 + width, :], preferred_element_type=F32)
        row0 += width
    hn = _rms_rows(h, g_ref[...]).astype(BF16)
    for c0 in range(0, d_ff, FF_CHUNK):
        cols = slice(c0, c0 + FF_CHUNK)
        u = jnp.dot(hn, wup_ref[:, cols], preferred_element_type=F32)
        v = jnp.dot(hn, wup_ref[:, d_ff + c0:d_ff + c0 + FF_CHUNK], preferred_element_type=F32)
        ubuf_ref[0:halo, :] = prev_ref[:, cols]
        ubuf_ref[halo:halo + tm, :] = u
        prev_ref[:, cols] = u[tm - halo:, :]
        uc = (cb_ref[:, cols]
              + cw_ref[0:1, cols] * ubuf_ref[halo - 2:halo - 2 + tm, :]
              + cw_ref[1:2, cols] * ubuf_ref[halo - 1:halo - 1 + tm, :]
              + cw_ref[2:3, cols] * u)
        gate_ref[:, cols] = (_silu(uc) * v).astype(BF16)
    o_ref[...] = h + jnp.dot(gate_ref[...], wd_ref[...], preferred_element_type=F32)


def _ffn(h2, mix, wo, g, wup, cw, cb, wd, seq):
    t, d = h2.shape
    d_ff = wd.shape[0]
    tm = min(ROW_TILE, seq)
    row = lambda i: (i, 0)
    return pl.pallas_call(
        functools.partial(_ffn_kernel, n_mix=len(mix), tiles_per_seq=seq // tm),
        grid=(t // tm,),
        in_specs=([pl.BlockSpec((tm, d), row)]
                  + [pl.BlockSpec((tm, a.shape[1]), row) for a in mix]
                  + [_const_spec(wo.shape), _const_spec((1, d)), _const_spec(wup.shape),
                     _const_spec(cw.shape), _const_spec(cb.shape), _const_spec(wd.shape)]),
        out_specs=pl.BlockSpec((tm, d), row),
        out_shape=jax.ShapeDtypeStruct((t, d), F32),
        scratch_shapes=[pltpu.VMEM((tm + SUBLANES, FF_CHUNK), F32),
                        pltpu.VMEM((SUBLANES, d_ff), F32),
                        pltpu.VMEM((tm, d_ff), BF16)],
        compiler_params=_params("arbitrary"),
        name="conv_ffn",
    )(h2, *mix, wo, g, wup, cw, cb, wd)


def _log_sigmoid(x):
    return jnp.minimum(x, 0.0) - jnp.log(1.0 + jnp.exp(-jnp.abs(x)))


def _odd_in_kernel(x_ref, g_ref, w_ref, wa_ref, walpha_ref, balpha_ref, tri_ref,
                   qt_ref, ktt_ref, kdt_ref, v_ref, r_ref, dec_ref):
    tm = x_ref.shape[0]
    span = tri_ref.shape[0]
    hn = _rms_rows(x_ref[...], g_ref[...]).astype(BF16)

    def proj(c0, width):
        return jnp.dot(hn, w_ref[:, c0:c0 + width], preferred_element_type=F32)

    ga = jnp.dot(hn, wa_ref[...], preferred_element_type=F32)
    ga_hi = ga.astype(BF16)
    ga_lo = (ga - ga_hi.astype(F32)).astype(BF16)
    lane = lax.broadcasted_iota(jnp.int32, ga.shape, 1)
    middle = jnp.logical_and(lane >= GLA_GATE_RANK, lane < 2 * GLA_GATE_RANK)
    pre = jnp.dot(jnp.where(middle, ga_lo, ga_hi), walpha_ref[...],
                  preferred_element_type=F32) + balpha_ref[...]
    la = _log_sigmoid(pre) * (LOG2E / GLA_GATE_TAU)
    la_hi = la.astype(BF16)
    la_lo = (la - la_hi.astype(F32)).astype(BF16)
    tri = tri_ref[...]
    cum = jnp.concatenate(
        [jnp.dot(tri, la_hi[r0:r0 + span], preferred_element_type=F32)
         + jnp.dot(tri, la_lo[r0:r0 + span], preferred_element_type=F32)
         for r0 in range(0, tm, span)], axis=0)
    chunks = tm // GLA_CHUNK
    last = cum.reshape(chunks, GLA_CHUNK, GLA_K_WIDTH)[:, GLA_CHUNK - 1:GLA_CHUNK, :]
    dec_ref[...] = jnp.exp2(last).reshape(chunks, GLA_K_WIDTH)
    to_end = (jnp.broadcast_to(last, (chunks, GLA_CHUNK, GLA_K_WIDTH)).reshape(tm, GLA_K_WIDTH)
              - cum)

    qt_ref[...] = (proj(0, GLA_K_WIDTH) * GLA_DK ** -0.5 * jnp.exp2(cum)).astype(BF16)
    k = proj(GLA_K_WIDTH, GLA_K_WIDTH)
    kt = k * jnp.exp2(-cum)
    kd = k * jnp.exp2(to_end)
    for h in range(GLA_HEADS):
        sl = slice(h * GLA_DK, (h + 1) * GLA_DK)
        ktt_ref[h] = kt[:, sl].T.astype(BF16)
        kdt_ref[h] = kd[:, sl].T.astype(BF16)
    v_ref[...] = proj(2 * GLA_K_WIDTH, GLA_V_WIDTH).astype(BF16)
    r_ref[...] = proj(2 * GLA_K_WIDTH + GLA_V_WIDTH, GLA_V_WIDTH).astype(BF16)


def _odd_in(x2, g, w, wa, walpha, balpha, tri, batch, seq):
    t, d = x2.shape
    tm = min(ROW_TILE, seq)
    tps = seq // tm
    row = lambda i: (i, 0)
    kt_shape = jax.ShapeDtypeStruct((batch, GLA_HEADS, GLA_DK, seq), BF16)
    kt_spec = pl.BlockSpec((None, GLA_HEADS, GLA_DK, tm), lambda i: (i // tps, 0, 0, i % tps))
    vout = jax.ShapeDtypeStruct((t, GLA_V_WIDTH), BF16)
    return pl.pallas_call(
        _odd_in_kernel,
        grid=(t // tm,),
        in_specs=[pl.BlockSpec((tm, d), row), _const_spec((1, d)), _const_spec(w.shape),
                  _const_spec(wa.shape), _const_spec(walpha.shape), _const_spec(balpha.shape),
                  _const_spec(tri.shape)],
        out_specs=[pl.BlockSpec((tm, GLA_K_WIDTH), row), kt_spec, kt_spec,
                   pl.BlockSpec((tm, GLA_V_WIDTH), row), pl.BlockSpec((tm, GLA_V_WIDTH), row),
                   pl.BlockSpec((tm // GLA_CHUNK, GLA_K_WIDTH), row)],
        out_shape=[jax.ShapeDtypeStruct((t, GLA_K_WIDTH), BF16), kt_shape, kt_shape, vout, vout,
                   jax.ShapeDtypeStruct((t // GLA_CHUNK, GLA_K_WIDTH), F32)],
        compiler_params=_params("parallel"),
        name="odd_in",
    )(x2, g, w, wa, walpha, balpha, tri)


def _gla_kernel(qt_ref, ktt_ref, kdt_ref, v_ref, r_ref, dec_ref, gain_ref, o_ref, state_ref):
    @pl.when(pl.program_id(2) == 0)
    def _():
        state_ref[...] = jnp.zeros_like(state_ref)

    c = GLA_CHUNK
    rows = qt_ref.shape[0]
    span = min(GLA_SPAN, rows)
    ri = lax.broadcasted_iota(jnp.int32, (span, span), 0)
    ci = lax.broadcasted_iota(jnp.int32, (span, span), 1)
    causal = jnp.logical_and(ci <= ri, ci // c == ri // c)
    col_chunk = lax.broadcasted_iota(jnp.int32, (1, span), 1) // c
    gain = gain_ref[...]
    for hh in range(HEADS_PER_STEP):
        klanes = slice(hh * GLA_DK, (hh + 1) * GLA_DK)
        vlanes = slice(hh * GLA_DV, (hh + 1) * GLA_DV)
        dec_t = dec_ref[:, klanes].T
        spans = range(rows // span)
        qt = [qt_ref[m * span:(m + 1) * span, klanes] for m in spans]
        ktt = [ktt_ref[hh, :, m * span:(m + 1) * span] for m in spans]
        kdt = [kdt_ref[hh, :, m * span:(m + 1) * span] for m in spans]
        v = [v_ref[m * span:(m + 1) * span, vlanes] for m in spans]
        a = [jnp.where(causal, jnp.dot(qt[m], ktt[m], preferred_element_type=F32), 0.0)
             for m in spans]
        intra = [jnp.dot(a[m].astype(BF16), v[m], preferred_element_type=F32) for m in spans]
        contrib = []
        for n in range(rows // c):
            m, sub = divmod(n * c, span)
            kd_n = jnp.where(col_chunk == sub // c, kdt[m], jnp.zeros_like(kdt[m]))
            contrib.append(jnp.dot(kd_n, v[m], preferred_element_type=F32))
        state = state_ref[hh]
        for n in range(rows // c):
            m, sub = divmod(n * c, span)
            inter = jnp.dot(qt[m][sub:sub + c], state.astype(BF16), preferred_element_type=F32)
            state = dec_t[:, n:n + 1] * state + contrib[n]
            o = _rms_rows(intra[m][sub:sub + c] + inter, gain)
            gate = _silu(r_ref[n * c:(n + 1) * c, vlanes].astype(F32))
            o_ref[n * c:(n + 1) * c, vlanes] = (o * gate).astype(o_ref.dtype)
        state_ref[hh] = state


def _gla(qt, ktt, kdt, v, r, dec, gain):
    b, s, kw = qt.shape
    vw = v.shape[-1]
    rows = min(GLA_GROUP, s)
    hps = HEADS_PER_STEP
    qblk = pl.BlockSpec((None, rows, hps * GLA_DK), lambda bi, h, n: (bi, n, h))
    tblk = pl.BlockSpec((None, hps, GLA_DK, rows), lambda bi, h, n: (bi, h, 0, n))
    vblk = pl.BlockSpec((None, rows, hps * GLA_DV), lambda bi, h, n: (bi, n, h))
    dblk = pl.BlockSpec((None, rows // GLA_CHUNK, hps * GLA_DK), lambda bi, h, n: (bi, n, h))
    return pl.pallas_call(
        _gla_kernel,
        grid=(b, GLA_HEADS // hps, s // rows),
        in_specs=[qblk, tblk, tblk, vblk, vblk, dblk, _const_spec((1, GLA_DV))],
        out_specs=vblk,
        out_shape=jax.ShapeDtypeStruct((b, s, vw), BF16),
        scratch_shapes=[pltpu.VMEM((hps, GLA_DK, GLA_DV), F32)],
        compiler_params=_params("parallel", "parallel", "arbitrary"),
        name="gla",
    )(qt, ktt, kdt, v, r, dec, gain)


def _rope_tables(seq, tile):
    half = RET_HEAD_DIM // 2
    inv = ROPE_BASE ** (-np.arange(half, dtype=np.float64) / half)

    def both(pos):
        ang = pos[:, None] * inv[None, :]
        dup = lambda t: np.concatenate([t, t], axis=-1).astype(np.float32)
        return dup(np.cos(ang)), dup(np.sin(ang))

    cos_a, sin_a = both(np.arange(0, seq, tile, dtype=np.float64))
    cos_b, sin_b = both(np.arange(tile, dtype=np.float64))
    return (jnp.asarray(cos_a[:, None, :]), jnp.asarray(sin_a[:, None, :]),
            jnp.asarray(cos_b), jnp.asarray(sin_b))


def _retention_tables(span):
    c = RET_CHUNK
    lg = np.log(1.0 - np.exp2(-5.0 - np.arange(RET_HEADS, dtype=np.float64)))[:, None]
    pos = np.arange(span, dtype=np.float64)
    diff = pos[:, None] - pos[None, :]
    same = (pos[:, None] // c) == (pos[None, :] // c)
    decay = np.where((diff >= 0) & same, np.exp(lg[:, :, None] * np.maximum(diff, 0.0)), 0.0)
    pos = np.arange(c, dtype=np.float64)
    zeta = np.exp(lg * (c - 1 - pos))[:, None, :]
    xi = np.exp(lg * (pos + 1.0))[:, :, None]
    g_chunk = np.exp(lg * c)[:, :, None]
    return tuple(jnp.asarray(t.astype(np.float32)) for t in (decay, zeta, xi, g_chunk))


def _suffix_tri(n):
    idx = np.arange(n)
    return jnp.asarray(idx[:, None] >= idx[None, :], dtype=BF16)


def _chunk_prefix_tri(rows, chunk):
    idx = np.arange(rows)
    same = (idx[:, None] // chunk) == (idx[None, :] // chunk)
    return jnp.asarray(same & (idx[None, :] <= idx[:, None]), dtype=BF16)


def _gate_operands(w_gate_in, w_alpha):
    rank = GLA_GATE_RANK
    wa = jnp.pad(jnp.tile(w_gate_in, (1, 3)), ((0, 0), (0, LANES - 3 * rank))).astype(BF16)
    w_hi = w_alpha.astype(BF16)
    w_lo = (w_alpha - w_hi.astype(F32)).astype(BF16)
    walpha = jnp.pad(jnp.concatenate([w_hi, w_hi, w_lo], axis=0), ((0, LANES - 3 * rank), (0, 0)))
    return wa, walpha


def kernel(x, mix_norm_g, even_w_in, sb_q_gain, sb_k_gain, ret_out_gain, even_w_out,
           odd_w_in, gla_w_alpha, gla_b_alpha, gla_out_gain, odd_w_out,
           ffn_norm_g, ffn_w_up, ffn_conv_w, ffn_conv_b, ffn_w_down):
    b, s, d = x.shape
    t = b * s
    depth = mix_norm_g.shape[0]
    h = x.reshape(t, d)

    rope_tables = _rope_tables(s, min(ROW_TILE, s))
    decay, zeta, xi, g_chunk = _retention_tables(min(RET_SPAN, s))
    sb_blk = min(SB_BLOCK, s)
    sb_tri = _suffix_tri(sb_blk)
    gla_tri = _chunk_prefix_tri(min(GLA_SPAN, s), GLA_CHUNK)

    for layer in range(depth):
        g_mix = mix_norm_g[layer][None, :]
        if layer % 2 == 0:
            e = layer // 2
            qg = jnp.tile(sb_q_gain[e], 2)[None, :]
            kg = jnp.tile(sb_k_gain[e], 2)[None, :]
            sbq, sbkt, sbv, rq, rkt, rv, rg = _even_in(
                h, g_mix, even_w_in[e].astype(BF16), qg, kg, rope_tables, b, s, sb_blk)
            out_a = _sb_attention(sbq.reshape(b, s, SB_WIDTH), sbkt, sbv.reshape(b, s, SB_WIDTH),
                                  sb_tri, qg, kg)
            shp = (b, s, RET_WIDTH)
            out_b = _retention(rq.reshape(shp), rkt, rv.reshape(shp), rg.reshape(shp),
                               ret_out_gain[e][None, :], decay, zeta, xi, g_chunk)
            mix = (out_a.reshape(t, SB_WIDTH), out_b.reshape(t, RET_WIDTH))
            w_out = even_w_out[e]
        else:
            o = layer // 2
            w_in = odd_w_in[o]
            n_main = 2 * GLA_K_WIDTH + 2 * GLA_V_WIDTH
            wa, walpha = _gate_operands(w_in[:, n_main:], gla_w_alpha[o])
            qt, ktt, kdt, gv, gr, dec = _odd_in(h, g_mix, w_in.astype(BF16), wa, walpha,
                                                gla_b_alpha[o][None, :], gla_tri, b, s)
            og = _gla(qt.reshape(b, s, GLA_K_WIDTH), ktt, kdt,
                      gv.reshape(b, s, GLA_V_WIDTH), gr.reshape(b, s, GLA_V_WIDTH),
                      dec.reshape(b, s // GLA_CHUNK, GLA_K_WIDTH), gla_out_gain[o][None, :])
            mix = (og.reshape(t, GLA_V_WIDTH),)
            w_out = odd_w_out[o]
        h = _ffn(h, mix, w_out.astype(BF16), ffn_norm_g[layer][None, :], ffn_w_up[layer].astype(BF16),
                 ffn_conv_w[layer], ffn_conv_b[layer][None, :], ffn_w_down[layer].astype(BF16), s)
    return h.reshape(b, s, d)
```

```python
import functools

import numpy as np

import jax
import jax.numpy as jnp
from jax import lax
from jax.experimental import pallas as pl
from jax.experimental.pallas import tpu as pltpu

F32 = jnp.float32
BF16 = jnp.bfloat16

EPS = 1e-6
LOG2E = 1.4426950408889634
ROPE_BASE = 10000.0
EXP2_UNDERFLOW = 150.0
SB_NORM_SLACK = 1.05
SOFTPLUS2_LINEAR_ABOVE = 32.0
MASKED_SCORE = -1e30

SB_HEADS = 8
SB_HEAD_DIM = 64
SB_WIDTH = SB_HEADS * SB_HEAD_DIM
RET_HEADS = 4
RET_HEAD_DIM = 128
RET_WIDTH = RET_HEADS * RET_HEAD_DIM
RET_CHUNK = 128
GLA_HEADS = 4
GLA_DK = 128
GLA_DV = 256
GLA_K_WIDTH = GLA_HEADS * GLA_DK
GLA_V_WIDTH = GLA_HEADS * GLA_DV
GLA_GATE_RANK = 16
GLA_GATE_TAU = 16.0
GLA_CHUNK = 64
CONV_WIDTH = 3

LANES = 128
SUBLANES = 8
MXU_DIM = 256
VMEM_LIMIT = 56 * 1024 * 1024

ROW_TILE = 1024
FFN_ROW_TILE = 1024
SB_BLOCK = MXU_DIM
SB_QBLOCKS_PER_STEP = 2
RET_HEADS_PER_STEP = 4
RET_GROUP = 1024
RET_SPAN = 2 * RET_CHUNK
GLA_GROUP = 512
GLA_HEADS_PER_STEP = 4
GLA_SPAN = 4 * GLA_CHUNK
FF_CHUNK = 256


def _params(*sem):
    return pltpu.CompilerParams(dimension_semantics=sem, vmem_limit_bytes=VMEM_LIMIT)


def _const_spec(shape):
    n = len(shape)
    return pl.BlockSpec(shape, lambda *_: (0,) * n)


def _resident_spec(shape):
    n = len(shape)
    return pl.BlockSpec(shape, lambda *_: (0,) * n, pipeline_mode=pl.Buffered(1))


def _rms_rows(x, g):
    ms = jnp.mean(x * x, axis=-1, keepdims=True)
    return x * lax.rsqrt(ms + EPS) * g


def _silu(x):
    return x / (1.0 + jnp.exp(-x))


def _half_head_rms(blk):
    sq = blk * blk
    lane = lax.broadcasted_iota(jnp.int32, blk.shape, 1)
    lo = lane < SB_HEAD_DIM
    s_lo = jnp.sum(jnp.where(lo, sq, 0.0), axis=-1, keepdims=True)
    s_hi = jnp.sum(jnp.where(lo, 0.0, sq), axis=-1, keepdims=True)
    ms = jnp.where(lo, s_lo, s_hi) * (1.0 / SB_HEAD_DIM)
    return blk * lax.rsqrt(ms + EPS)


def _even_in_kernel(x_ref, g_ref, w_ref, qg_ref, kg_ref, ca_ref, sa_ref, cb_ref, sb_ref,
                    sbq_ref, sbkt_ref, sbv_ref, rq_ref, rkt_ref, rv_ref, rg_ref):
    tm = x_ref.shape[0]
    hn = _rms_rows(x_ref[...], g_ref[...]).astype(BF16)

    def proj(seg):
        return jnp.dot(hn, w_ref[:, seg * SB_WIDTH:(seg + 1) * SB_WIDTH],
                       preferred_element_type=F32)

    q_scale = SB_HEAD_DIM ** -0.5 * LOG2E
    p = proj(0)
    for hp in range(SB_WIDTH // LANES):
        sl = slice(hp * LANES, (hp + 1) * LANES)
        sbq_ref[:, sl] = (_half_head_rms(p[:, sl]) * (qg_ref[...] * q_scale)).astype(BF16)
    p = proj(1)
    blk = sbkt_ref.shape[-1]
    for hp in range(SB_WIDTH // LANES):
        sl = slice(hp * LANES, (hp + 1) * LANES)
        kt = (_half_head_rms(p[:, sl]) * kg_ref[...]).T
        for c in range(tm // blk):
            sbkt_ref[hp, c] = kt[:, c * blk:(c + 1) * blk].astype(BF16)
    sbv_ref[...] = proj(2).astype(BF16)

    cos_a, sin_a, cos_b, sin_b = ca_ref[...], sa_ref[...], cb_ref[...], sb_ref[...]
    sign = jnp.where(lax.broadcasted_iota(jnp.int32, (1, LANES), 1) < RET_HEAD_DIM // 2, -1.0, 1.0)
    cos2 = cos_a * cos_b - sin_a * sin_b
    sin2 = sign * (sin_a * cos_b + cos_a * sin_b)

    def rope(blk):
        return blk * cos2 + pltpu.roll(blk, RET_HEAD_DIM // 2, 1) * sin2

    p = proj(3)
    for h in range(RET_HEADS):
        sl = slice(h * LANES, (h + 1) * LANES)
        rq_ref[:, sl] = rope(p[:, sl]).astype(BF16)
    p = proj(4)
    for h in range(RET_HEADS):
        sl = slice(h * LANES, (h + 1) * LANES)
        rkt_ref[h] = (rope(p[:, sl]) * RET_HEAD_DIM ** -0.5).T.astype(BF16)
    rv_ref[...] = proj(5).astype(BF16)
    rg_ref[...] = proj(6).astype(BF16)


def _even_in(x2, g, w, qg, kg, rope_tables, batch, seq, sb_blk):
    t, d = x2.shape
    tm = min(ROW_TILE, seq)
    tps = seq // tm
    tile_start = pl.BlockSpec((None, 1, LANES), lambda i: (i % tps, 0, 0))
    row = lambda i: (i, 0)
    pairs = SB_WIDTH // LANES
    flat = jax.ShapeDtypeStruct((t, SB_WIDTH), BF16)
    flat_spec = pl.BlockSpec((tm, SB_WIDTH), row)
    sbkt = jax.ShapeDtypeStruct((batch, pairs, seq // sb_blk, LANES, sb_blk), BF16)
    sbkt_spec = pl.BlockSpec((None, pairs, tm // sb_blk, LANES, sb_blk),
                             lambda i: (i // tps, 0, i % tps, 0, 0))
    rkt = jax.ShapeDtypeStruct((batch, RET_HEADS, RET_HEAD_DIM, seq), BF16)
    rkt_spec = pl.BlockSpec((None, RET_HEADS, RET_HEAD_DIM, tm), lambda i: (i // tps, 0, 0, i % tps))
    return pl.pallas_call(
        _even_in_kernel,
        grid=(t // tm,),
        in_specs=[
            pl.BlockSpec((tm, d), row),
            _const_spec((1, d)),
            _resident_spec(w.shape),
            _const_spec((1, LANES)),
            _const_spec((1, LANES)),
            tile_start,
            tile_start,
            _const_spec((tm, LANES)),
            _const_spec((tm, LANES)),
        ],
        out_specs=[flat_spec, sbkt_spec, flat_spec, flat_spec, rkt_spec, flat_spec, flat_spec],
        out_shape=[flat, sbkt, flat, flat, rkt, flat, flat],
        compiler_params=_params("parallel"),
        name="even_in",
    )(x2, g, w, qg, kg, *rope_tables)


def _softplus2(z):
    return jnp.where(z > SOFTPLUS2_LINEAR_ABOVE, z, jnp.log2(1.0 + jnp.exp2(z)))


def _sb_kernel(q_ref, kt_ref, v_ref, tri_ref, qg_ref, kg_ref, o_ref, acc_ref, carry_ref):
    step = pl.program_id(2)
    blk = kt_ref.shape[-1]
    per_step = q_ref.shape[0] // blk
    z_bound = (SB_NORM_SLACK * SB_HEAD_DIM * SB_HEAD_DIM ** -0.5 * LOG2E
               * jnp.max(jnp.abs(qg_ref[...])) * jnp.max(jnp.abs(kg_ref[...])))
    dead_above = z_bound + EXP2_UNDERFLOW
    lane = lax.broadcasted_iota(jnp.int32, (blk, LANES), 1)
    lo = lane < SB_HEAD_DIM
    row_id = lax.broadcasted_iota(jnp.int32, (blk, blk), 0)
    col_id = lax.broadcasted_iota(jnp.int32, (blk, blk), 1)
    below_diag = col_id < row_id
    tri = tri_ref[...]
    heads = range(2)

    def split_heads(q):
        zero = jnp.zeros_like(q)
        return (jnp.where(lo, q, zero), jnp.where(lo, zero, q))

    def key_blocks(j_a):
        j_b = jnp.maximum(j_a - 1, 0)
        return j_a > 0, j_b

    def values(j):
        return v_ref[pl.ds(pl.multiple_of(j * blk, blk), blk), :]

    def diagonal_pairs(q_heads, j_as):
        chains = [(n, h) for n in range(len(j_as)) for h in heads]
        has_b, kt_a, kt_b, v_a, v_b = [], [], [], [], []
        for j_a in j_as:
            flag, j_b = key_blocks(j_a)
            has_b.append(flag)
            kt_a.append(kt_ref[j_a])
            kt_b.append(kt_ref[j_b])
            v_a.append(values(j_a))
            v_b.append(values(j_b))
        z_a = {(n, h): jnp.dot(q_heads[n][h], kt_a[n], preferred_element_type=F32) for n, h in chains}
        z_b = {(n, h): jnp.dot(q_heads[n][h], kt_b[n], preferred_element_type=F32) for n, h in chains}
        z_a = {c: jnp.where(below_diag, z, MASKED_SCORE) for c, z in z_a.items()}
        sp_a = {c: _softplus2(z) for c, z in z_a.items()}
        sp_b = {c: _softplus2(z) for c, z in z_b.items()}
        cs_a = {c: jnp.dot(sp.astype(BF16), tri, preferred_element_type=F32) for c, sp in sp_a.items()}
        cs_b = {c: jnp.dot(sp.astype(BF16), tri, preferred_element_type=F32) for c, sp in sp_b.items()}
        w_a, w_b, carry_out = {}, {}, {}
        for n, h in chains:
            c = (n, h)
            tot_a = jnp.sum(sp_a[c], axis=-1, keepdims=True)
            tot_b = jnp.where(has_b[n], jnp.sum(sp_b[c], axis=-1, keepdims=True), 0.0)
            before_b = jnp.where(has_b[n], tot_a, -MASKED_SCORE)
            w_a[c] = jnp.exp2(z_a[c] - cs_a[c])
            w_b[c] = jnp.exp2(z_b[c] - (cs_b[c] + before_b))
            carry_out[c] = tot_a + tot_b
        for n, h in chains:
            c = (n, h)
            acc_ref[n, h] = (jnp.dot(w_a[c].astype(BF16), v_a[n], preferred_element_type=F32)
                             + jnp.dot(w_b[c].astype(BF16), v_b[n], preferred_element_type=F32))
            carry_ref[n, h] = carry_out[c]

    first = step * per_step
    q_heads = [split_heads(q_ref[n * blk:(n + 1) * blk, :]) for n in range(per_step)]
    diagonal_pairs(q_heads, [first + n for n in range(per_step)])

    def older_pair(n, j_a):
        has_b, j_b = key_blocks(j_a)
        kt_a, kt_b = kt_ref[j_a], kt_ref[j_b]
        v_a, v_b = values(j_a), values(j_b)
        z_a = [jnp.dot(q_heads[n][h], kt_a, preferred_element_type=F32) for h in heads]
        z_b = [jnp.dot(q_heads[n][h], kt_b, preferred_element_type=F32) for h in heads]
        sp_a = [_softplus2(z) for z in z_a]
        sp_b = [_softplus2(z) for z in z_b]
        cs_a = [jnp.dot(sp.astype(BF16), tri, preferred_element_type=F32) for sp in sp_a]
        cs_b = [jnp.dot(sp.astype(BF16), tri, preferred_element_type=F32) for sp in sp_b]
        for h in heads:
            carry = carry_ref[n, h]
            after_a = carry + jnp.sum(sp_a[h], axis=-1, keepdims=True)
            tot_b = jnp.where(has_b, jnp.sum(sp_b[h], axis=-1, keepdims=True), 0.0)
            before_b = jnp.where(has_b, after_a, -MASKED_SCORE)
            w_a = jnp.exp2(z_a[h] - (cs_a[h] + carry))
            w_b = jnp.exp2(z_b[h] - (cs_b[h] + before_b))
            acc_ref[n, h] += (jnp.dot(w_a.astype(BF16), v_a, preferred_element_type=F32)
                              + jnp.dot(w_b.astype(BF16), v_b, preferred_element_type=F32))
            carry_ref[n, h] = after_a + tot_b

    def live(n):
        return jnp.minimum(jnp.min(carry_ref[n, 0]), jnp.min(carry_ref[n, 1])) <= dead_above

    for n in range(per_step):
        def body(c, n=n):
            older_pair(n, c[0])
            return c[0] - 2, live(n)

        lax.while_loop(lambda c: jnp.logical_and(c[0] >= 0, c[1]), body, (first + n - 2, live(n)))
        o_ref[n * blk:(n + 1) * blk, :] = jnp.where(lo, acc_ref[n, 0], acc_ref[n, 1]).astype(o_ref.dtype)


def _sb_attention(q, kt, v, tri, qg, kg):
    b, s, width = q.shape
    blk = kt.shape[-1]
    n_blocks = s // blk
    per_step = min(SB_QBLOCKS_PER_STEP, n_blocks)
    assert n_blocks % per_step == 0
    rows = per_step * blk
    return pl.pallas_call(
        _sb_kernel,
        grid=(b, width // LANES, n_blocks // per_step),
        in_specs=[
            pl.BlockSpec((None, rows, LANES), lambda bi, hp, i: (bi, i, hp)),
            pl.BlockSpec((None, None, n_blocks, LANES, blk), lambda bi, hp, i: (bi, hp, 0, 0, 0)),
            pl.BlockSpec((None, s, LANES), lambda bi, hp, i: (bi, 0, hp)),
            _const_spec(tri.shape),
            _const_spec((1, LANES)),
            _const_spec((1, LANES)),
        ],
        out_specs=pl.BlockSpec((None, rows, LANES), lambda bi, hp, i: (bi, i, hp)),
        out_shape=jax.ShapeDtypeStruct((b, s, width), BF16),
        scratch_shapes=[pltpu.VMEM((per_step, 2, blk, LANES), F32),
                        pltpu.VMEM((per_step, 2, blk, 1), F32)],
        compiler_params=_params("parallel", "parallel", "arbitrary"),
        name="sb_attention",
    )(q, kt, v, tri, qg, kg)


def _ret_kernel(q_ref, kt_ref, v_ref, g_ref, gain_ref, decay_ref, zeta_ref, xi_ref, gc_ref,
                o_ref, state_ref):
    @pl.when(pl.program_id(2) == 0)
    def _():
        state_ref[...] = jnp.zeros_like(state_ref)

    c = RET_CHUNK
    d = RET_HEAD_DIM
    span = decay_ref.shape[-1]
    rows = q_ref.shape[0]
    gain = gain_ref[...]
    for hh in range(state_ref.shape[0]):
        lanes = slice(hh * d, (hh + 1) * d)
        decay = decay_ref[hh]
        zeta = zeta_ref[hh]
        xi = xi_ref[hh]
        g_chunk = gc_ref[hh]
        q = [q_ref[m * span:(m + 1) * span, lanes] for m in range(rows // span)]
        kt = [kt_ref[hh, :, m * span:(m + 1) * span] for m in range(rows // span)]
        v = [v_ref[m * span:(m + 1) * span, lanes] for m in range(rows // span)]
        scores = [jnp.dot(q[m], kt[m], preferred_element_type=F32) * decay
                  for m in range(rows // span)]
        inner = [jnp.dot(scores[m].astype(BF16), v[m], preferred_element_type=F32)
                 for m in range(rows // span)]
        contrib = []
        for n in range(rows // c):
            m, sub = divmod(n * c, span)
            kz = (kt[m][:, sub:sub + c].astype(F32) * zeta).astype(BF16)
            contrib.append(jnp.dot(kz, v[m][sub:sub + c], preferred_element_type=F32))
        state = state_ref[hh]
        for n in range(rows // c):
            m, sub = divmod(n * c, span)
            cross = jnp.dot(q[m][sub:sub + c], state.astype(BF16), preferred_element_type=F32) * xi
            state = g_chunk * state + contrib[n]
            ret = _rms_rows(inner[m][sub:sub + c] + cross, gain)
            gate = _silu(g_ref[n * c:(n + 1) * c, lanes].astype(F32))
            o_ref[n * c:(n + 1) * c, lanes] = (ret * gate).astype(o_ref.dtype)
        state_ref[hh] = state


def _retention(q, kt, v, g, gain, decay, zeta, xi, gc):
    b, s, width = q.shape
    rows = min(RET_GROUP, s)
    span = decay.shape[-1]
    hps = RET_HEADS_PER_STEP
    blk = pl.BlockSpec((None, rows, hps * RET_HEAD_DIM), lambda bi, h, n: (bi, n, h))
    head = lambda shape: pl.BlockSpec((hps,) + shape, lambda bi, h, n: (h, 0, 0))
    return pl.pallas_call(
        _ret_kernel,
        grid=(b, RET_HEADS // hps, s // rows),
        in_specs=[blk,
                  pl.BlockSpec((None, hps, RET_HEAD_DIM, rows), lambda bi, h, n: (bi, h, 0, n)),
                  blk, blk, _const_spec((1, RET_HEAD_DIM)),
                  head((span, span)), head((1, RET_CHUNK)), head((RET_CHUNK, 1)), head((1, 1))],
        out_specs=blk,
        out_shape=jax.ShapeDtypeStruct((b, s, width), BF16),
        scratch_shapes=[pltpu.VMEM((hps, RET_HEAD_DIM, RET_HEAD_DIM), F32)],
        compiler_params=_params("parallel", "parallel", "arbitrary"),
        name="retention",
    )(q, kt, v, g, gain, decay, zeta, xi, gc)


def _ffn_kernel(*refs, n_mix, tiles_per_seq):
    h_ref = refs[0]
    mix_refs = refs[1:1 + n_mix]
    (wo_ref, g_ref, wup_ref, cw_ref, cb_ref, wd_ref, o_ref,
     ubuf_ref, prev_ref, gate_ref) = refs[1 + n_mix:]
    tm = h_ref.shape[0]
    d_ff = wd_ref.shape[0]
    halo = SUBLANES

    @pl.when(pl.program_id(0) % tiles_per_seq == 0)
    def _():
        prev_ref[...] = jnp.zeros_like(prev_ref)

    h = h_ref[...]
    row0 = 0
    for a_ref in mix_refs:
        width = a_ref.shape[1]
        h = h + jnp.dot(a_ref[...], wo_ref[row0:row0 + width, :], preferred_element_type=F32)
        row0 += width
    hn = _rms_rows(h, g_ref[...]).astype(BF16)
    for c0 in range(0, d_ff, FF_CHUNK):
        cols = slice(c0, c0 + FF_CHUNK)
        u = jnp.dot(hn, wup_ref[:, cols], preferred_element_type=F32)
        v = jnp.dot(hn, wup_ref[:, d_ff + c0:d_ff + c0 + FF_CHUNK], preferred_element_type=F32)
        ubuf_ref[0:halo, :] = prev_ref[:, cols]
        ubuf_ref[halo:halo + tm, :] = u
        prev_ref[:, cols] = u[tm - halo:, :]
        uc = (cb_ref[:, cols]
              + cw_ref[0:1, cols] * ubuf_ref[halo - 2:halo - 2 + tm, :]
              + cw_ref[1:2, cols] * ubuf_ref[halo - 1:halo - 1 + tm, :]
              + cw_ref[2:3, cols] * u)
        gate_ref[:, cols] = (_silu(uc) * v).astype(BF16)
    o_ref[...] = h + jnp.dot(gate_ref[...], wd_ref[...], preferred_element_type=F32)


def _ffn(h2, mix, wo, g, wup, cw, cb, wd, seq):
    t, d = h2.shape
    d_ff = wd.shape[0]
    tm = min(FFN_ROW_TILE, seq)
    row = lambda i: (i, 0)
    return pl.pallas_call(
        functools.partial(_ffn_kernel, n_mix=len(mix), tiles_per_seq=seq // tm),
        grid=(t // tm,),
        in_specs=([pl.BlockSpec((tm, d), row)]
                  + [pl.BlockSpec((tm, a.shape[1]), row) for a in mix]
                  + [_resident_spec(wo.shape), _const_spec((1, d)), _resident_spec(wup.shape),
                     _const_spec(cw.shape), _const_spec(cb.shape), _resident_spec(wd.shape)]),
        out_specs=pl.BlockSpec((tm, d), row),
        out_shape=jax.ShapeDtypeStruct((t, d), F32),
        scratch_shapes=[pltpu.VMEM((tm + SUBLANES, FF_CHUNK), F32),
                        pltpu.VMEM((SUBLANES, d_ff), F32),
                        pltpu.VMEM((tm, d_ff), BF16)],
        compiler_params=_params("arbitrary"),
        name="conv_ffn",
    )(h2, *mix, wo, g, wup, cw, cb, wd)


def _log_sigmoid(x):
    return jnp.minimum(x, 0.0) - jnp.log(1.0 + jnp.exp(-jnp.abs(x)))


def _odd_in_kernel(x_ref, g_ref, w_ref, wa_ref, walpha_ref, balpha_ref, tri_ref,
                   qt_ref, ktt_ref, kdt_ref, v_ref, r_ref, dec_ref):
    tm = x_ref.shape[0]
    span = tri_ref.shape[0]
    hn = _rms_rows(x_ref[...], g_ref[...]).astype(BF16)

    def proj(c0, width):
        return jnp.dot(hn, w_ref[:, c0:c0 + width], preferred_element_type=F32)

    ga = jnp.dot(hn, wa_ref[...], preferred_element_type=F32)
    ga_hi = ga.astype(BF16)
    ga_lo = (ga - ga_hi.astype(F32)).astype(BF16)
    lane = lax.broadcasted_iota(jnp.int32, ga.shape, 1)
    middle = jnp.logical_and(lane >= GLA_GATE_RANK, lane < 2 * GLA_GATE_RANK)
    pre = jnp.dot(jnp.where(middle, ga_lo, ga_hi), walpha_ref[...],
                  preferred_element_type=F32) + balpha_ref[...]
    la = _log_sigmoid(pre) * (LOG2E / GLA_GATE_TAU)
    la_hi = la.astype(BF16)
    la_lo = (la - la_hi.astype(F32)).astype(BF16)
    tri = tri_ref[...]
    cum = jnp.concatenate(
        [jnp.dot(tri, la_hi[r0:r0 + span], preferred_element_type=F32)
         + jnp.dot(tri, la_lo[r0:r0 + span], preferred_element_type=F32)
         for r0 in range(0, tm, span)], axis=0)
    chunks = tm // GLA_CHUNK
    last = cum.reshape(chunks, GLA_CHUNK, GLA_K_WIDTH)[:, GLA_CHUNK - 1:GLA_CHUNK, :]
    dec_ref[...] = jnp.exp2(last).reshape(chunks, GLA_K_WIDTH)
    to_end = (jnp.broadcast_to(last, (chunks, GLA_CHUNK, GLA_K_WIDTH)).reshape(tm, GLA_K_WIDTH)
              - cum)

    qt_ref[...] = (proj(0, GLA_K_WIDTH) * GLA_DK ** -0.5 * jnp.exp2(cum)).astype(BF16)
    k = proj(GLA_K_WIDTH, GLA_K_WIDTH)
    kt = k * jnp.exp2(-cum)
    kd = k * jnp.exp2(to_end)
    for h in range(GLA_HEADS):
        sl = slice(h * GLA_DK, (h + 1) * GLA_DK)
        ktt_ref[h] = kt[:, sl].T.astype(BF16)
        kdt_ref[h] = kd[:, sl].T.astype(BF16)
    v_ref[...] = proj(2 * GLA_K_WIDTH, GLA_V_WIDTH).astype(BF16)
    r_ref[...] = proj(2 * GLA_K_WIDTH + GLA_V_WIDTH, GLA_V_WIDTH).astype(BF16)


def _odd_in(x2, g, w, wa, walpha, balpha, tri, batch, seq):
    t, d = x2.shape
    tm = min(ROW_TILE, seq)
    tps = seq // tm
    row = lambda i: (i, 0)
    kt_shape = jax.ShapeDtypeStruct((batch, GLA_HEADS, GLA_DK, seq), BF16)
    kt_spec = pl.BlockSpec((None, GLA_HEADS, GLA_DK, tm), lambda i: (i // tps, 0, 0, i % tps))
    vout = jax.ShapeDtypeStruct((t, GLA_V_WIDTH), BF16)
    return pl.pallas_call(
        _odd_in_kernel,
        grid=(t // tm,),
        in_specs=[pl.BlockSpec((tm, d), row), _const_spec((1, d)), _resident_spec(w.shape),
                  _const_spec(wa.shape), _const_spec(walpha.shape), _const_spec(balpha.shape),
                  _const_spec(tri.shape)],
        out_specs=[pl.BlockSpec((tm, GLA_K_WIDTH), row), kt_spec, kt_spec,
                   pl.BlockSpec((tm, GLA_V_WIDTH), row), pl.BlockSpec((tm, GLA_V_WIDTH), row),
                   pl.BlockSpec((tm // GLA_CHUNK, GLA_K_WIDTH), row)],
        out_shape=[jax.ShapeDtypeStruct((t, GLA_K_WIDTH), BF16), kt_shape, kt_shape, vout, vout,
                   jax.ShapeDtypeStruct((t // GLA_CHUNK, GLA_K_WIDTH), F32)],
        compiler_params=_params("parallel"),
        name="odd_in",
    )(x2, g, w, wa, walpha, balpha, tri)


def _gla_kernel(qt_ref, ktt_ref, kdt_ref, v_ref, r_ref, dec_ref, gain_ref, o_ref, state_ref):
    @pl.when(pl.program_id(2) == 0)
    def _():
        state_ref[...] = jnp.zeros_like(state_ref)

    c = GLA_CHUNK
    rows = qt_ref.shape[0]
    span = min(GLA_SPAN, rows)
    ri = lax.broadcasted_iota(jnp.int32, (span, span), 0)
    ci = lax.broadcasted_iota(jnp.int32, (span, span), 1)
    causal = jnp.logical_and(ci <= ri, ci // c == ri // c)
    col_chunk = lax.broadcasted_iota(jnp.int32, (1, span), 1) // c
    gain = gain_ref[...]
    for hh in range(state_ref.shape[0]):
        klanes = slice(hh * GLA_DK, (hh + 1) * GLA_DK)
        vlanes = slice(hh * GLA_DV, (hh + 1) * GLA_DV)
        dec_t = dec_ref[:, klanes].T
        spans = range(rows // span)
        qt = [qt_ref[m * span:(m + 1) * span, klanes] for m in spans]
        ktt = [ktt_ref[hh, :, m * span:(m + 1) * span] for m in spans]
        kdt = [kdt_ref[hh, :, m * span:(m + 1) * span] for m in spans]
        v = [v_ref[m * span:(m + 1) * span, vlanes] for m in spans]
        a = [jnp.where(causal, jnp.dot(qt[m], ktt[m], preferred_element_type=F32), 0.0)
             for m in spans]
        intra = [jnp.dot(a[m].astype(BF16), v[m], preferred_element_type=F32) for m in spans]
        contrib = []
        for n in range(rows // c):
            m, sub = divmod(n * c, span)
            kd_n = jnp.where(col_chunk == sub // c, kdt[m], jnp.zeros_like(kdt[m]))
            contrib.append(jnp.dot(kd_n, v[m], preferred_element_type=F32))
        state = state_ref[hh]
        for n in range(rows // c):
            m, sub = divmod(n * c, span)
            inter = jnp.dot(qt[m][sub:sub + c], state.astype(BF16), preferred_element_type=F32)
            state = dec_t[:, n:n + 1] * state + contrib[n]
            o = _rms_rows(intra[m][sub:sub + c] + inter, gain)
            gate = _silu(r_ref[n * c:(n + 1) * c, vlanes].astype(F32))
            o_ref[n * c:(n + 1) * c, vlanes] = (o * gate).astype(o_ref.dtype)
        state_ref[hh] = state


def _gla(qt, ktt, kdt, v, r, dec, gain):
    b, s, kw = qt.shape
    vw = v.shape[-1]
    rows = min(GLA_GROUP, s)
    hps = GLA_HEADS_PER_STEP
    chunks = rows // GLA_CHUNK
    dec = dec.reshape(b, s // rows, chunks, kw)
    qblk = pl.BlockSpec((None, rows, hps * GLA_DK), lambda bi, h, n: (bi, n, h))
    tblk = pl.BlockSpec((None, hps, GLA_DK, rows), lambda bi, h, n: (bi, h, 0, n))
    vblk = pl.BlockSpec((None, rows, hps * GLA_DV), lambda bi, h, n: (bi, n, h))
    dblk = pl.BlockSpec((None, None, chunks, hps * GLA_DK), lambda bi, h, n: (bi, n, 0, h))
    return pl.pallas_call(
        _gla_kernel,
        grid=(b, GLA_HEADS // hps, s // rows),
        in_specs=[qblk, tblk, tblk, vblk, vblk, dblk, _const_spec((1, GLA_DV))],
        out_specs=vblk,
        out_shape=jax.ShapeDtypeStruct((b, s, vw), BF16),
        scratch_shapes=[pltpu.VMEM((hps, GLA_DK, GLA_DV), F32)],
        compiler_params=_params("parallel", "parallel", "arbitrary"),
        name="gla",
    )(qt, ktt, kdt, v, r, dec, gain)


def _rope_tables(seq, tile):
    half = RET_HEAD_DIM // 2
    inv = ROPE_BASE ** (-np.arange(half, dtype=np.float64) / half)

    def both(pos):
        ang = pos[:, None] * inv[None, :]
        dup = lambda t: np.concatenate([t, t], axis=-1).astype(np.float32)
        return dup(np.cos(ang)), dup(np.sin(ang))

    cos_a, sin_a = both(np.arange(0, seq, tile, dtype=np.float64))
    cos_b, sin_b = both(np.arange(tile, dtype=np.float64))
    return (jnp.asarray(cos_a[:, None, :]), jnp.asarray(sin_a[:, None, :]),
            jnp.asarray(cos_b), jnp.asarray(sin_b))


def _retention_tables(span):
    c = RET_CHUNK
    lg = np.log(1.0 - np.exp2(-5.0 - np.arange(RET_HEADS, dtype=np.float64)))[:, None]
    pos = np.arange(span, dtype=np.float64)
    diff = pos[:, None] - pos[None, :]
    same = (pos[:, None] // c) == (pos[None, :] // c)
    decay = np.where((diff >= 0) & same, np.exp(lg[:, :, None] * np.maximum(diff, 0.0)), 0.0)
    pos = np.arange(c, dtype=np.float64)
    zeta = np.exp(lg * (c - 1 - pos))[:, None, :]
    xi = np.exp(lg * (pos + 1.0))[:, :, None]
    g_chunk = np.exp(lg * c)[:, :, None]
    return tuple(jnp.asarray(t.astype(np.float32)) for t in (decay, zeta, xi, g_chunk))


def _suffix_tri(n):
    idx = np.arange(n)
    return jnp.asarray(idx[:, None] >= idx[None, :], dtype=BF16)


def _chunk_prefix_tri(rows, chunk):
    idx = np.arange(rows)
    same = (idx[:, None] // chunk) == (idx[None, :] // chunk)
    return jnp.asarray(same & (idx[None, :] <= idx[:, None]), dtype=BF16)


def _gate_operands(w_gate_in, w_alpha):
    rank = GLA_GATE_RANK
    wa = jnp.pad(jnp.tile(w_gate_in, (1, 3)), ((0, 0), (0, LANES - 3 * rank))).astype(BF16)
    w_hi = w_alpha.astype(BF16)
    w_lo = (w_alpha - w_hi.astype(F32)).astype(BF16)
    walpha = jnp.pad(jnp.concatenate([w_hi, w_hi, w_lo], axis=0), ((0, LANES - 3 * rank), (0, 0)))
    return wa, walpha


def kernel(x, mix_norm_g, even_w_in, sb_q_gain, sb_k_gain, ret_out_gain, even_w_out,
           odd_w_in, gla_w_alpha, gla_b_alpha, gla_out_gain, odd_w_out,
           ffn_norm_g, ffn_w_up, ffn_conv_w, ffn_conv_b, ffn_w_down):
    b, s, d = x.shape
    t = b * s
    depth = mix_norm_g.shape[0]
    h = x.reshape(t, d)

    rope_tables = _rope_tables(s, min(ROW_TILE, s))
    decay, zeta, xi, g_chunk = _retention_tables(min(RET_SPAN, s))
    sb_blk = min(SB_BLOCK, s)
    sb_tri = _suffix_tri(sb_blk)
    gla_tri = _chunk_prefix_tri(min(GLA_SPAN, s), GLA_CHUNK)

    for layer in range(depth):
        g_mix = mix_norm_g[layer][None, :]
        if layer % 2 == 0:
            e = layer // 2
            qg = jnp.tile(sb_q_gain[e], 2)[None, :]
            kg = jnp.tile(sb_k_gain[e], 2)[None, :]
            sbq, sbkt, sbv, rq, rkt, rv, rg = _even_in(
                h, g_mix, even_w_in[e].astype(BF16), qg, kg, rope_tables, b, s, sb_blk)
            out_a = _sb_attention(sbq.reshape(b, s, SB_WIDTH), sbkt, sbv.reshape(b, s, SB_WIDTH),
                                  sb_tri, qg, kg)
            shp = (b, s, RET_WIDTH)
            out_b = _retention(rq.reshape(shp), rkt, rv.reshape(shp), rg.reshape(shp),
                               ret_out_gain[e][None, :], decay, zeta, xi, g_chunk)
            mix = (out_a.reshape(t, SB_WIDTH), out_b.reshape(t, RET_WIDTH))
            w_out = even_w_out[e]
        else:
            o = layer // 2
            w_in = odd_w_in[o]
            n_main = 2 * GLA_K_WIDTH + 2 * GLA_V_WIDTH
            wa, walpha = _gate_operands(w_in[:, n_main:], gla_w_alpha[o])
            qt, ktt, kdt, gv, gr, dec = _odd_in(h, g_mix, w_in.astype(BF16), wa, walpha,
                                                gla_b_alpha[o][None, :], gla_tri, b, s)
            og = _gla(qt.reshape(b, s, GLA_K_WIDTH), ktt, kdt,
                      gv.reshape(b, s, GLA_V_WIDTH), gr.reshape(b, s, GLA_V_WIDTH),
                      dec.reshape(b, s // GLA_CHUNK, GLA_K_WIDTH), gla_out_gain[o][None, :])
            mix = (og.reshape(t, GLA_V_WIDTH),)
            w_out = odd_w_out[o]
        h = _ffn(h, mix, w_out.astype(BF16), ffn_norm_g[layer][None, :], ffn_w_up[layer].astype(BF16),
                 ffn_conv_w[layer], ffn_conv_b[layer][None, :], ffn_w_down[layer].astype(BF16), s)
    return h.reshape(b, s, d)
```

```python
import functools

import numpy as np

import jax
import jax.numpy as jnp
from jax import lax
from jax.experimental import pallas as pl
from jax.experimental.pallas import tpu as pltpu

F32 = jnp.float32
BF16 = jnp.bfloat16

EPS = 1e-6
LOG2E = 1.4426950408889634
ROPE_BASE = 10000.0
EXP2_UNDERFLOW = 150.0
SB_NORM_SLACK = 1.05
SOFTPLUS2_LINEAR_ABOVE = 32.0
MASKED_SCORE = -1e30

SB_HEADS = 8
SB_HEAD_DIM = 64
SB_WIDTH = SB_HEADS * SB_HEAD_DIM
RET_HEADS = 4
RET_HEAD_DIM = 128
RET_WIDTH = RET_HEADS * RET_HEAD_DIM
RET_CHUNK = 128
GLA_HEADS = 4
GLA_DK = 128
GLA_DV = 256
GLA_K_WIDTH = GLA_HEADS * GLA_DK
GLA_V_WIDTH = GLA_HEADS * GLA_DV
GLA_GATE_RANK = 16
GLA_GATE_TAU = 16.0
GLA_CHUNK = 64
CONV_WIDTH = 3

LANES = 128
SUBLANES = 8
MXU_DIM = 256
VMEM_LIMIT = 56 * 1024 * 1024

ROW_TILE = 1024
FFN_ROW_TILE = 1024
SB_BLOCK = MXU_DIM
SB_QBLOCKS_PER_STEP = 4
RET_HEADS_PER_STEP = 4
RET_GROUP = 1024
RET_SPAN = 2 * RET_CHUNK
GLA_GROUP = 512
GLA_HEADS_PER_STEP = 4
GLA_SPAN = 4 * GLA_CHUNK
FF_CHUNK = 256


def _params(*sem):
    return pltpu.CompilerParams(dimension_semantics=sem, vmem_limit_bytes=VMEM_LIMIT)


def _const_spec(shape):
    n = len(shape)
    return pl.BlockSpec(shape, lambda *_: (0,) * n)


def _layer_spec(stacked_shape, layer):
    zeros = (0,) * (len(stacked_shape) - 1)
    return pl.BlockSpec((None,) + tuple(stacked_shape[1:]), lambda *_: (layer,) + zeros,
                        pipeline_mode=pl.Buffered(1))


def _rms_rows(x, g):
    ms = jnp.mean(x * x, axis=-1, keepdims=True)
    return x * lax.rsqrt(ms + EPS) * g


def _silu(x):
    return x / (1.0 + jnp.exp2(x * -LOG2E))


def _half_head_rms(blk):
    sq = blk * blk
    lane = lax.broadcasted_iota(jnp.int32, blk.shape, 1)
    lo = lane < SB_HEAD_DIM
    s_lo = jnp.sum(jnp.where(lo, sq, 0.0), axis=-1, keepdims=True)
    s_hi = jnp.sum(jnp.where(lo, 0.0, sq), axis=-1, keepdims=True)
    ms = jnp.where(lo, s_lo, s_hi) * (1.0 / SB_HEAD_DIM)
    return blk * lax.rsqrt(ms + EPS)


def _even_in_kernel(x_ref, g_ref, w_ref, qg_ref, kg_ref, ca_ref, sa_ref, cb_ref, sb_ref,
                    sbq_ref, sbkt_ref, sbv_ref, rq_ref, rkt_ref, rv_ref, rg_ref):
    tm = x_ref.shape[0]
    hn = _rms_rows(x_ref[...], g_ref[...]).astype(BF16)

    def proj(seg):
        return jnp.dot(hn, w_ref[:, seg * SB_WIDTH:(seg + 1) * SB_WIDTH],
                       preferred_element_type=F32)

    q_scale = SB_HEAD_DIM ** -0.5 * LOG2E
    p = proj(0)
    for hp in range(SB_WIDTH // LANES):
        sl = slice(hp * LANES, (hp + 1) * LANES)
        sbq_ref[:, sl] = (_half_head_rms(p[:, sl]) * (qg_ref[...] * q_scale)).astype(BF16)
    p = proj(1)
    blk = sbkt_ref.shape[-1]
    for hp in range(SB_WIDTH // LANES):
        sl = slice(hp * LANES, (hp + 1) * LANES)
        kt = (_half_head_rms(p[:, sl]) * kg_ref[...]).T
        for c in range(tm // blk):
            sbkt_ref[hp, c] = kt[:, c * blk:(c + 1) * blk].astype(BF16)
    sbv_ref[...] = proj(2).astype(BF16)

    cos_a, sin_a, cos_b, sin_b = ca_ref[...], sa_ref[...], cb_ref[...], sb_ref[...]
    sign = jnp.where(lax.broadcasted_iota(jnp.int32, (1, LANES), 1) < RET_HEAD_DIM // 2, -1.0, 1.0)
    cos2 = cos_a * cos_b - sin_a * sin_b
    sin2 = sign * (sin_a * cos_b + cos_a * sin_b)

    def rope(blk):
        return blk * cos2 + pltpu.roll(blk, RET_HEAD_DIM // 2, 1) * sin2

    p = proj(3)
    for h in range(RET_HEADS):
        sl = slice(h * LANES, (h + 1) * LANES)
        rq_ref[:, sl] = rope(p[:, sl]).astype(BF16)
    p = proj(4)
    for h in range(RET_HEADS):
        sl = slice(h * LANES, (h + 1) * LANES)
        rkt_ref[h] = (rope(p[:, sl]) * RET_HEAD_DIM ** -0.5).T.astype(BF16)
    rv_ref[...] = proj(5).astype(BF16)
    rg_ref[...] = _silu(proj(6)).astype(BF16)


def _even_in(x2, g, w_stack, layer, qg, kg, rope_tables, batch, seq, sb_blk):
    t, d = x2.shape
    tm = min(ROW_TILE, seq)
    tps = seq // tm
    tile_start = pl.BlockSpec((None, 1, LANES), lambda i: (i % tps, 0, 0))
    row = lambda i: (i, 0)
    pairs = SB_WIDTH // LANES
    flat = jax.ShapeDtypeStruct((t, SB_WIDTH), BF16)
    flat_spec = pl.BlockSpec((tm, SB_WIDTH), row)
    sbkt = jax.ShapeDtypeStruct((batch, pairs, seq // sb_blk, LANES, sb_blk), BF16)
    sbkt_spec = pl.BlockSpec((None, pairs, tm // sb_blk, LANES, sb_blk),
                             lambda i: (i // tps, 0, i % tps, 0, 0))
    rkt = jax.ShapeDtypeStruct((batch, RET_HEADS, RET_HEAD_DIM, seq), BF16)
    rkt_spec = pl.BlockSpec((None, RET_HEADS, RET_HEAD_DIM, tm), lambda i: (i // tps, 0, 0, i % tps))
    return pl.pallas_call(
        _even_in_kernel,
        grid=(t // tm,),
        in_specs=[
            pl.BlockSpec((tm, d), row),
            _const_spec((1, d)),
            _layer_spec(w_stack.shape, layer),
            _const_spec((1, LANES)),
            _const_spec((1, LANES)),
            tile_start,
            tile_start,
            _const_spec((tm, LANES)),
            _const_spec((tm, LANES)),
        ],
        out_specs=[flat_spec, sbkt_spec, flat_spec, flat_spec, rkt_spec, flat_spec, flat_spec],
        out_shape=[flat, sbkt, flat, flat, rkt, flat, flat],
        compiler_params=_params("parallel"),
        name="even_in",
    )(x2, g, w_stack, qg, kg, *rope_tables)


def _softplus2(z):
    return jnp.where(z > SOFTPLUS2_LINEAR_ABOVE, z, jnp.log2(1.0 + jnp.exp2(z)))


def _sb_kernel(q_ref, kt_ref, v_ref, tri_ref, qg_ref, kg_ref, o_ref, acc_ref, carry_ref):
    step = pl.program_id(2)
    blk = kt_ref.shape[-1]
    per_step = q_ref.shape[0] // blk
    z_bound = (SB_NORM_SLACK * SB_HEAD_DIM * SB_HEAD_DIM ** -0.5 * LOG2E
               * jnp.max(jnp.abs(qg_ref[...])) * jnp.max(jnp.abs(kg_ref[...])))
    dead_above = z_bound + EXP2_UNDERFLOW
    lane = lax.broadcasted_iota(jnp.int32, (blk, LANES), 1)
    lo = lane < SB_HEAD_DIM
    row_id = lax.broadcasted_iota(jnp.int32, (blk, blk), 0)
    col_id = lax.broadcasted_iota(jnp.int32, (blk, blk), 1)
    below_diag = col_id < row_id
    tri = tri_ref[...]
    heads = range(2)

    def split_heads(q):
        zero = jnp.zeros_like(q)
        return (jnp.where(lo, q, zero), jnp.where(lo, zero, q))

    def key_blocks(j_a):
        j_b = jnp.maximum(j_a - 1, 0)
        return j_a > 0, j_b

    def values(j):
        return v_ref[pl.ds(pl.multiple_of(j * blk, blk), blk), :]

    def diagonal_pairs(q_heads, j_as):
        chains = [(n, h) for n in range(len(j_as)) for h in heads]
        has_b, kt_a, kt_b, v_a, v_b = [], [], [], [], []
        for j_a in j_as:
            flag, j_b = key_blocks(j_a)
            has_b.append(flag)
            kt_a.append(kt_ref[j_a])
            kt_b.append(kt_ref[j_b])
            v_a.append(values(j_a))
            v_b.append(values(j_b))
        z_a = {(n, h): jnp.dot(q_heads[n][h], kt_a[n], preferred_element_type=F32) for n, h in chains}
        z_b = {(n, h): jnp.dot(q_heads[n][h], kt_b[n], preferred_element_type=F32) for n, h in chains}
        z_a = {c: jnp.where(below_diag, z, MASKED_SCORE) for c, z in z_a.items()}
        sp_a = {c: _softplus2(z) for c, z in z_a.items()}
        sp_b = {c: _softplus2(z) for c, z in z_b.items()}
        cs_a = {c: jnp.dot(sp.astype(BF16), tri, preferred_element_type=F32) for c, sp in sp_a.items()}
        cs_b = {c: jnp.dot(sp.astype(BF16), tri, preferred_element_type=F32) for c, sp in sp_b.items()}
        w_a, w_b, carry_out = {}, {}, {}
        for n, h in chains:
            c = (n, h)
            tot_a = jnp.sum(sp_a[c], axis=-1, keepdims=True)
            tot_b = jnp.where(has_b[n], jnp.sum(sp_b[c], axis=-1, keepdims=True), 0.0)
            before_b = jnp.where(has_b[n], tot_a, -MASKED_SCORE)
            w_a[c] = jnp.exp2(z_a[c] - cs_a[c])
            w_b[c] = jnp.exp2(z_b[c] - (cs_b[c] + before_b))
            carry_out[c] = tot_a + tot_b
        for n, h in chains:
            c = (n, h)
            acc_ref[n, h] = (jnp.dot(w_a[c].astype(BF16), v_a[n], preferred_element_type=F32)
                             + jnp.dot(w_b[c].astype(BF16), v_b[n], preferred_element_type=F32))
            carry_ref[n, h] = carry_out[c]

    first = step * per_step
    q_heads = [split_heads(q_ref[n * blk:(n + 1) * blk, :]) for n in range(per_step)]
    diagonal_pairs(q_heads, [first + n for n in range(per_step)])

    def older_pair(n, j_a):
        has_b, j_b = key_blocks(j_a)
        kt_a, kt_b = kt_ref[j_a], kt_ref[j_b]
        v_a, v_b = values(j_a), values(j_b)
        z_a = [jnp.dot(q_heads[n][h], kt_a, preferred_element_type=F32) for h in heads]
        z_b = [jnp.dot(q_heads[n][h], kt_b, preferred_element_type=F32) for h in heads]
        sp_a = [_softplus2(z) for z in z_a]
        sp_b = [_softplus2(z) for z in z_b]
        cs_a = [jnp.dot(sp.astype(BF16), tri, preferred_element_type=F32) for sp in sp_a]
        cs_b = [jnp.dot(sp.astype(BF16), tri, preferred_element_type=F32) for sp in sp_b]
        for h in heads:
            carry = carry_ref[n, h]
            after_a = carry + jnp.sum(sp_a[h], axis=-1, keepdims=True)
            tot_b = jnp.where(has_b, jnp.sum(sp_b[h], axis=-1, keepdims=True), 0.0)
            before_b = jnp.where(has_b, after_a, -MASKED_SCORE)
            w_a = jnp.exp2(z_a[h] - (cs_a[h] + carry))
            w_b = jnp.exp2(z_b[h] - (cs_b[h] + before_b))
            acc_ref[n, h] += (jnp.dot(w_a.astype(BF16), v_a, preferred_element_type=F32)
                              + jnp.dot(w_b.astype(BF16), v_b, preferred_element_type=F32))
            carry_ref[n, h] = after_a + tot_b

    def live(n):
        return jnp.minimum(jnp.min(carry_ref[n, 0]), jnp.min(carry_ref[n, 1])) <= dead_above

    for n in range(per_step):
        def body(c, n=n):
            older_pair(n, c[0])
            return c[0] - 2, live(n)

        lax.while_loop(lambda c: jnp.logical_and(c[0] >= 0, c[1]), body, (first + n - 2, live(n)))
        o_ref[n * blk:(n + 1) * blk, :] = jnp.where(lo, acc_ref[n, 0], acc_ref[n, 1]).astype(o_ref.dtype)


def _sb_attention(q, kt, v, tri, qg, kg):
    b, s, width = q.shape
    blk = kt.shape[-1]
    n_blocks = s // blk
    per_step = min(SB_QBLOCKS_PER_STEP, n_blocks)
    assert n_blocks % per_step == 0
    rows = per_step * blk
    return pl.pallas_call(
        _sb_kernel,
        grid=(b, width // LANES, n_blocks // per_step),
        in_specs=[
            pl.BlockSpec((None, rows, LANES), lambda bi, hp, i: (bi, i, hp)),
            pl.BlockSpec((None, None, n_blocks, LANES, blk), lambda bi, hp, i: (bi, hp, 0, 0, 0)),
            pl.BlockSpec((None, s, LANES), lambda bi, hp, i: (bi, 0, hp)),
            _const_spec(tri.shape),
            _const_spec((1, LANES)),
            _const_spec((1, LANES)),
        ],
        out_specs=pl.BlockSpec((None, rows, LANES), lambda bi, hp, i: (bi, i, hp)),
        out_shape=jax.ShapeDtypeStruct((b, s, width), BF16),
        scratch_shapes=[pltpu.VMEM((per_step, 2, blk, LANES), F32),
                        pltpu.VMEM((per_step, 2, blk, 1), F32)],
        compiler_params=_params("parallel", "parallel", "arbitrary"),
        name="sb_attention",
    )(q, kt, v, tri, qg, kg)


def _ret_kernel(q_ref, kt_ref, v_ref, g_ref, gain_ref, decay_ref, zeta_ref, xi_ref, gc_ref,
                o_ref, state_ref):
    @pl.when(pl.program_id(2) == 0)
    def _():
        state_ref[...] = jnp.zeros_like(state_ref)

    c = RET_CHUNK
    d = RET_HEAD_DIM
    span = decay_ref.shape[-1]
    rows = q_ref.shape[0]
    gain = gain_ref[...]
    for hh in range(state_ref.shape[0]):
        lanes = slice(hh * d, (hh + 1) * d)
        decay = decay_ref[hh]
        zeta = zeta_ref[hh]
        xi = xi_ref[hh]
        g_chunk = gc_ref[hh]
        q = [q_ref[m * span:(m + 1) * span, lanes] for m in range(rows // span)]
        kt = [kt_ref[hh, :, m * span:(m + 1) * span] for m in range(rows // span)]
        v = [v_ref[m * span:(m + 1) * span, lanes] for m in range(rows // span)]
        scores = [jnp.dot(q[m], kt[m], preferred_element_type=F32) * decay
                  for m in range(rows // span)]
        inner = [jnp.dot(scores[m].astype(BF16), v[m], preferred_element_type=F32)
                 for m in range(rows // span)]
        contrib = []
        for n in range(rows // c):
            m, sub = divmod(n * c, span)
            kz = (kt[m][:, sub:sub + c].astype(F32) * zeta).astype(BF16)
            contrib.append(jnp.dot(kz, v[m][sub:sub + c], preferred_element_type=F32))
        state = state_ref[hh]
        for n in range(rows // c):
            m, sub = divmod(n * c, span)
            cross = jnp.dot(q[m][sub:sub + c], state.astype(BF16), preferred_element_type=F32) * xi
            state = g_chunk * state + contrib[n]
            ret = _rms_rows(inner[m][sub:sub + c] + cross, gain)
            gate = g_ref[n * c:(n + 1) * c, lanes].astype(F32)
            o_ref[n * c:(n + 1) * c, lanes] = (ret * gate).astype(o_ref.dtype)
        state_ref[hh] = state


def _retention(q, kt, v, g, gain, decay, zeta, xi, gc):
    b, s, width = q.shape
    rows = min(RET_GROUP, s)
    span = decay.shape[-1]
    hps = RET_HEADS_PER_STEP
    blk = pl.BlockSpec((None, rows, hps * RET_HEAD_DIM), lambda bi, h, n: (bi, n, h))
    head = lambda shape: pl.BlockSpec((hps,) + shape, lambda bi, h, n: (h, 0, 0))
    return pl.pallas_call(
        _ret_kernel,
        grid=(b, RET_HEADS // hps, s // rows),
        in_specs=[blk,
                  pl.BlockSpec((None, hps, RET_HEAD_DIM, rows), lambda bi, h, n: (bi, h, 0, n)),
                  blk, blk, _const_spec((1, RET_HEAD_DIM)),
                  head((span, span)), head((1, RET_CHUNK)), head((RET_CHUNK, 1)), head((1, 1))],
        out_specs=blk,
        out_shape=jax.ShapeDtypeStruct((b, s, width), BF16),
        scratch_shapes=[pltpu.VMEM((hps, RET_HEAD_DIM, RET_HEAD_DIM), F32)],
        compiler_params=_params("parallel", "parallel", "arbitrary"),
        name="retention",
    )(q, kt, v, g, gain, decay, zeta, xi, gc)


def _ffn_kernel(*refs, n_mix, tiles_per_seq):
    h_ref = refs[0]
    mix_refs = refs[1:1 + n_mix]
    (wo_ref, g_ref, wup_ref, cw_ref, cb_ref, wd_ref, o_ref,
     ubuf_ref, prev_ref, gate_ref) = refs[1 + n_mix:]
    tm = h_ref.shape[0]
    d_ff = wd_ref.shape[0]
    halo = SUBLANES

    @pl.when(pl.program_id(0) % tiles_per_seq == 0)
    def _():
        prev_ref[...] = jnp.zeros_like(prev_ref)

    h = h_ref[...]
    row0 = 0
    for a_ref in mix_refs:
        width = a_ref.shape[1]
        h = h + jnp.dot(a_ref[...], wo_ref[row0:row0 + width, :], preferred_element_type=F32)
        row0 += width
    hn = _rms_rows(h, g_ref[...]).astype(BF16)
    for c0 in range(0, d_ff, FF_CHUNK):
        cols = slice(c0, c0 + FF_CHUNK)
        u = jnp.dot(hn, wup_ref[:, cols], preferred_element_type=F32)
        v = jnp.dot(hn, wup_ref[:, d_ff + c0:d_ff + c0 + FF_CHUNK], preferred_element_type=F32)
        ubuf_ref[0:halo, :] = prev_ref[:, cols]
        ubuf_ref[halo:halo + tm, :] = u
        prev_ref[:, cols] = u[tm - halo:, :]
        uc = (cb_ref[:, cols]
              + cw_ref[0:1, cols] * ubuf_ref[halo - 2:halo - 2 + tm, :]
              + cw_ref[1:2, cols] * ubuf_ref[halo - 1:halo - 1 + tm, :]
              + cw_ref[2:3, cols] * u)
        gate_ref[:, cols] = (_silu(uc) * v).astype(BF16)
    o_ref[...] = h + jnp.dot(gate_ref[...], wd_ref[...], preferred_element_type=F32)


def _ffn(h2, mix, wo_stack, wo_layer, g, wup_stack, cw, cb, wd_stack, layer, seq):
    t, d = h2.shape
    d_ff = wd_stack.shape[1]
    tm = min(FFN_ROW_TILE, seq)
    row = lambda i: (i, 0)
    return pl.pallas_call(
        functools.partial(_ffn_kernel, n_mix=len(mix), tiles_per_seq=seq // tm),
        grid=(t // tm,),
        in_specs=([pl.BlockSpec((tm, d), row)]
                  + [pl.BlockSpec((tm, a.shape[1]), row) for a in mix]
                  + [_layer_spec(wo_stack.shape, wo_layer), _const_spec((1, d)),
                     _layer_spec(wup_stack.shape, layer), _const_spec(cw.shape),
                     _const_spec(cb.shape), _layer_spec(wd_stack.shape, layer)]),
        out_specs=pl.BlockSpec((tm, d), row),
        out_shape=jax.ShapeDtypeStruct((t, d), F32),
        scratch_shapes=[pltpu.VMEM((tm + SUBLANES, FF_CHUNK), F32),
                        pltpu.VMEM((SUBLANES, d_ff), F32),
                        pltpu.VMEM((tm, d_ff), BF16)],
        compiler_params=_params("arbitrary"),
        name="conv_ffn",
    )(h2, *mix, wo_stack, g, wup_stack, cw, cb, wd_stack)


def _log_sigmoid(x):
    return jnp.minimum(x, 0.0) - jnp.log(1.0 + jnp.exp(-jnp.abs(x)))


def _odd_in_kernel(x_ref, g_ref, w_ref, wa_ref, walpha_ref, balpha_ref, tri_ref,
                   qt_ref, ktt_ref, kdt_ref, v_ref, r_ref, dec_ref):
    tm = x_ref.shape[0]
    span = tri_ref.shape[0]
    hn = _rms_rows(x_ref[...], g_ref[...]).astype(BF16)

    def proj(c0, width):
        return jnp.dot(hn, w_ref[:, c0:c0 + width], preferred_element_type=F32)

    ga = jnp.dot(hn, wa_ref[...], preferred_element_type=F32)
    ga_hi = ga.astype(BF16)
    ga_lo = (ga - ga_hi.astype(F32)).astype(BF16)
    lane = lax.broadcasted_iota(jnp.int32, ga.shape, 1)
    middle = jnp.logical_and(lane >= GLA_GATE_RANK, lane < 2 * GLA_GATE_RANK)
    pre = jnp.dot(jnp.where(middle, ga_lo, ga_hi), walpha_ref[...],
                  preferred_element_type=F32) + balpha_ref[...]
    la = _log_sigmoid(pre) * (LOG2E / GLA_GATE_TAU)
    la_hi = la.astype(BF16)
    la_lo = (la - la_hi.astype(F32)).astype(BF16)
    tri = tri_ref[...]
    cum = jnp.concatenate(
        [jnp.dot(tri, la_hi[r0:r0 + span], preferred_element_type=F32)
         + jnp.dot(tri, la_lo[r0:r0 + span], preferred_element_type=F32)
         for r0 in range(0, tm, span)], axis=0)
    chunks = tm // GLA_CHUNK
    last = cum.reshape(chunks, GLA_CHUNK, GLA_K_WIDTH)[:, GLA_CHUNK - 1:GLA_CHUNK, :]
    dec_ref[...] = jnp.exp2(last).reshape(chunks, GLA_K_WIDTH)
    to_end = (jnp.broadcast_to(last, (chunks, GLA_CHUNK, GLA_K_WIDTH)).reshape(tm, GLA_K_WIDTH)
              - cum)

    qt_ref[...] = (proj(0, GLA_K_WIDTH) * GLA_DK ** -0.5 * jnp.exp2(cum)).astype(BF16)
    k = proj(GLA_K_WIDTH, GLA_K_WIDTH)
    kt = k * jnp.exp2(-cum)
    kd = k * jnp.exp2(to_end)
    for h in range(GLA_HEADS):
        sl = slice(h * GLA_DK, (h + 1) * GLA_DK)
        ktt_ref[h] = kt[:, sl].T.astype(BF16)
        kdt_ref[h] = kd[:, sl].T.astype(BF16)
    v_ref[...] = proj(2 * GLA_K_WIDTH, GLA_V_WIDTH).astype(BF16)
    r_ref[...] = _silu(proj(2 * GLA_K_WIDTH + GLA_V_WIDTH, GLA_V_WIDTH)).astype(BF16)


def _odd_in(x2, g, w_stack, layer, wa, walpha, balpha, tri, batch, seq):
    t, d = x2.shape
    tm = min(ROW_TILE, seq)
    tps = seq // tm
    row = lambda i: (i, 0)
    kt_shape = jax.ShapeDtypeStruct((batch, GLA_HEADS, GLA_DK, seq), BF16)
    kt_spec = pl.BlockSpec((None, GLA_HEADS, GLA_DK, tm), lambda i: (i // tps, 0, 0, i % tps))
    vout = jax.ShapeDtypeStruct((t, GLA_V_WIDTH), BF16)
    return pl.pallas_call(
        _odd_in_kernel,
        grid=(t // tm,),
        in_specs=[pl.BlockSpec((tm, d), row), _const_spec((1, d)), _layer_spec(w_stack.shape, layer),
                  _const_spec(wa.shape), _const_spec(walpha.shape), _const_spec(balpha.shape),
                  _const_spec(tri.shape)],
        out_specs=[pl.BlockSpec((tm, GLA_K_WIDTH), row), kt_spec, kt_spec,
                   pl.BlockSpec((tm, GLA_V_WIDTH), row), pl.BlockSpec((tm, GLA_V_WIDTH), row),
                   pl.BlockSpec((tm // GLA_CHUNK, GLA_K_WIDTH), row)],
        out_shape=[jax.ShapeDtypeStruct((t, GLA_K_WIDTH), BF16), kt_shape, kt_shape, vout, vout,
                   jax.ShapeDtypeStruct((t // GLA_CHUNK, GLA_K_WIDTH), F32)],
        compiler_params=_params("parallel"),
        name="odd_in",
    )(x2, g, w_stack, wa, walpha, balpha, tri)


def _gla_kernel(qt_ref, ktt_ref, kdt_ref, v_ref, r_ref, dec_ref, gain_ref, o_ref, state_ref):
    @pl.when(pl.program_id(2) == 0)
    def _():
        state_ref[...] = jnp.zeros_like(state_ref)

    c = GLA_CHUNK
    rows = qt_ref.shape[0]
    span = min(GLA_SPAN, rows)
    ri = lax.broadcasted_iota(jnp.int32, (span, span), 0)
    ci = lax.broadcasted_iota(jnp.int32, (span, span), 1)
    causal = jnp.logical_and(ci <= ri, ci // c == ri // c)
    col_chunk = lax.broadcasted_iota(jnp.int32, (1, span), 1) // c
    gain = gain_ref[...]
    for hh in range(state_ref.shape[0]):
        klanes = slice(hh * GLA_DK, (hh + 1) * GLA_DK)
        vlanes = slice(hh * GLA_DV, (hh + 1) * GLA_DV)
        dec_t = dec_ref[:, klanes].T
        spans = range(rows // span)
        qt = [qt_ref[m * span:(m + 1) * span, klanes] for m in spans]
        ktt = [ktt_ref[hh, :, m * span:(m + 1) * span] for m in spans]
        kdt = [kdt_ref[hh, :, m * span:(m + 1) * span] for m in spans]
        v = [v_ref[m * span:(m + 1) * span, vlanes] for m in spans]
        a = [jnp.where(causal, jnp.dot(qt[m], ktt[m], preferred_element_type=F32), 0.0)
             for m in spans]
        intra = [jnp.dot(a[m].astype(BF16), v[m], preferred_element_type=F32) for m in spans]
        contrib = []
        for n in range(rows // c):
            m, sub = divmod(n * c, span)
            kd_n = jnp.where(col_chunk == sub // c, kdt[m], jnp.zeros_like(kdt[m]))
            contrib.append(jnp.dot(kd_n, v[m], preferred_element_type=F32))
        state = state_ref[hh]
        for n in range(rows // c):
            m, sub = divmod(n * c, span)
            inter = jnp.dot(qt[m][sub:sub + c], state.astype(BF16), preferred_element_type=F32)
            state = dec_t[:, n:n + 1] * state + contrib[n]
            o = _rms_rows(intra[m][sub:sub + c] + inter, gain)
            gate = r_ref[n * c:(n + 1) * c, vlanes].astype(F32)
            o_ref[n * c:(n + 1) * c, vlanes] = (o * gate).astype(o_ref.dtype)
        state_ref[hh] = state


def _gla(qt, ktt, kdt, v, r, dec, gain):
    b, s, kw = qt.shape
    vw = v.shape[-1]
    rows = min(GLA_GROUP, s)
    hps = GLA_HEADS_PER_STEP
    chunks = rows // GLA_CHUNK
    dec = dec.reshape(b, s // rows, chunks, kw)
    qblk = pl.BlockSpec((None, rows, hps * GLA_DK), lambda bi, h, n: (bi, n, h))
    tblk = pl.BlockSpec((None, hps, GLA_DK, rows), lambda bi, h, n: (bi, h, 0, n))
    vblk = pl.BlockSpec((None, rows, hps * GLA_DV), lambda bi, h, n: (bi, n, h))
    dblk = pl.BlockSpec((None, None, chunks, hps * GLA_DK), lambda bi, h, n: (bi, n, 0, h))
    return pl.pallas_call(
        _gla_kernel,
        grid=(b, GLA_HEADS // hps, s // rows),
        in_specs=[qblk, tblk, tblk, vblk, vblk, dblk, _const_spec((1, GLA_DV))],
        out_specs=vblk,
        out_shape=jax.ShapeDtypeStruct((b, s, vw), BF16),
        scratch_shapes=[pltpu.VMEM((hps, GLA_DK, GLA_DV), F32)],
        compiler_params=_params("parallel", "parallel", "arbitrary"),
        name="gla",
    )(qt, ktt, kdt, v, r, dec, gain)


def _rope_tables(seq, tile):
    half = RET_HEAD_DIM // 2
    inv = ROPE_BASE ** (-np.arange(half, dtype=np.float64) / half)

    def both(pos):
        ang = pos[:, None] * inv[None, :]
        dup = lambda t: np.concatenate([t, t], axis=-1).astype(np.float32)
        return dup(np.cos(ang)), dup(np.sin(ang))

    cos_a, sin_a = both(np.arange(0, seq, tile, dtype=np.float64))
    cos_b, sin_b = both(np.arange(tile, dtype=np.float64))
    return (jnp.asarray(cos_a[:, None, :]), jnp.asarray(sin_a[:, None, :]),
            jnp.asarray(cos_b), jnp.asarray(sin_b))


def _retention_tables(span):
    c = RET_CHUNK
    lg = np.log(1.0 - np.exp2(-5.0 - np.arange(RET_HEADS, dtype=np.float64)))[:, None]
    pos = np.arange(span, dtype=np.float64)
    diff = pos[:, None] - pos[None, :]
    same = (pos[:, None] // c) == (pos[None, :] // c)
    decay = np.where((diff >= 0) & same, np.exp(lg[:, :, None] * np.maximum(diff, 0.0)), 0.0)
    pos = np.arange(c, dtype=np.float64)
    zeta = np.exp(lg * (c - 1 - pos))[:, None, :]
    xi = np.exp(lg * (pos + 1.0))[:, :, None]
    g_chunk = np.exp(lg * c)[:, :, None]
    return tuple(jnp.asarray(t.astype(np.float32)) for t in (decay, zeta, xi, g_chunk))


def _suffix_tri(n):
    idx = np.arange(n)
    return jnp.asarray(idx[:, None] >= idx[None, :], dtype=BF16)


def _chunk_prefix_tri(rows, chunk):
    idx = np.arange(rows)
    same = (idx[:, None] // chunk) == (idx[None, :] // chunk)
    return jnp.asarray(same & (idx[None, :] <= idx[:, None]), dtype=BF16)


def _gate_operands(w_gate_in, w_alpha):
    rank = GLA_GATE_RANK
    wa = jnp.pad(jnp.tile(w_gate_in, (1, 3)), ((0, 0), (0, LANES - 3 * rank))).astype(BF16)
    w_hi = w_alpha.astype(BF16)
    w_lo = (w_alpha - w_hi.astype(F32)).astype(BF16)
    walpha = jnp.pad(jnp.concatenate([w_hi, w_hi, w_lo], axis=0), ((0, LANES - 3 * rank), (0, 0)))
    return wa, walpha


def kernel(x, mix_norm_g, even_w_in, sb_q_gain, sb_k_gain, ret_out_gain, even_w_out,
           odd_w_in, gla_w_alpha, gla_b_alpha, gla_out_gain, odd_w_out,
           ffn_norm_g, ffn_w_up, ffn_conv_w, ffn_conv_b, ffn_w_down):
    b, s, d = x.shape
    t = b * s
    depth = mix_norm_g.shape[0]
    h = x.reshape(t, d)

    rope_tables = _rope_tables(s, min(ROW_TILE, s))
    decay, zeta, xi, g_chunk = _retention_tables(min(RET_SPAN, s))
    sb_blk = min(SB_BLOCK, s)
    sb_tri = _suffix_tri(sb_blk)
    gla_tri = _chunk_prefix_tri(min(GLA_SPAN, s), GLA_CHUNK)

    even_w_in, even_w_out, odd_w_in_b, odd_w_out, ffn_w_up, ffn_w_down = (
        w.astype(BF16) for w in (even_w_in, even_w_out, odd_w_in, odd_w_out, ffn_w_up, ffn_w_down))

    for layer in range(depth):
        g_mix = mix_norm_g[layer][None, :]
        if layer % 2 == 0:
            e = layer // 2
            qg = jnp.tile(sb_q_gain[e], 2)[None, :]
            kg = jnp.tile(sb_k_gain[e], 2)[None, :]
            sbq, sbkt, sbv, rq, rkt, rv, rg = _even_in(
                h, g_mix, even_w_in, e, qg, kg, rope_tables, b, s, sb_blk)
            out_a = _sb_attention(sbq.reshape(b, s, SB_WIDTH), sbkt, sbv.reshape(b, s, SB_WIDTH),
                                  sb_tri, qg, kg)
            shp = (b, s, RET_WIDTH)
            out_b = _retention(rq.reshape(shp), rkt, rv.reshape(shp), rg.reshape(shp),
                               ret_out_gain[e][None, :], decay, zeta, xi, g_chunk)
            mix = (out_a.reshape(t, SB_WIDTH), out_b.reshape(t, RET_WIDTH))
            w_out, w_out_layer = even_w_out, e
        else:
            o = layer // 2
            n_main = 2 * GLA_K_WIDTH + 2 * GLA_V_WIDTH
            wa, walpha = _gate_operands(odd_w_in[o, :, n_main:], gla_w_alpha[o])
            qt, ktt, kdt, gv, gr, dec = _odd_in(h, g_mix, odd_w_in_b, o, wa, walpha,
                                                gla_b_alpha[o][None, :], gla_tri, b, s)
            og = _gla(qt.reshape(b, s, GLA_K_WIDTH), ktt, kdt,
                      gv.reshape(b, s, GLA_V_WIDTH), gr.reshape(b, s, GLA_V_WIDTH),
                      dec.reshape(b, s // GLA_CHUNK, GLA_K_WIDTH), gla_out_gain[o][None, :])
            mix = (og.reshape(t, GLA_V_WIDTH),)
            w_out, w_out_layer = odd_w_out, o
        h = _ffn(h, mix, w_out, w_out_layer, ffn_norm_g[layer][None, :], ffn_w_up,
                 ffn_conv_w[layer], ffn_conv_b[layer][None, :], ffn_w_down, layer, s)
    return h.reshape(b, s, d)
```

```python
import functools

import numpy as np

import jax
import jax.numpy as jnp
from jax import lax
from jax.experimental import pallas as pl
from jax.experimental.pallas import tpu as pltpu

F32 = jnp.float32
BF16 = jnp.bfloat16

EPS = 1e-6
LOG2E = 1.4426950408889634
ROPE_BASE = 10000.0
EXP2_UNDERFLOW = 150.0
SB_NORM_SLACK = 1.05
SOFTPLUS2_LINEAR_ABOVE = 32.0
MASKED_SCORE = -1e30

SB_HEADS = 8
SB_HEAD_DIM = 64
SB_WIDTH = SB_HEADS * SB_HEAD_DIM
RET_HEADS = 4
RET_HEAD_DIM = 128
RET_WIDTH = RET_HEADS * RET_HEAD_DIM
RET_CHUNK = 128
GLA_HEADS = 4
GLA_DK = 128
GLA_DV = 256
GLA_K_WIDTH = GLA_HEADS * GLA_DK
GLA_V_WIDTH = GLA_HEADS * GLA_DV
GLA_GATE_RANK = 16
GLA_GATE_TAU = 16.0
GLA_CHUNK = 64
CONV_WIDTH = 3

LANES = 128
SUBLANES = 8
MXU_DIM = 256
VMEM_LIMIT = 56 * 1024 * 1024

ROW_TILE = 1024
FFN_ROW_TILE = 1024
SB_BLOCK = MXU_DIM
SB_QBLOCKS_PER_STEP = 4
RET_HEADS_PER_STEP = 4
RET_GROUP = 1024
RET_SPAN = 2 * RET_CHUNK
GLA_GROUP = 512
GLA_HEADS_PER_STEP = 4
GLA_SPAN = 4 * GLA_CHUNK
FF_CHUNK = 256


def _params(*sem):
    return pltpu.CompilerParams(dimension_semantics=sem, vmem_limit_bytes=VMEM_LIMIT)


def _const_spec(shape):
    n = len(shape)
    return pl.BlockSpec(shape, lambda *_: (0,) * n)


def _layer_spec(stacked_shape, layer):
    zeros = (0,) * (len(stacked_shape) - 1)
    return pl.BlockSpec((None,) + tuple(stacked_shape[1:]), lambda *_: (layer,) + zeros,
                        pipeline_mode=pl.Buffered(1))


def _rms_rows(x, g):
    ms = jnp.mean(x * x, axis=-1, keepdims=True)
    return x * lax.rsqrt(ms + EPS) * g


def _silu(x):
    return x / (1.0 + jnp.exp2(x * -LOG2E))


def _half_head_rms(blk):
    sq = blk * blk
    lane = lax.broadcasted_iota(jnp.int32, blk.shape, 1)
    lo = lane < SB_HEAD_DIM
    s_lo = jnp.sum(jnp.where(lo, sq, 0.0), axis=-1, keepdims=True)
    s_hi = jnp.sum(jnp.where(lo, 0.0, sq), axis=-1, keepdims=True)
    ms = jnp.where(lo, s_lo, s_hi) * (1.0 / SB_HEAD_DIM)
    return blk * lax.rsqrt(ms + EPS)


def _even_in_kernel(x_ref, g_ref, w_ref, qg_ref, kg_ref, ca_ref, sa_ref, cb_ref, sb_ref,
                    sbq_ref, sbkt_ref, sbv_ref, rq_ref, rkt_ref, rv_ref, rg_ref):
    tm = x_ref.shape[0]
    hn = _rms_rows(x_ref[...], g_ref[...]).astype(BF16)

    def proj(seg):
        return jnp.dot(hn, w_ref[:, seg * SB_WIDTH:(seg + 1) * SB_WIDTH],
                       preferred_element_type=F32)

    q_scale = SB_HEAD_DIM ** -0.5 * LOG2E
    p = proj(0)
    for hp in range(SB_WIDTH // LANES):
        sl = slice(hp * LANES, (hp + 1) * LANES)
        sbq_ref[:, sl] = (_half_head_rms(p[:, sl]) * (qg_ref[...] * q_scale)).astype(BF16)
    p = proj(1)
    blk = sbkt_ref.shape[-1]
    for hp in range(SB_WIDTH // LANES):
        sl = slice(hp * LANES, (hp + 1) * LANES)
        kt = (_half_head_rms(p[:, sl]) * kg_ref[...]).T
        for c in range(tm // blk):
            sbkt_ref[hp, c] = kt[:, c * blk:(c + 1) * blk].astype(BF16)
    sbv_ref[...] = proj(2).astype(BF16)

    cos_a, sin_a, cos_b, sin_b = ca_ref[...], sa_ref[...], cb_ref[...], sb_ref[...]
    sign = jnp.where(lax.broadcasted_iota(jnp.int32, (1, LANES), 1) < RET_HEAD_DIM // 2, -1.0, 1.0)
    cos2 = cos_a * cos_b - sin_a * sin_b
    sin2 = sign * (sin_a * cos_b + cos_a * sin_b)

    def rope(blk):
        return blk * cos2 + pltpu.roll(blk, RET_HEAD_DIM // 2, 1) * sin2

    p = proj(3)
    for h in range(RET_HEADS):
        sl = slice(h * LANES, (h + 1) * LANES)
        rq_ref[:, sl] = rope(p[:, sl]).astype(BF16)
    p = proj(4)
    for h in range(RET_HEADS):
        sl = slice(h * LANES, (h + 1) * LANES)
        rkt_ref[h] = (rope(p[:, sl]) * RET_HEAD_DIM ** -0.5).T.astype(BF16)
    rv_ref[...] = proj(5).astype(BF16)
    rg_ref[...] = _silu(proj(6)).astype(BF16)


def _even_in(x2, g, w_stack, layer, qg, kg, rope_tables, batch, seq, sb_blk):
    t, d = x2.shape
    tm = min(ROW_TILE, seq)
    tps = seq // tm
    tile_start = pl.BlockSpec((None, 1, LANES), lambda i: (i % tps, 0, 0))
    row = lambda i: (i, 0)
    pairs = SB_WIDTH // LANES
    flat = jax.ShapeDtypeStruct((t, SB_WIDTH), BF16)
    flat_spec = pl.BlockSpec((tm, SB_WIDTH), row)
    sbkt = jax.ShapeDtypeStruct((batch, pairs, seq // sb_blk, LANES, sb_blk), BF16)
    sbkt_spec = pl.BlockSpec((None, pairs, tm // sb_blk, LANES, sb_blk),
                             lambda i: (i // tps, 0, i % tps, 0, 0))
    rkt = jax.ShapeDtypeStruct((batch, RET_HEADS, RET_HEAD_DIM, seq), BF16)
    rkt_spec = pl.BlockSpec((None, RET_HEADS, RET_HEAD_DIM, tm), lambda i: (i // tps, 0, 0, i % tps))
    return pl.pallas_call(
        _even_in_kernel,
        grid=(t // tm,),
        in_specs=[
            pl.BlockSpec((tm, d), row),
            _const_spec((1, d)),
            _layer_spec(w_stack.shape, layer),
            _const_spec((1, LANES)),
            _const_spec((1, LANES)),
            tile_start,
            tile_start,
            _const_spec((tm, LANES)),
            _const_spec((tm, LANES)),
        ],
        out_specs=[flat_spec, sbkt_spec, flat_spec, flat_spec, rkt_spec, flat_spec, flat_spec],
        out_shape=[flat, sbkt, flat, flat, rkt, flat, flat],
        compiler_params=_params("parallel"),
        name="even_in",
    )(x2, g, w_stack, qg, kg, *rope_tables)


def _softplus2(z):
    return jnp.where(z > SOFTPLUS2_LINEAR_ABOVE, z, jnp.log2(1.0 + jnp.exp2(z)))


def _sb_kernel(q_ref, kt_ref, v_ref, tri_ref, qg_ref, kg_ref, o_ref, acc_ref, carry_ref):
    step = pl.program_id(2)
    blk = kt_ref.shape[-1]
    per_step = q_ref.shape[0] // blk
    z_bound = (SB_NORM_SLACK * SB_HEAD_DIM * SB_HEAD_DIM ** -0.5 * LOG2E
               * jnp.max(jnp.abs(qg_ref[...])) * jnp.max(jnp.abs(kg_ref[...])))
    dead_above = z_bound + EXP2_UNDERFLOW
    lane = lax.broadcasted_iota(jnp.int32, (blk, LANES), 1)
    lo = lane < SB_HEAD_DIM
    row_id = lax.broadcasted_iota(jnp.int32, (blk, blk), 0)
    col_id = lax.broadcasted_iota(jnp.int32, (blk, blk), 1)
    below_diag = col_id < row_id
    tri = tri_ref[...]
    heads = range(2)

    def split_heads(q):
        zero = jnp.zeros_like(q)
        return (jnp.where(lo, q, zero), jnp.where(lo, zero, q))

    def key_blocks(j_a):
        j_b = jnp.maximum(j_a - 1, 0)
        return j_a > 0, j_b

    def values(j):
        return v_ref[pl.ds(pl.multiple_of(j * blk, blk), blk), :]

    def diagonal_pairs(q_heads, j_as):
        chains = [(n, h) for n in range(len(j_as)) for h in heads]
        has_b, kt_a, kt_b, v_a, v_b = [], [], [], [], []
        for j_a in j_as:
            flag, j_b = key_blocks(j_a)
            has_b.append(flag)
            kt_a.append(kt_ref[j_a])
            kt_b.append(kt_ref[j_b])
            v_a.append(values(j_a))
            v_b.append(values(j_b))
        z_a = {(n, h): jnp.dot(q_heads[n][h], kt_a[n], preferred_element_type=F32) for n, h in chains}
        z_b = {(n, h): jnp.dot(q_heads[n][h], kt_b[n], preferred_element_type=F32) for n, h in chains}
        z_a = {c: jnp.where(below_diag, z, MASKED_SCORE) for c, z in z_a.items()}
        sp_a = {c: _softplus2(z) for c, z in z_a.items()}
        sp_b = {c: _softplus2(z) for c, z in z_b.items()}
        cs_a = {c: jnp.dot(sp.astype(BF16), tri, preferred_element_type=F32) for c, sp in sp_a.items()}
        cs_b = {c: jnp.dot(sp.astype(BF16), tri, preferred_element_type=F32) for c, sp in sp_b.items()}
        w_a, w_b, carry_out = {}, {}, {}
        for n, h in chains:
            c = (n, h)
            tot_a = jnp.sum(sp_a[c], axis=-1, keepdims=True)
            tot_b = jnp.where(has_b[n], jnp.sum(sp_b[c], axis=-1, keepdims=True), 0.0)
            before_b = jnp.where(has_b[n], tot_a, -MASKED_SCORE)
            w_a[c] = jnp.exp2(z_a[c] - cs_a[c])
            w_b[c] = jnp.exp2(z_b[c] - (cs_b[c] + before_b))
            carry_out[c] = tot_a + tot_b
        for n, h in chains:
            c = (n, h)
            acc_ref[n, h] = (jnp.dot(w_a[c].astype(BF16), v_a[n], preferred_element_type=F32)
                             + jnp.dot(w_b[c].astype(BF16), v_b[n], preferred_element_type=F32))
            carry_ref[n, h] = carry_out[c]

    first = step * per_step
    q_heads = [split_heads(q_ref[n * blk:(n + 1) * blk, :]) for n in range(per_step)]
    diagonal_pairs(q_heads, [first + n for n in range(per_step)])

    def older_pair(n, j_a):
        has_b, j_b = key_blocks(j_a)
        kt_a, kt_b = kt_ref[j_a], kt_ref[j_b]
        v_a, v_b = values(j_a), values(j_b)
        z_a = [jnp.dot(q_heads[n][h], kt_a, preferred_element_type=F32) for h in heads]
        z_b = [jnp.dot(q_heads[n][h], kt_b, preferred_element_type=F32) for h in heads]
        sp_a = [_softplus2(z) for z in z_a]
        sp_b = [_softplus2(z) for z in z_b]
        cs_a = [jnp.dot(sp.astype(BF16), tri, preferred_element_type=F32) for sp in sp_a]
        cs_b = [jnp.dot(sp.astype(BF16), tri, preferred_element_type=F32) for sp in sp_b]
        for h in heads:
            carry = carry_ref[n, h]
            after_a = carry + jnp.sum(sp_a[h], axis=-1, keepdims=True)
            tot_b = jnp.where(has_b, jnp.sum(sp_b[h], axis=-1, keepdims=True), 0.0)
            before_b = jnp.where(has_b, after_a, -MASKED_SCORE)
            w_a = jnp.exp2(z_a[h] - (cs_a[h] + carry))
            w_b = jnp.exp2(z_b[h] - (cs_b[h] + before_b))
            acc_ref[n, h] += (jnp.dot(w_a.astype(BF16), v_a, preferred_element_type=F32)
                              + jnp.dot(w_b.astype(BF16), v_b, preferred_element_type=F32))
            carry_ref[n, h] = after_a + tot_b

    def live(n):
        return jnp.minimum(jnp.min(carry_ref[n, 0]), jnp.min(carry_ref[n, 1])) <= dead_above

    def write_out(n):
        o_ref[n * blk:(n + 1) * blk, :] = jnp.where(lo, acc_ref[n, 0], acc_ref[n, 1]).astype(o_ref.dtype)

    least = carry_ref[0, 0]
    for n in range(per_step):
        for h in heads:
            least = jnp.minimum(least, carry_ref[n, h])
    any_live = jnp.min(least) <= dead_above

    @pl.when(jnp.logical_not(any_live))
    def _():
        for n in range(per_step):
            write_out(n)

    @pl.when(any_live)
    def _():
        for n in range(per_step):
            def body(c, n=n):
                older_pair(n, c[0])
                return c[0] - 2, live(n)

            lax.while_loop(lambda c: jnp.logical_and(c[0] >= 0, c[1]), body, (first + n - 2, live(n)))
            write_out(n)


def _sb_attention(q, kt, v, tri, qg, kg):
    b, s, width = q.shape
    blk = kt.shape[-1]
    n_blocks = s // blk
    per_step = min(SB_QBLOCKS_PER_STEP, n_blocks)
    assert n_blocks % per_step == 0
    rows = per_step * blk
    return pl.pallas_call(
        _sb_kernel,
        grid=(b, width // LANES, n_blocks // per_step),
        in_specs=[
            pl.BlockSpec((None, rows, LANES), lambda bi, hp, i: (bi, i, hp)),
            pl.BlockSpec((None, None, n_blocks, LANES, blk), lambda bi, hp, i: (bi, hp, 0, 0, 0)),
            pl.BlockSpec((None, s, LANES), lambda bi, hp, i: (bi, 0, hp)),
            _const_spec(tri.shape),
            _const_spec((1, LANES)),
            _const_spec((1, LANES)),
        ],
        out_specs=pl.BlockSpec((None, rows, LANES), lambda bi, hp, i: (bi, i, hp)),
        out_shape=jax.ShapeDtypeStruct((b, s, width), BF16),
        scratch_shapes=[pltpu.VMEM((per_step, 2, blk, LANES), F32),
                        pltpu.VMEM((per_step, 2, blk, 1), F32)],
        compiler_params=_params("parallel", "parallel", "arbitrary"),
        name="sb_attention",
    )(q, kt, v, tri, qg, kg)


def _ret_kernel(q_ref, kt_ref, v_ref, g_ref, gain_ref, decay_ref, zeta_ref, xi_ref, gc_ref,
                o_ref, state_ref):
    @pl.when(pl.program_id(2) == 0)
    def _():
        state_ref[...] = jnp.zeros_like(state_ref)

    c = RET_CHUNK
    d = RET_HEAD_DIM
    span = decay_ref.shape[-1]
    rows = q_ref.shape[0]
    gain = gain_ref[...]
    for hh in range(state_ref.shape[0]):
        lanes = slice(hh * d, (hh + 1) * d)
        decay = decay_ref[hh]
        zeta = zeta_ref[hh]
        xi = xi_ref[hh]
        g_chunk = gc_ref[hh]
        q = [q_ref[m * span:(m + 1) * span, lanes] for m in range(rows // span)]
        kt = [kt_ref[hh, :, m * span:(m + 1) * span] for m in range(rows // span)]
        v = [v_ref[m * span:(m + 1) * span, lanes] for m in range(rows // span)]
        scores = [jnp.dot(q[m], kt[m], preferred_element_type=F32) * decay
                  for m in range(rows // span)]
        inner = [jnp.dot(scores[m].astype(BF16), v[m], preferred_element_type=F32)
                 for m in range(rows // span)]
        contrib = []
        for n in range(rows // c):
            m, sub = divmod(n * c, span)
            kz = (kt[m][:, sub:sub + c].astype(F32) * zeta).astype(BF16)
            contrib.append(jnp.dot(kz, v[m][sub:sub + c], preferred_element_type=F32))
        state = state_ref[hh]
        for n in range(rows // c):
            m, sub = divmod(n * c, span)
            cross = jnp.dot(q[m][sub:sub + c], state.astype(BF16), preferred_element_type=F32) * xi
            state = g_chunk * state + contrib[n]
            ret = _rms_rows(inner[m][sub:sub + c] + cross, gain)
            gate = g_ref[n * c:(n + 1) * c, lanes].astype(F32)
            o_ref[n * c:(n + 1) * c, lanes] = (ret * gate).astype(o_ref.dtype)
        state_ref[hh] = state


def _retention(q, kt, v, g, gain, decay, zeta, xi, gc):
    b, s, width = q.shape
    rows = min(RET_GROUP, s)
    span = decay.shape[-1]
    hps = RET_HEADS_PER_STEP
    blk = pl.BlockSpec((None, rows, hps * RET_HEAD_DIM), lambda bi, h, n: (bi, n, h))
    head = lambda shape: pl.BlockSpec((hps,) + shape, lambda bi, h, n: (h, 0, 0))
    return pl.pallas_call(
        _ret_kernel,
        grid=(b, RET_HEADS // hps, s // rows),
        in_specs=[blk,
                  pl.BlockSpec((None, hps, RET_HEAD_DIM, rows), lambda bi, h, n: (bi, h, 0, n)),
                  blk, blk, _const_spec((1, RET_HEAD_DIM)),
                  head((span, span)), head((1, RET_CHUNK)), head((RET_CHUNK, 1)), head((1, 1))],
        out_specs=blk,
        out_shape=jax.ShapeDtypeStruct((b, s, width), BF16),
        scratch_shapes=[pltpu.VMEM((hps, RET_HEAD_DIM, RET_HEAD_DIM), F32)],
        compiler_params=_params("parallel", "parallel", "arbitrary"),
        name="retention",
    )(q, kt, v, g, gain, decay, zeta, xi, gc)


def _ffn_kernel(*refs, n_mix, tiles_per_seq):
    h_ref = refs[0]
    mix_refs = refs[1:1 + n_mix]
    (wo_ref, g_ref, wup_ref, cw_ref, cb_ref, wd_ref, o_ref,
     ubuf_ref, prev_ref, gate_ref) = refs[1 + n_mix:]
    tm = h_ref.shape[0]
    d_ff = wd_ref.shape[0]
    halo = SUBLANES

    @pl.when(pl.program_id(0) % tiles_per_seq == 0)
    def _():
        prev_ref[...] = jnp.zeros_like(prev_ref)

    h = h_ref[...]
    row0 = 0
    for a_ref in mix_refs:
        width = a_ref.shape[1]
        h = h + jnp.dot(a_ref[...], wo_ref[row0:row0 + width, :], preferred_element_type=F32)
        row0 += width
    hn = _rms_rows(h, g_ref[...]).astype(BF16)
    for c0 in range(0, d_ff, FF_CHUNK):
        cols = slice(c0, c0 + FF_CHUNK)
        u = jnp.dot(hn, wup_ref[:, cols], preferred_element_type=F32)
        v = jnp.dot(hn, wup_ref[:, d_ff + c0:d_ff + c0 + FF_CHUNK], preferred_element_type=F32)
        ubuf_ref[0:halo, :] = prev_ref[:, cols]
        ubuf_ref[halo:halo + tm, :] = u
        prev_ref[:, cols] = u[tm - halo:, :]
        last_tap = CONV_WIDTH - 1
        uc = cb_ref[:, cols] + cw_ref[last_tap:CONV_WIDTH, cols] * u
        for tap in range(last_tap):
            back = last_tap - tap
            uc = uc + cw_ref[tap:tap + 1, cols] * ubuf_ref[halo - back:halo - back + tm, :]
        gate_ref[:, cols] = (_silu(uc) * v).astype(BF16)
    o_ref[...] = h + jnp.dot(gate_ref[...], wd_ref[...], preferred_element_type=F32)


def _ffn(h2, mix, wo_stack, wo_layer, g, wup_stack, cw, cb, wd_stack, layer, seq):
    t, d = h2.shape
    d_ff = wd_stack.shape[1]
    tm = min(FFN_ROW_TILE, seq)
    row = lambda i: (i, 0)
    return pl.pallas_call(
        functools.partial(_ffn_kernel, n_mix=len(mix), tiles_per_seq=seq // tm),
        grid=(t // tm,),
        in_specs=([pl.BlockSpec((tm, d), row)]
                  + [pl.BlockSpec((tm, a.shape[1]), row) for a in mix]
                  + [_layer_spec(wo_stack.shape, wo_layer), _const_spec((1, d)),
                     _layer_spec(wup_stack.shape, layer), _const_spec(cw.shape),
                     _const_spec(cb.shape), _layer_spec(wd_stack.shape, layer)]),
        out_specs=pl.BlockSpec((tm, d), row),
        out_shape=jax.ShapeDtypeStruct((t, d), F32),
        scratch_shapes=[pltpu.VMEM((tm + SUBLANES, FF_CHUNK), F32),
                        pltpu.VMEM((SUBLANES, d_ff), F32),
                        pltpu.VMEM((tm, d_ff), BF16)],
        compiler_params=_params("arbitrary"),
        name="conv_ffn",
    )(h2, *mix, wo_stack, g, wup_stack, cw, cb, wd_stack)


def _log_sigmoid(x):
    return jnp.minimum(x, 0.0) - jnp.log(1.0 + jnp.exp(-jnp.abs(x)))


def _odd_in_kernel(x_ref, g_ref, w_ref, wa_ref, walpha_ref, balpha_ref, tri_ref,
                   qt_ref, ktt_ref, kdt_ref, v_ref, r_ref, dec_ref):
    tm = x_ref.shape[0]
    span = tri_ref.shape[0]
    hn = _rms_rows(x_ref[...], g_ref[...]).astype(BF16)

    def proj(c0, width):
        return jnp.dot(hn, w_ref[:, c0:c0 + width], preferred_element_type=F32)

    ga = jnp.dot(hn, wa_ref[...], preferred_element_type=F32)
    ga_hi = ga.astype(BF16)
    ga_lo = (ga - ga_hi.astype(F32)).astype(BF16)
    lane = lax.broadcasted_iota(jnp.int32, ga.shape, 1)
    middle = jnp.logical_and(lane >= GLA_GATE_RANK, lane < 2 * GLA_GATE_RANK)
    pre = jnp.dot(jnp.where(middle, ga_lo, ga_hi), walpha_ref[...],
                  preferred_element_type=F32) + balpha_ref[...]
    la = _log_sigmoid(pre) * (LOG2E / GLA_GATE_TAU)
    la_hi = la.astype(BF16)
    la_lo = (la - la_hi.astype(F32)).astype(BF16)
    tri = tri_ref[...]
    cum = jnp.concatenate(
        [jnp.dot(tri, la_hi[r0:r0 + span], preferred_element_type=F32)
         + jnp.dot(tri, la_lo[r0:r0 + span], preferred_element_type=F32)
         for r0 in range(0, tm, span)], axis=0)
    chunks = tm // GLA_CHUNK
    last = cum.reshape(chunks, GLA_CHUNK, GLA_K_WIDTH)[:, GLA_CHUNK - 1:GLA_CHUNK, :]
    dec_ref[...] = jnp.exp2(last).reshape(chunks, GLA_K_WIDTH)
    to_end = (jnp.broadcast_to(last, (chunks, GLA_CHUNK, GLA_K_WIDTH)).reshape(tm, GLA_K_WIDTH)
              - cum)

    qt_ref[...] = (proj(0, GLA_K_WIDTH) * GLA_DK ** -0.5 * jnp.exp2(cum)).astype(BF16)
    k = proj(GLA_K_WIDTH, GLA_K_WIDTH)
    kt = k * jnp.exp2(-cum)
    kd = k * jnp.exp2(to_end)
    for h in range(GLA_HEADS):
        sl = slice(h * GLA_DK, (h + 1) * GLA_DK)
        ktt_ref[h] = kt[:, sl].T.astype(BF16)
        kdt_ref[h] = kd[:, sl].T.astype(BF16)
    v_ref[...] = proj(2 * GLA_K_WIDTH, GLA_V_WIDTH).astype(BF16)
    r_ref[...] = _silu(proj(2 * GLA_K_WIDTH + GLA_V_WIDTH, GLA_V_WIDTH)).astype(BF16)


def _odd_in(x2, g, w_stack, layer, wa, walpha, balpha, tri, batch, seq):
    t, d = x2.shape
    tm = min(ROW_TILE, seq)
    tps = seq // tm
    row = lambda i: (i, 0)
    kt_shape = jax.ShapeDtypeStruct((batch, GLA_HEADS, GLA_DK, seq), BF16)
    kt_spec = pl.BlockSpec((None, GLA_HEADS, GLA_DK, tm), lambda i: (i // tps, 0, 0, i % tps))
    vout = jax.ShapeDtypeStruct((t, GLA_V_WIDTH), BF16)
    return pl.pallas_call(
        _odd_in_kernel,
        grid=(t // tm,),
        in_specs=[pl.BlockSpec((tm, d), row), _const_spec((1, d)), _layer_spec(w_stack.shape, layer),
                  _const_spec(wa.shape), _const_spec(walpha.shape), _const_spec(balpha.shape),
                  _const_spec(tri.shape)],
        out_specs=[pl.BlockSpec((tm, GLA_K_WIDTH), row), kt_spec, kt_spec,
                   pl.BlockSpec((tm, GLA_V_WIDTH), row), pl.BlockSpec((tm, GLA_V_WIDTH), row),
                   pl.BlockSpec((tm // GLA_CHUNK, GLA_K_WIDTH), row)],
        out_shape=[jax.ShapeDtypeStruct((t, GLA_K_WIDTH), BF16), kt_shape, kt_shape, vout, vout,
                   jax.ShapeDtypeStruct((t // GLA_CHUNK, GLA_K_WIDTH), F32)],
        compiler_params=_params("parallel"),
        name="odd_in",
    )(x2, g, w_stack, wa, walpha, balpha, tri)


def _gla_kernel(qt_ref, ktt_ref, kdt_ref, v_ref, r_ref, dec_ref, gain_ref, o_ref, state_ref):
    @pl.when(pl.program_id(2) == 0)
    def _():
        state_ref[...] = jnp.zeros_like(state_ref)

    c = GLA_CHUNK
    rows = qt_ref.shape[0]
    span = min(GLA_SPAN, rows)
    ri = lax.broadcasted_iota(jnp.int32, (span, span), 0)
    ci = lax.broadcasted_iota(jnp.int32, (span, span), 1)
    causal = jnp.logical_and(ci <= ri, ci // c == ri // c)
    col_chunk = lax.broadcasted_iota(jnp.int32, (1, span), 1) // c
    gain = gain_ref[...]
    for hh in range(state_ref.shape[0]):
        klanes = slice(hh * GLA_DK, (hh + 1) * GLA_DK)
        vlanes = slice(hh * GLA_DV, (hh + 1) * GLA_DV)
        dec_t = dec_ref[:, klanes].T
        spans = range(rows // span)
        qt = [qt_ref[m * span:(m + 1) * span, klanes] for m in spans]
        ktt = [ktt_ref[hh, :, m * span:(m + 1) * span] for m in spans]
        kdt = [kdt_ref[hh, :, m * span:(m + 1) * span] for m in spans]
        v = [v_ref[m * span:(m + 1) * span, vlanes] for m in spans]
        a = [jnp.where(causal, jnp.dot(qt[m], ktt[m], preferred_element_type=F32), 0.0)
             for m in spans]
        intra = [jnp.dot(a[m].astype(BF16), v[m], preferred_element_type=F32) for m in spans]
        contrib = []
        for n in range(rows // c):
            m, sub = divmod(n * c, span)
            kd_n = jnp.where(col_chunk == sub // c, kdt[m], jnp.zeros_like(kdt[m]))
            contrib.append(jnp.dot(kd_n, v[m], preferred_element_type=F32))
        state = state_ref[hh]
        for n in range(rows // c):
            m, sub = divmod(n * c, span)
            inter = jnp.dot(qt[m][sub:sub + c], state.astype(BF16), preferred_element_type=F32)
            state = dec_t[:, n:n + 1] * state + contrib[n]
            o = _rms_rows(intra[m][sub:sub + c] + inter, gain)
            gate = r_ref[n * c:(n + 1) * c, vlanes].astype(F32)
            o_ref[n * c:(n + 1) * c, vlanes] = (o * gate).astype(o_ref.dtype)
        state_ref[hh] = state


def _gla(qt, ktt, kdt, v, r, dec, gain):
    b, s, kw = qt.shape
    vw = v.shape[-1]
    rows = min(GLA_GROUP, s)
    hps = GLA_HEADS_PER_STEP
    chunks = rows // GLA_CHUNK
    dec = dec.reshape(b, s // rows, chunks, kw)
    qblk = pl.BlockSpec((None, rows, hps * GLA_DK), lambda bi, h, n: (bi, n, h))
    tblk = pl.BlockSpec((None, hps, GLA_DK, rows), lambda bi, h, n: (bi, h, 0, n))
    vblk = pl.BlockSpec((None, rows, hps * GLA_DV), lambda bi, h, n: (bi, n, h))
    dblk = pl.BlockSpec((None, None, chunks, hps * GLA_DK), lambda bi, h, n: (bi, n, 0, h))
    return pl.pallas_call(
        _gla_kernel,
        grid=(b, GLA_HEADS // hps, s // rows),
        in_specs=[qblk, tblk, tblk, vblk, vblk, dblk, _const_spec((1, GLA_DV))],
        out_specs=vblk,
        out_shape=jax.ShapeDtypeStruct((b, s, vw), BF16),
        scratch_shapes=[pltpu.VMEM((hps, GLA_DK, GLA_DV), F32)],
        compiler_params=_params("parallel", "parallel", "arbitrary"),
        name="gla",
    )(qt, ktt, kdt, v, r, dec, gain)


def _rope_tables(seq, tile):
    half = RET_HEAD_DIM // 2
    inv = ROPE_BASE ** (-np.arange(half, dtype=np.float64) / half)

    def both(pos):
        ang = pos[:, None] * inv[None, :]
        dup = lambda t: np.concatenate([t, t], axis=-1).astype(np.float32)
        return dup(np.cos(ang)), dup(np.sin(ang))

    cos_a, sin_a = both(np.arange(0, seq, tile, dtype=np.float64))
    cos_b, sin_b = both(np.arange(tile, dtype=np.float64))
    return (jnp.asarray(cos_a[:, None, :]), jnp.asarray(sin_a[:, None, :]),
            jnp.asarray(cos_b), jnp.asarray(sin_b))


def _retention_tables(span):
    c = RET_CHUNK
    lg = np.log(1.0 - np.exp2(-5.0 - np.arange(RET_HEADS, dtype=np.float64)))[:, None]
    pos = np.arange(span, dtype=np.float64)
    diff = pos[:, None] - pos[None, :]
    same = (pos[:, None] // c) == (pos[None, :] // c)
    decay = np.where((diff >= 0) & same, np.exp(lg[:, :, None] * np.maximum(diff, 0.0)), 0.0)
    pos = np.arange(c, dtype=np.float64)
    zeta = np.exp(lg * (c - 1 - pos))[:, None, :]
    xi = np.exp(lg * (pos + 1.0))[:, :, None]
    g_chunk = np.exp(lg * c)[:, :, None]
    return tuple(jnp.asarray(t.astype(np.float32)) for t in (decay, zeta, xi, g_chunk))


def _suffix_tri(n):
    idx = np.arange(n)
    return jnp.asarray(idx[:, None] >= idx[None, :], dtype=BF16)


def _chunk_prefix_tri(rows, chunk):
    idx = np.arange(rows)
    same = (idx[:, None] // chunk) == (idx[None, :] // chunk)
    return jnp.asarray(same & (idx[None, :] <= idx[:, None]), dtype=BF16)


def _gate_operands(w_gate_in, w_alpha):
    rank = GLA_GATE_RANK
    wa = jnp.pad(jnp.tile(w_gate_in, (1, 3)), ((0, 0), (0, LANES - 3 * rank))).astype(BF16)
    w_hi = w_alpha.astype(BF16)
    w_lo = (w_alpha - w_hi.astype(F32)).astype(BF16)
    walpha = jnp.pad(jnp.concatenate([w_hi, w_hi, w_lo], axis=0), ((0, LANES - 3 * rank), (0, 0)))
    return wa, walpha


def kernel(x, mix_norm_g, even_w_in, sb_q_gain, sb_k_gain, ret_out_gain, even_w_out,
           odd_w_in, gla_w_alpha, gla_b_alpha, gla_out_gain, odd_w_out,
           ffn_norm_g, ffn_w_up, ffn_conv_w, ffn_conv_b, ffn_w_down):
    b, s, d = x.shape
    t = b * s
    depth = mix_norm_g.shape[0]
    h = x.reshape(t, d)

    rope_tables = _rope_tables(s, min(ROW_TILE, s))
    decay, zeta, xi, g_chunk = _retention_tables(min(RET_SPAN, s))
    sb_blk = min(SB_BLOCK, s)
    sb_tri = _suffix_tri(sb_blk)
    gla_tri = _chunk_prefix_tri(min(GLA_SPAN, s), GLA_CHUNK)

    even_w_in, even_w_out, odd_w_in_b, odd_w_out, ffn_w_up, ffn_w_down = (
        w.astype(BF16) for w in (even_w_in, even_w_out, odd_w_in, odd_w_out, ffn_w_up, ffn_w_down))

    for layer in range(depth):
        g_mix = mix_norm_g[layer][None, :]
        if layer % 2 == 0:
            e = layer // 2
            qg = jnp.tile(sb_q_gain[e], 2)[None, :]
            kg = jnp.tile(sb_k_gain[e], 2)[None, :]
            sbq, sbkt, sbv, rq, rkt, rv, rg = _even_in(
                h, g_mix, even_w_in, e, qg, kg, rope_tables, b, s, sb_blk)
            out_a = _sb_attention(sbq.reshape(b, s, SB_WIDTH), sbkt, sbv.reshape(b, s, SB_WIDTH),
                                  sb_tri, qg, kg)
            shp = (b, s, RET_WIDTH)
            out_b = _retention(rq.reshape(shp), rkt, rv.reshape(shp), rg.reshape(shp),
                               ret_out_gain[e][None, :], decay, zeta, xi, g_chunk)
            mix = (out_a.reshape(t, SB_WIDTH), out_b.reshape(t, RET_WIDTH))
            w_out, w_out_layer = even_w_out, e
        else:
            o = layer // 2
            n_main = 2 * GLA_K_WIDTH + 2 * GLA_V_WIDTH
            wa, walpha = _gate_operands(odd_w_in[o, :, n_main:], gla_w_alpha[o])
            qt, ktt, kdt, gv, gr, dec = _odd_in(h, g_mix, odd_w_in_b, o, wa, walpha,
                                                gla_b_alpha[o][None, :], gla_tri, b, s)
            og = _gla(qt.reshape(b, s, GLA_K_WIDTH), ktt, kdt,
                      gv.reshape(b, s, GLA_V_WIDTH), gr.reshape(b, s, GLA_V_WIDTH),
                      dec.reshape(b, s // GLA_CHUNK, GLA_K_WIDTH), gla_out_gain[o][None, :])
            mix = (og.reshape(t, GLA_V_WIDTH),)
            w_out, w_out_layer = odd_w_out, o
        h = _ffn(h, mix, w_out, w_out_layer, ffn_norm_g[layer][None, :], ffn_w_up,
                 ffn_conv_w[layer], ffn_conv_b[layer][None, :], ffn_w_down, layer, s)
    return h.reshape(b, s, d)
```

```python
import functools

import numpy as np

import jax
import jax.numpy as jnp
from jax import lax
from jax.experimental import pallas as pl
from jax.experimental.pallas import tpu as pltpu

F32 = jnp.float32
BF16 = jnp.bfloat16

EPS = 1e-6
LOG2E = 1.4426950408889634
ROPE_BASE = 10000.0
EXP2_UNDERFLOW = 150.0
SB_NORM_SLACK = 1.05
SOFTPLUS2_LINEAR_ABOVE = 32.0
MASKED_SCORE = -1e30

SB_HEADS = 8
SB_HEAD_DIM = 64
SB_WIDTH = SB_HEADS * SB_HEAD_DIM
RET_HEADS = 4
RET_HEAD_DIM = 128
RET_WIDTH = RET_HEADS * RET_HEAD_DIM
RET_CHUNK = 128
GLA_HEADS = 4
GLA_DK = 128
GLA_DV = 256
GLA_K_WIDTH = GLA_HEADS * GLA_DK
GLA_V_WIDTH = GLA_HEADS * GLA_DV
GLA_GATE_RANK = 16
GLA_GATE_TAU = 16.0
GLA_CHUNK = 64
CONV_WIDTH = 3

LANES = 128
SUBLANES = 8
MXU_DIM = 256
VMEM_LIMIT = 56 * 1024 * 1024

ROW_TILE = 1024
FFN_ROW_TILE = 1024
SB_BLOCK = MXU_DIM
SB_QBLOCKS_PER_STEP = 4
RET_HEADS_PER_STEP = 4
RET_GROUP = 1024
RET_SPAN = 2 * RET_CHUNK
GLA_GROUP = 512
GLA_HEADS_PER_STEP = 4
GLA_SPAN = 4 * GLA_CHUNK
FF_CHUNK = 256


def _params(*sem):
    return pltpu.CompilerParams(dimension_semantics=sem, vmem_limit_bytes=VMEM_LIMIT)


def _const_spec(shape):
    n = len(shape)
    return pl.BlockSpec(shape, lambda *_: (0,) * n)


def _layer_spec(stacked_shape, layer):
    zeros = (0,) * (len(stacked_shape) - 1)
    return pl.BlockSpec((None,) + tuple(stacked_shape[1:]), lambda *_: (layer,) + zeros,
                        pipeline_mode=pl.Buffered(1))


def _rms_rows(x, g):
    ms = jnp.mean(x * x, axis=-1, keepdims=True)
    return x * lax.rsqrt(ms + EPS) * g


def _silu(x):
    return x / (1.0 + jnp.exp2(x * -LOG2E))


def _half_head_rms(blk):
    sq = blk * blk
    lane = lax.broadcasted_iota(jnp.int32, blk.shape, 1)
    lo = lane < SB_HEAD_DIM
    s_lo = jnp.sum(jnp.where(lo, sq, 0.0), axis=-1, keepdims=True)
    s_hi = jnp.sum(jnp.where(lo, 0.0, sq), axis=-1, keepdims=True)
    ms = jnp.where(lo, s_lo, s_hi) * (1.0 / SB_HEAD_DIM)
    return blk * lax.rsqrt(ms + EPS)


def _even_in_kernel(x_ref, g_ref, w_ref, qg_ref, kg_ref, ca_ref, sa_ref, cb_ref, sb_ref,
                    sbq_ref, sbkt_ref, sbv_ref, rq_ref, rkt_ref, rv_ref, rg_ref):
    tm = x_ref.shape[0]
    hn = _rms_rows(x_ref[...], g_ref[...]).astype(BF16)

    def proj(seg):
        return jnp.dot(hn, w_ref[:, seg * SB_WIDTH:(seg + 1) * SB_WIDTH],
                       preferred_element_type=F32)

    p = proj(1)
    blk = sbkt_ref.shape[-1]
    for hp in range(SB_WIDTH // LANES):
        sl = slice(hp * LANES, (hp + 1) * LANES)
        kt = (_half_head_rms(p[:, sl]) * kg_ref[...]).T
        for c in range(tm // blk):
            sbkt_ref[hp, c] = kt[:, c * blk:(c + 1) * blk].astype(BF16)

    cos_a, sin_a, cos_b, sin_b = ca_ref[...], sa_ref[...], cb_ref[...], sb_ref[...]
    sign = jnp.where(lax.broadcasted_iota(jnp.int32, (1, LANES), 1) < RET_HEAD_DIM // 2, -1.0, 1.0)
    cos2 = cos_a * cos_b - sin_a * sin_b
    sin2 = sign * (sin_a * cos_b + cos_a * sin_b)

    def rope(blk):
        return blk * cos2 + pltpu.roll(blk, RET_HEAD_DIM // 2, 1) * sin2

    p = proj(4)
    for h in range(RET_HEADS):
        sl = slice(h * LANES, (h + 1) * LANES)
        rkt_ref[h] = (rope(p[:, sl]) * RET_HEAD_DIM ** -0.5).T.astype(BF16)
    q_scale = SB_HEAD_DIM ** -0.5 * LOG2E
    p = proj(0)
    for hp in range(SB_WIDTH // LANES):
        sl = slice(hp * LANES, (hp + 1) * LANES)
        sbq_ref[:, sl] = (_half_head_rms(p[:, sl]) * (qg_ref[...] * q_scale)).astype(BF16)
    p = proj(3)
    for h in range(RET_HEADS):
        sl = slice(h * LANES, (h + 1) * LANES)
        rq_ref[:, sl] = rope(p[:, sl]).astype(BF16)
    rg_ref[...] = _silu(proj(6)).astype(BF16)
    sbv_ref[...] = proj(2).astype(BF16)
    rv_ref[...] = proj(5).astype(BF16)


def _even_in(x2, g, w_stack, layer, qg, kg, rope_tables, batch, seq, sb_blk):
    t, d = x2.shape
    tm = min(ROW_TILE, seq)
    tps = seq // tm
    tile_start = pl.BlockSpec((None, 1, LANES), lambda i: (i % tps, 0, 0))
    row = lambda i: (i, 0)
    pairs = SB_WIDTH // LANES
    flat = jax.ShapeDtypeStruct((t, SB_WIDTH), BF16)
    flat_spec = pl.BlockSpec((tm, SB_WIDTH), row)
    sbkt = jax.ShapeDtypeStruct((batch, pairs, seq // sb_blk, LANES, sb_blk), BF16)
    sbkt_spec = pl.BlockSpec((None, pairs, tm // sb_blk, LANES, sb_blk),
                             lambda i: (i // tps, 0, i % tps, 0, 0))
    rkt = jax.ShapeDtypeStruct((batch, RET_HEADS, RET_HEAD_DIM, seq), BF16)
    rkt_spec = pl.BlockSpec((None, RET_HEADS, RET_HEAD_DIM, tm), lambda i: (i // tps, 0, 0, i % tps))
    return pl.pallas_call(
        _even_in_kernel,
        grid=(t // tm,),
        in_specs=[
            pl.BlockSpec((tm, d), row),
            _const_spec((1, d)),
            _layer_spec(w_stack.shape, layer),
            _const_spec((1, LANES)),
            _const_spec((1, LANES)),
            tile_start,
            tile_start,
            _const_spec((tm, LANES)),
            _const_spec((tm, LANES)),
        ],
        out_specs=[flat_spec, sbkt_spec, flat_spec, flat_spec, rkt_spec, flat_spec, flat_spec],
        out_shape=[flat, sbkt, flat, flat, rkt, flat, flat],
        compiler_params=_params("parallel"),
        name="even_in",
    )(x2, g, w_stack, qg, kg, *rope_tables)


def _softplus2(z):
    return jnp.where(z > SOFTPLUS2_LINEAR_ABOVE, z, jnp.log2(1.0 + jnp.exp2(z)))


def _sb_kernel(q_ref, kt_ref, v_ref, tri_ref, qg_ref, kg_ref, o_ref, acc_ref, carry_ref):
    step = pl.program_id(2)
    blk = kt_ref.shape[-1]
    per_step = q_ref.shape[0] // blk
    z_bound = (SB_NORM_SLACK * SB_HEAD_DIM * SB_HEAD_DIM ** -0.5 * LOG2E
               * jnp.max(jnp.abs(qg_ref[...])) * jnp.max(jnp.abs(kg_ref[...])))
    dead_above = z_bound + EXP2_UNDERFLOW
    lane = lax.broadcasted_iota(jnp.int32, (blk, LANES), 1)
    lo = lane < SB_HEAD_DIM
    row_id = lax.broadcasted_iota(jnp.int32, (blk, blk), 0)
    col_id = lax.broadcasted_iota(jnp.int32, (blk, blk), 1)
    below_diag = col_id < row_id
    tri = tri_ref[...]
    heads = range(2)

    def split_heads(q):
        zero = jnp.zeros_like(q)
        return (jnp.where(lo, q, zero), jnp.where(lo, zero, q))

    def key_blocks(j_a):
        j_b = jnp.maximum(j_a - 1, 0)
        return j_a > 0, j_b

    def values(j):
        return v_ref[pl.ds(pl.multiple_of(j * blk, blk), blk), :]

    def diagonal_pairs(q_heads, j_as):
        chains = [(n, h) for n in range(len(j_as)) for h in heads]
        has_b, kt_a, kt_b, v_a, v_b = [], [], [], [], []
        for j_a in j_as:
            flag, j_b = key_blocks(j_a)
            has_b.append(flag)
            kt_a.append(kt_ref[j_a])
            kt_b.append(kt_ref[j_b])
            v_a.append(values(j_a))
            v_b.append(values(j_b))
        z_a = {(n, h): jnp.dot(q_heads[n][h], kt_a[n], preferred_element_type=F32) for n, h in chains}
        z_b = {(n, h): jnp.dot(q_heads[n][h], kt_b[n], preferred_element_type=F32) for n, h in chains}
        z_a = {c: jnp.where(below_diag, z, MASKED_SCORE) for c, z in z_a.items()}
        sp_a = {c: _softplus2(z) for c, z in z_a.items()}
        sp_b = {c: _softplus2(z) for c, z in z_b.items()}
        cs_a = {c: jnp.dot(sp.astype(BF16), tri, preferred_element_type=F32) for c, sp in sp_a.items()}
        cs_b = {c: jnp.dot(sp.astype(BF16), tri, preferred_element_type=F32) for c, sp in sp_b.items()}
        w_a, w_b, carry_out = {}, {}, {}
        for n, h in chains:
            c = (n, h)
            tot_a = jnp.sum(sp_a[c], axis=-1, keepdims=True)
            tot_b = jnp.where(has_b[n], jnp.sum(sp_b[c], axis=-1, keepdims=True), 0.0)
            before_b = jnp.where(has_b[n], tot_a, -MASKED_SCORE)
            w_a[c] = jnp.exp2(z_a[c] - cs_a[c])
            w_b[c] = jnp.exp2(z_b[c] - (cs_b[c] + before_b))
            carry_out[c] = tot_a + tot_b
        for n, h in chains:
            c = (n, h)
            acc_ref[n, h] = (jnp.dot(w_a[c].astype(BF16), v_a[n], preferred_element_type=F32)
                             + jnp.dot(w_b[c].astype(BF16), v_b[n], preferred_element_type=F32))
            carry_ref[n, h] = carry_out[c]

    first = step * per_step
    q_heads = [split_heads(q_ref[n * blk:(n + 1) * blk, :]) for n in range(per_step)]
    diagonal_pairs(q_heads, [first + n for n in range(per_step)])

    def older_pair(n, j_a):
        has_b, j_b = key_blocks(j_a)
        kt_a, kt_b = kt_ref[j_a], kt_ref[j_b]
        v_a, v_b = values(j_a), values(j_b)
        z_a = [jnp.dot(q_heads[n][h], kt_a, preferred_element_type=F32) for h in heads]
        z_b = [jnp.dot(q_heads[n][h], kt_b, preferred_element_type=F32) for h in heads]
        sp_a = [_softplus2(z) for z in z_a]
        sp_b = [_softplus2(z) for z in z_b]
        cs_a = [jnp.dot(sp.astype(BF16), tri, preferred_element_type=F32) for sp in sp_a]
        cs_b = [jnp.dot(sp.astype(BF16), tri, preferred_element_type=F32) for sp in sp_b]
        for h in heads:
            carry = carry_ref[n, h]
            after_a = carry + jnp.sum(sp_a[h], axis=-1, keepdims=True)
            tot_b = jnp.where(has_b, jnp.sum(sp_b[h], axis=-1, keepdims=True), 0.0)
            before_b = jnp.where(has_b, after_a, -MASKED_SCORE)
            w_a = jnp.exp2(z_a[h] - (cs_a[h] + carry))
            w_b = jnp.exp2(z_b[h] - (cs_b[h] + before_b))
            acc_ref[n, h] += (jnp.dot(w_a.astype(BF16), v_a, preferred_element_type=F32)
                              + jnp.dot(w_b.astype(BF16), v_b, preferred_element_type=F32))
            carry_ref[n, h] = after_a + tot_b

    def live(n):
        return jnp.minimum(jnp.min(carry_ref[n, 0]), jnp.min(carry_ref[n, 1])) <= dead_above

    def write_out(n):
        o_ref[n * blk:(n + 1) * blk, :] = jnp.where(lo, acc_ref[n, 0], acc_ref[n, 1]).astype(o_ref.dtype)

    least = carry_ref[0, 0]
    for n in range(per_step):
        for h in heads:
            least = jnp.minimum(least, carry_ref[n, h])
    any_live = jnp.min(least) <= dead_above

    @pl.when(jnp.logical_not(any_live))
    def _():
        for n in range(per_step):
            write_out(n)

    @pl.when(any_live)
    def _():
        for n in range(per_step):
            def body(c, n=n):
                older_pair(n, c[0])
                return c[0] - 2, live(n)

            lax.while_loop(lambda c: jnp.logical_and(c[0] >= 0, c[1]), body, (first + n - 2, live(n)))
            write_out(n)


def _sb_attention(q, kt, v, tri, qg, kg):
    b, s, width = q.shape
    blk = kt.shape[-1]
    n_blocks = s // blk
    per_step = min(SB_QBLOCKS_PER_STEP, n_blocks)
    assert n_blocks % per_step == 0
    rows = per_step * blk
    return pl.pallas_call(
        _sb_kernel,
        grid=(b, width // LANES, n_blocks // per_step),
        in_specs=[
            pl.BlockSpec((None, rows, LANES), lambda bi, hp, i: (bi, i, hp)),
            pl.BlockSpec((None, None, n_blocks, LANES, blk), lambda bi, hp, i: (bi, hp, 0, 0, 0)),
            pl.BlockSpec((None, s, LANES), lambda bi, hp, i: (bi, 0, hp)),
            _const_spec(tri.shape),
            _const_spec((1, LANES)),
            _const_spec((1, LANES)),
        ],
        out_specs=pl.BlockSpec((None, rows, LANES), lambda bi, hp, i: (bi, i, hp)),
        out_shape=jax.ShapeDtypeStruct((b, s, width), BF16),
        scratch_shapes=[pltpu.VMEM((per_step, 2, blk, LANES), F32),
                        pltpu.VMEM((per_step, 2, blk, 1), F32)],
        compiler_params=_params("parallel", "parallel", "arbitrary"),
        name="sb_attention",
    )(q, kt, v, tri, qg, kg)


def _ret_kernel(q_ref, kt_ref, v_ref, g_ref, gain_ref, decay_ref, zeta_ref, xi_ref, gc_ref,
                o_ref, state_ref):
    @pl.when(pl.program_id(2) == 0)
    def _():
        state_ref[...] = jnp.zeros_like(state_ref)

    c = RET_CHUNK
    d = RET_HEAD_DIM
    span = decay_ref.shape[-1]
    rows = q_ref.shape[0]
    gain = gain_ref[...]
    for hh in range(state_ref.shape[0]):
        lanes = slice(hh * d, (hh + 1) * d)
        decay = decay_ref[hh]
        zeta = zeta_ref[hh]
        xi = xi_ref[hh]
        g_chunk = gc_ref[hh]
        q = [q_ref[m * span:(m + 1) * span, lanes] for m in range(rows // span)]
        kt = [kt_ref[hh, :, m * span:(m + 1) * span] for m in range(rows // span)]
        v = [v_ref[m * span:(m + 1) * span, lanes] for m in range(rows // span)]
        scores = [jnp.dot(q[m], kt[m], preferred_element_type=F32) * decay
                  for m in range(rows // span)]
        inner = [jnp.dot(scores[m].astype(BF16), v[m], preferred_element_type=F32)
                 for m in range(rows // span)]
        contrib = []
        for n in range(rows // c):
            m, sub = divmod(n * c, span)
            kz = (kt[m][:, sub:sub + c].astype(F32) * zeta).astype(BF16)
            contrib.append(jnp.dot(kz, v[m][sub:sub + c], preferred_element_type=F32))
        state = state_ref[hh]
        for n in range(rows // c):
            m, sub = divmod(n * c, span)
            cross = jnp.dot(q[m][sub:sub + c], state.astype(BF16), preferred_element_type=F32) * xi
            state = g_chunk * state + contrib[n]
            ret = _rms_rows(inner[m][sub:sub + c] + cross, gain)
            gate = g_ref[n * c:(n + 1) * c, lanes].astype(F32)
            o_ref[n * c:(n + 1) * c, lanes] = (ret * gate).astype(o_ref.dtype)
        state_ref[hh] = state


def _retention(q, kt, v, g, gain, decay, zeta, xi, gc):
    b, s, width = q.shape
    rows = min(RET_GROUP, s)
    span = decay.shape[-1]
    hps = RET_HEADS_PER_STEP
    blk = pl.BlockSpec((None, rows, hps * RET_HEAD_DIM), lambda bi, h, n: (bi, n, h))
    head = lambda shape: pl.BlockSpec((hps,) + shape, lambda bi, h, n: (h, 0, 0))
    return pl.pallas_call(
        _ret_kernel,
        grid=(b, RET_HEADS // hps, s // rows),
        in_specs=[blk,
                  pl.BlockSpec((None, hps, RET_HEAD_DIM, rows), lambda bi, h, n: (bi, h, 0, n)),
                  blk, blk, _const_spec((1, RET_HEAD_DIM)),
                  head((span, span)), head((1, RET_CHUNK)), head((RET_CHUNK, 1)), head((1, 1))],
        out_specs=blk,
        out_shape=jax.ShapeDtypeStruct((b, s, width), BF16),
        scratch_shapes=[pltpu.VMEM((hps, RET_HEAD_DIM, RET_HEAD_DIM), F32)],
        compiler_params=_params("parallel", "parallel", "arbitrary"),
        name="retention",
    )(q, kt, v, g, gain, decay, zeta, xi, gc)


def _ffn_kernel(*refs, n_mix, tiles_per_seq):
    h_ref = refs[0]
    mix_refs = refs[1:1 + n_mix]
    (wo_ref, g_ref, wup_ref, cw_ref, cb_ref, wd_ref, o_ref,
     ubuf_ref, prev_ref, gate_ref) = refs[1 + n_mix:]
    tm = h_ref.shape[0]
    d_ff = wd_ref.shape[0]
    halo = SUBLANES

    @pl.when(pl.program_id(0) % tiles_per_seq == 0)
    def _():
        prev_ref[...] = jnp.zeros_like(prev_ref)

    h = h_ref[...]
    row0 = 0
    for a_ref in mix_refs:
        width = a_ref.shape[1]
        h = h + jnp.dot(a_ref[...], wo_ref[row0:row0 + width, :], preferred_element_type=F32)
        row0 += width
    hn = _rms_rows(h, g_ref[...]).astype(BF16)
    for c0 in range(0, d_ff, FF_CHUNK):
        cols = slice(c0, c0 + FF_CHUNK)
        u = jnp.dot(hn, wup_ref[:, cols], preferred_element_type=F32)
        v = jnp.dot(hn, wup_ref[:, d_ff + c0:d_ff + c0 + FF_CHUNK], preferred_element_type=F32)
        ubuf_ref[0:halo, :] = prev_ref[:, cols]
        ubuf_ref[halo:halo + tm, :] = u
        prev_ref[:, cols] = u[tm - halo:, :]
        last_tap = CONV_WIDTH - 1
        uc = cb_ref[:, cols] + cw_ref[last_tap:CONV_WIDTH, cols] * u
        for tap in range(last_tap):
            back = last_tap - tap
            uc = uc + cw_ref[tap:tap + 1, cols] * ubuf_ref[halo - back:halo - back + tm, :]
        gate_ref[:, cols] = (_silu(uc) * v).astype(BF16)
    o_ref[...] = h + jnp.dot(gate_ref[...], wd_ref[...], preferred_element_type=F32)


def _ffn(h2, mix, wo_stack, wo_layer, g, wup_stack, cw, cb, wd_stack, layer, seq):
    t, d = h2.shape
    d_ff = wd_stack.shape[1]
    tm = min(FFN_ROW_TILE, seq)
    row = lambda i: (i, 0)
    return pl.pallas_call(
        functools.partial(_ffn_kernel, n_mix=len(mix), tiles_per_seq=seq // tm),
        grid=(t // tm,),
        in_specs=([pl.BlockSpec((tm, d), row)]
                  + [pl.BlockSpec((tm, a.shape[1]), row) for a in mix]
                  + [_layer_spec(wo_stack.shape, wo_layer), _const_spec((1, d)),
                     _layer_spec(wup_stack.shape, layer), _const_spec(cw.shape),
                     _const_spec(cb.shape), _layer_spec(wd_stack.shape, layer)]),
        out_specs=pl.BlockSpec((tm, d), row),
        out_shape=jax.ShapeDtypeStruct((t, d), F32),
        scratch_shapes=[pltpu.VMEM((tm + SUBLANES, FF_CHUNK), F32),
                        pltpu.VMEM((SUBLANES, d_ff), F32),
                        pltpu.VMEM((tm, d_ff), BF16)],
        compiler_params=_params("arbitrary"),
        name="conv_ffn",
    )(h2, *mix, wo_stack, g, wup_stack, cw, cb, wd_stack)


def _log_sigmoid(x):
    return jnp.minimum(x, 0.0) - jnp.log(1.0 + jnp.exp(-jnp.abs(x)))


def _odd_in_kernel(x_ref, g_ref, w_ref, wa_ref, walpha_ref, balpha_ref, tri_ref,
                   qt_ref, ktt_ref, kdt_ref, v_ref, r_ref, dec_ref):
    tm = x_ref.shape[0]
    span = tri_ref.shape[0]
    hn = _rms_rows(x_ref[...], g_ref[...]).astype(BF16)

    def proj(c0, width):
        return jnp.dot(hn, w_ref[:, c0:c0 + width], preferred_element_type=F32)

    ga = jnp.dot(hn, wa_ref[...], preferred_element_type=F32)
    v_ref[...] = proj(2 * GLA_K_WIDTH, GLA_V_WIDTH).astype(BF16)
    r_ref[...] = _silu(proj(2 * GLA_K_WIDTH + GLA_V_WIDTH, GLA_V_WIDTH)).astype(BF16)
    ga_hi = ga.astype(BF16)
    ga_lo = (ga - ga_hi.astype(F32)).astype(BF16)
    lane = lax.broadcasted_iota(jnp.int32, ga.shape, 1)
    middle = jnp.logical_and(lane >= GLA_GATE_RANK, lane < 2 * GLA_GATE_RANK)
    pre = jnp.dot(jnp.where(middle, ga_lo, ga_hi), walpha_ref[...],
                  preferred_element_type=F32) + balpha_ref[...]
    la = _log_sigmoid(pre) * (LOG2E / GLA_GATE_TAU)
    la_hi = la.astype(BF16)
    la_lo = (la - la_hi.astype(F32)).astype(BF16)
    tri = tri_ref[...]
    cum = jnp.concatenate(
        [jnp.dot(tri, la_hi[r0:r0 + span], preferred_element_type=F32)
         + jnp.dot(tri, la_lo[r0:r0 + span], preferred_element_type=F32)
         for r0 in range(0, tm, span)], axis=0)
    chunks = tm // GLA_CHUNK
    last = cum.reshape(chunks, GLA_CHUNK, GLA_K_WIDTH)[:, GLA_CHUNK - 1:GLA_CHUNK, :]
    dec_ref[...] = jnp.exp2(last).reshape(chunks, GLA_K_WIDTH)
    to_end = (jnp.broadcast_to(last, (chunks, GLA_CHUNK, GLA_K_WIDTH)).reshape(tm, GLA_K_WIDTH)
              - cum)

    k = proj(GLA_K_WIDTH, GLA_K_WIDTH)
    kt = k * jnp.exp2(-cum)
    kd = k * jnp.exp2(to_end)
    for h in range(GLA_HEADS):
        sl = slice(h * GLA_DK, (h + 1) * GLA_DK)
        ktt_ref[h] = kt[:, sl].T.astype(BF16)
        kdt_ref[h] = kd[:, sl].T.astype(BF16)
    qt_ref[...] = (proj(0, GLA_K_WIDTH) * GLA_DK ** -0.5 * jnp.exp2(cum)).astype(BF16)


def _odd_in(x2, g, w_stack, layer, wa, walpha, balpha, tri, batch, seq):
    t, d = x2.shape
    tm = min(ROW_TILE, seq)
    tps = seq // tm
    row = lambda i: (i, 0)
    kt_shape = jax.ShapeDtypeStruct((batch, GLA_HEADS, GLA_DK, seq), BF16)
    kt_spec = pl.BlockSpec((None, GLA_HEADS, GLA_DK, tm), lambda i: (i // tps, 0, 0, i % tps))
    vout = jax.ShapeDtypeStruct((t, GLA_V_WIDTH), BF16)
    return pl.pallas_call(
        _odd_in_kernel,
        grid=(t // tm,),
        in_specs=[pl.BlockSpec((tm, d), row), _const_spec((1, d)), _layer_spec(w_stack.shape, layer),
                  _const_spec(wa.shape), _const_spec(walpha.shape), _const_spec(balpha.shape),
                  _const_spec(tri.shape)],
        out_specs=[pl.BlockSpec((tm, GLA_K_WIDTH), row), kt_spec, kt_spec,
                   pl.BlockSpec((tm, GLA_V_WIDTH), row), pl.BlockSpec((tm, GLA_V_WIDTH), row),
                   pl.BlockSpec((tm // GLA_CHUNK, GLA_K_WIDTH), row)],
        out_shape=[jax.ShapeDtypeStruct((t, GLA_K_WIDTH), BF16), kt_shape, kt_shape, vout, vout,
                   jax.ShapeDtypeStruct((t // GLA_CHUNK, GLA_K_WIDTH), F32)],
        compiler_params=_params("parallel"),
        name="odd_in",
    )(x2, g, w_stack, wa, walpha, balpha, tri)


def _gla_kernel(qt_ref, ktt_ref, kdt_ref, v_ref, r_ref, dec_ref, gain_ref, o_ref, state_ref):
    @pl.when(pl.program_id(2) == 0)
    def _():
        state_ref[...] = jnp.zeros_like(state_ref)

    c = GLA_CHUNK
    rows = qt_ref.shape[0]
    span = min(GLA_SPAN, rows)
    ri = lax.broadcasted_iota(jnp.int32, (span, span), 0)
    ci = lax.broadcasted_iota(jnp.int32, (span, span), 1)
    causal = jnp.logical_and(ci <= ri, ci // c == ri // c)
    col_chunk = lax.broadcasted_iota(jnp.int32, (1, span), 1) // c
    gain = gain_ref[...]
    for hh in range(state_ref.shape[0]):
        klanes = slice(hh * GLA_DK, (hh + 1) * GLA_DK)
        vlanes = slice(hh * GLA_DV, (hh + 1) * GLA_DV)
        dec_t = dec_ref[:, klanes].T
        spans = range(rows // span)
        qt = [qt_ref[m * span:(m + 1) * span, klanes] for m in spans]
        ktt = [ktt_ref[hh, :, m * span:(m + 1) * span] for m in spans]
        kdt = [kdt_ref[hh, :, m * span:(m + 1) * span] for m in spans]
        v = [v_ref[m * span:(m + 1) * span, vlanes] for m in spans]
        a = [jnp.where(causal, jnp.dot(qt[m], ktt[m], preferred_element_type=F32), 0.0)
             for m in spans]
        intra = [jnp.dot(a[m].astype(BF16), v[m], preferred_element_type=F32) for m in spans]
        contrib = []
        for n in range(rows // c):
            m, sub = divmod(n * c, span)
            kd_n = jnp.where(col_chunk == sub // c, kdt[m], jnp.zeros_like(kdt[m]))
            contrib.append(jnp.dot(kd_n, v[m], preferred_element_type=F32))
        state = state_ref[hh]
        for n in range(rows // c):
            m, sub = divmod(n * c, span)
            inter = jnp.dot(qt[m][sub:sub + c], state.astype(BF16), preferred_element_type=F32)
            state = dec_t[:, n:n + 1] * state + contrib[n]
            o = _rms_rows(intra[m][sub:sub + c] + inter, gain)
            gate = r_ref[n * c:(n + 1) * c, vlanes].astype(F32)
            o_ref[n * c:(n + 1) * c, vlanes] = (o * gate).astype(o_ref.dtype)
        state_ref[hh] = state


def _gla(qt, ktt, kdt, v, r, dec, gain):
    b, s, kw = qt.shape
    vw = v.shape[-1]
    rows = min(GLA_GROUP, s)
    hps = GLA_HEADS_PER_STEP
    chunks = rows // GLA_CHUNK
    dec = dec.reshape(b, s // rows, chunks, kw)
    qblk = pl.BlockSpec((None, rows, hps * GLA_DK), lambda bi, h, n: (bi, n, h))
    tblk = pl.BlockSpec((None, hps, GLA_DK, rows), lambda bi, h, n: (bi, h, 0, n))
    vblk = pl.BlockSpec((None, rows, hps * GLA_DV), lambda bi, h, n: (bi, n, h))
    dblk = pl.BlockSpec((None, None, chunks, hps * GLA_DK), lambda bi, h, n: (bi, n, 0, h))
    return pl.pallas_call(
        _gla_kernel,
        grid=(b, GLA_HEADS // hps, s // rows),
        in_specs=[qblk, tblk, tblk, vblk, vblk, dblk, _const_spec((1, GLA_DV))],
        out_specs=vblk,
        out_shape=jax.ShapeDtypeStruct((b, s, vw), BF16),
        scratch_shapes=[pltpu.VMEM((hps, GLA_DK, GLA_DV), F32)],
        compiler_params=_params("parallel", "parallel", "arbitrary"),
        name="gla",
    )(qt, ktt, kdt, v, r, dec, gain)


def _rope_tables(seq, tile):
    half = RET_HEAD_DIM // 2
    inv = ROPE_BASE ** (-np.arange(half, dtype=np.float64) / half)

    def both(pos):
        ang = pos[:, None] * inv[None, :]
        dup = lambda t: np.concatenate([t, t], axis=-1).astype(np.float32)
        return dup(np.cos(ang)), dup(np.sin(ang))

    cos_a, sin_a = both(np.arange(0, seq, tile, dtype=np.float64))
    cos_b, sin_b = both(np.arange(tile, dtype=np.float64))
    return (jnp.asarray(cos_a[:, None, :]), jnp.asarray(sin_a[:, None, :]),
            jnp.asarray(cos_b), jnp.asarray(sin_b))


def _retention_tables(span):
    c = RET_CHUNK
    lg = np.log(1.0 - np.exp2(-5.0 - np.arange(RET_HEADS, dtype=np.float64)))[:, None]
    pos = np.arange(span, dtype=np.float64)
    diff = pos[:, None] - pos[None, :]
    same = (pos[:, None] // c) == (pos[None, :] // c)
    decay = np.where((diff >= 0) & same, np.exp(lg[:, :, None] * np.maximum(diff, 0.0)), 0.0)
    pos = np.arange(c, dtype=np.float64)
    zeta = np.exp(lg * (c - 1 - pos))[:, None, :]
    xi = np.exp(lg * (pos + 1.0))[:, :, None]
    g_chunk = np.exp(lg * c)[:, :, None]
    return tuple(jnp.asarray(t.astype(np.float32)) for t in (decay, zeta, xi, g_chunk))


def _suffix_tri(n):
    idx = np.arange(n)
    return jnp.asarray(idx[:, None] >= idx[None, :], dtype=BF16)


def _chunk_prefix_tri(rows, chunk):
    idx = np.arange(rows)
    same = (idx[:, None] // chunk) == (idx[None, :] // chunk)
    return jnp.asarray(same & (idx[None, :] <= idx[:, None]), dtype=BF16)


def _gate_operands(w_gate_in, w_alpha):
    rank = GLA_GATE_RANK
    wa = jnp.pad(jnp.tile(w_gate_in, (1, 3)), ((0, 0), (0, LANES - 3 * rank))).astype(BF16)
    w_hi = w_alpha.astype(BF16)
    w_lo = (w_alpha - w_hi.astype(F32)).astype(BF16)
    walpha = jnp.pad(jnp.concatenate([w_hi, w_hi, w_lo], axis=0), ((0, LANES - 3 * rank), (0, 0)))
    return wa, walpha


def kernel(x, mix_norm_g, even_w_in, sb_q_gain, sb_k_gain, ret_out_gain, even_w_out,
           odd_w_in, gla_w_alpha, gla_b_alpha, gla_out_gain, odd_w_out,
           ffn_norm_g, ffn_w_up, ffn_conv_w, ffn_conv_b, ffn_w_down):
    b, s, d = x.shape
    t = b * s
    depth = mix_norm_g.shape[0]
    h = x.reshape(t, d)

    rope_tables = _rope_tables(s, min(ROW_TILE, s))
    decay, zeta, xi, g_chunk = _retention_tables(min(RET_SPAN, s))
    sb_blk = min(SB_BLOCK, s)
    sb_tri = _suffix_tri(sb_blk)
    gla_tri = _chunk_prefix_tri(min(GLA_SPAN, s), GLA_CHUNK)

    even_w_in, even_w_out, odd_w_in_b, odd_w_out, ffn_w_up, ffn_w_down = (
        w.astype(BF16) for w in (even_w_in, even_w_out, odd_w_in, odd_w_out, ffn_w_up, ffn_w_down))

    for layer in range(depth):
        g_mix = mix_norm_g[layer][None, :]
        if layer % 2 == 0:
            e = layer // 2
            qg = jnp.tile(sb_q_gain[e], 2)[None, :]
            kg = jnp.tile(sb_k_gain[e], 2)[None, :]
            sbq, sbkt, sbv, rq, rkt, rv, rg = _even_in(
                h, g_mix, even_w_in, e, qg, kg, rope_tables, b, s, sb_blk)
            out_a = _sb_attention(sbq.reshape(b, s, SB_WIDTH), sbkt, sbv.reshape(b, s, SB_WIDTH),
                                  sb_tri, qg, kg)
            shp = (b, s, RET_WIDTH)
            out_b = _retention(rq.reshape(shp), rkt, rv.reshape(shp), rg.reshape(shp),
                               ret_out_gain[e][None, :], decay, zeta, xi, g_chunk)
            mix = (out_a.reshape(t, SB_WIDTH), out_b.reshape(t, RET_WIDTH))
            w_out, w_out_layer = even_w_out, e
        else:
            o = layer // 2
            n_main = 2 * GLA_K_WIDTH + 2 * GLA_V_WIDTH
            wa, walpha = _gate_operands(odd_w_in[o, :, n_main:], gla_w_alpha[o])
            qt, ktt, kdt, gv, gr, dec = _odd_in(h, g_mix, odd_w_in_b, o, wa, walpha,
                                                gla_b_alpha[o][None, :], gla_tri, b, s)
            og = _gla(qt.reshape(b, s, GLA_K_WIDTH), ktt, kdt,
                      gv.reshape(b, s, GLA_V_WIDTH), gr.reshape(b, s, GLA_V_WIDTH),
                      dec.reshape(b, s // GLA_CHUNK, GLA_K_WIDTH), gla_out_gain[o][None, :])
            mix = (og.reshape(t, GLA_V_WIDTH),)
            w_out, w_out_layer = odd_w_out, o
        h = _ffn(h, mix, w_out, w_out_layer, ffn_norm_g[layer][None, :], ffn_w_up,
                 ffn_conv_w[layer], ffn_conv_b[layer][None, :], ffn_w_down, layer, s)
    return h.reshape(b, s, d)
```

```python
import functools

import numpy as np

import jax
import jax.numpy as jnp
from jax import lax
from jax.experimental import pallas as pl
from jax.experimental.pallas import tpu as pltpu

F32 = jnp.float32
BF16 = jnp.bfloat16

EPS = 1e-6
LOG2E = 1.4426950408889634
ROPE_BASE = 10000.0
EXP2_UNDERFLOW = 150.0
SB_NORM_SLACK = 1.05
SOFTPLUS2_LINEAR_ABOVE = 32.0
MASKED_SCORE = -1e30

SB_HEADS = 8
SB_HEAD_DIM = 64
SB_WIDTH = SB_HEADS * SB_HEAD_DIM
RET_HEADS = 4
RET_HEAD_DIM = 128
RET_WIDTH = RET_HEADS * RET_HEAD_DIM
RET_CHUNK = 128
GLA_HEADS = 4
GLA_DK = 128
GLA_DV = 256
GLA_K_WIDTH = GLA_HEADS * GLA_DK
GLA_V_WIDTH = GLA_HEADS * GLA_DV
GLA_GATE_RANK = 16
GLA_GATE_TAU = 16.0
GLA_CHUNK = 64
CONV_WIDTH = 3

LANES = 128
SUBLANES = 8
MXU_DIM = 256
VMEM_LIMIT = 56 * 1024 * 1024

ROW_TILE = 1024
FFN_ROW_TILE = 1024
SB_BLOCK = MXU_DIM
SB_QBLOCKS_PER_STEP = 4
SB_RECENT_ROWS = 176
RET_HEADS_PER_STEP = 4
RET_GROUP = 1024
RET_SPAN = 2 * RET_CHUNK
GLA_GROUP = 512
GLA_HEADS_PER_STEP = 4
GLA_SPAN = 4 * GLA_CHUNK
FF_CHUNK = 256


def _params(*sem):
    return pltpu.CompilerParams(dimension_semantics=sem, vmem_limit_bytes=VMEM_LIMIT)


def _const_spec(shape):
    n = len(shape)
    return pl.BlockSpec(shape, lambda *_: (0,) * n)


def _layer_spec(stacked_shape, layer):
    zeros = (0,) * (len(stacked_shape) - 1)
    return pl.BlockSpec((None,) + tuple(stacked_shape[1:]), lambda *_: (layer,) + zeros,
                        pipeline_mode=pl.Buffered(1))


def _rms_rows(x, g):
    ms = jnp.mean(x * x, axis=-1, keepdims=True)
    return x * lax.rsqrt(ms + EPS) * g


def _silu(x):
    return x / (1.0 + jnp.exp2(x * -LOG2E))


def _half_head_rms(blk):
    sq = blk * blk
    lane = lax.broadcasted_iota(jnp.int32, blk.shape, 1)
    lo = lane < SB_HEAD_DIM
    s_lo = jnp.sum(jnp.where(lo, sq, 0.0), axis=-1, keepdims=True)
    s_hi = jnp.sum(jnp.where(lo, 0.0, sq), axis=-1, keepdims=True)
    ms = jnp.where(lo, s_lo, s_hi) * (1.0 / SB_HEAD_DIM)
    return blk * lax.rsqrt(ms + EPS)


def _even_in_kernel(x_ref, g_ref, w_ref, qg_ref, kg_ref, ca_ref, sa_ref, cb_ref, sb_ref,
                    sbq_ref, sbkt_ref, sbv_ref, rq_ref, rkt_ref, rv_ref, rg_ref):
    tm = x_ref.shape[0]
    hn = _rms_rows(x_ref[...], g_ref[...]).astype(BF16)

    def proj(seg):
        return jnp.dot(hn, w_ref[:, seg * SB_WIDTH:(seg + 1) * SB_WIDTH],
                       preferred_element_type=F32)

    p = proj(1)
    blk = sbkt_ref.shape[-1]
    for hp in range(SB_WIDTH // LANES):
        sl = slice(hp * LANES, (hp + 1) * LANES)
        kt = (_half_head_rms(p[:, sl]) * kg_ref[...]).T
        for c in range(tm // blk):
            sbkt_ref[hp, c] = kt[:, c * blk:(c + 1) * blk].astype(BF16)

    cos_a, sin_a, cos_b, sin_b = ca_ref[...], sa_ref[...], cb_ref[...], sb_ref[...]
    sign = jnp.where(lax.broadcasted_iota(jnp.int32, (1, LANES), 1) < RET_HEAD_DIM // 2, -1.0, 1.0)
    cos2 = cos_a * cos_b - sin_a * sin_b
    sin2 = sign * (sin_a * cos_b + cos_a * sin_b)

    def rope(blk):
        return blk * cos2 + pltpu.roll(blk, RET_HEAD_DIM // 2, 1) * sin2

    p = proj(4)
    for h in range(RET_HEADS):
        sl = slice(h * LANES, (h + 1) * LANES)
        rkt_ref[h] = (rope(p[:, sl]) * RET_HEAD_DIM ** -0.5).T.astype(BF16)
    q_scale = SB_HEAD_DIM ** -0.5 * LOG2E
    p = proj(0)
    for hp in range(SB_WIDTH // LANES):
        sl = slice(hp * LANES, (hp + 1) * LANES)
        sbq_ref[:, sl] = (_half_head_rms(p[:, sl]) * (qg_ref[...] * q_scale)).astype(BF16)
    p = proj(3)
    for h in range(RET_HEADS):
        sl = slice(h * LANES, (h + 1) * LANES)
        rq_ref[:, sl] = rope(p[:, sl]).astype(BF16)
    rg_ref[...] = _silu(proj(6)).astype(BF16)
    sbv_ref[...] = proj(2).astype(BF16)
    rv_ref[...] = proj(5).astype(BF16)


def _even_in(x2, g, w_stack, layer, qg, kg, rope_tables, batch, seq, sb_blk):
    t, d = x2.shape
    tm = min(ROW_TILE, seq)
    tps = seq // tm
    tile_start = pl.BlockSpec((None, 1, LANES), lambda i: (i % tps, 0, 0))
    row = lambda i: (i, 0)
    pairs = SB_WIDTH // LANES
    flat = jax.ShapeDtypeStruct((t, SB_WIDTH), BF16)
    flat_spec = pl.BlockSpec((tm, SB_WIDTH), row)
    sbkt = jax.ShapeDtypeStruct((batch, pairs, seq // sb_blk, LANES, sb_blk), BF16)
    sbkt_spec = pl.BlockSpec((None, pairs, tm // sb_blk, LANES, sb_blk),
                             lambda i: (i // tps, 0, i % tps, 0, 0))
    rkt = jax.ShapeDtypeStruct((batch, RET_HEADS, RET_HEAD_DIM, seq), BF16)
    rkt_spec = pl.BlockSpec((None, RET_HEADS, RET_HEAD_DIM, tm), lambda i: (i // tps, 0, 0, i % tps))
    return pl.pallas_call(
        _even_in_kernel,
        grid=(t // tm,),
        in_specs=[
            pl.BlockSpec((tm, d), row),
            _const_spec((1, d)),
            _layer_spec(w_stack.shape, layer),
            _const_spec((1, LANES)),
            _const_spec((1, LANES)),
            tile_start,
            tile_start,
            _const_spec((tm, LANES)),
            _const_spec((tm, LANES)),
        ],
        out_specs=[flat_spec, sbkt_spec, flat_spec, flat_spec, rkt_spec, flat_spec, flat_spec],
        out_shape=[flat, sbkt, flat, flat, rkt, flat, flat],
        compiler_params=_params("parallel"),
        name="even_in",
    )(x2, g, w_stack, qg, kg, *rope_tables)


def _softplus2(z):
    return jnp.where(z > SOFTPLUS2_LINEAR_ABOVE, z, jnp.log2(1.0 + jnp.exp2(z)))


def _sb_kernel(q_ref, kt_ref, v_ref, tri_ref, qg_ref, kg_ref, o_ref, acc_ref, carry_ref):
    step = pl.program_id(2)
    blk = kt_ref.shape[-1]
    per_step = q_ref.shape[0] // blk
    z_bound = (SB_NORM_SLACK * SB_HEAD_DIM * SB_HEAD_DIM ** -0.5 * LOG2E
               * jnp.max(jnp.abs(qg_ref[...])) * jnp.max(jnp.abs(kg_ref[...])))
    dead_above = z_bound + EXP2_UNDERFLOW
    lane = lax.broadcasted_iota(jnp.int32, (blk, LANES), 1)
    lo = lane < SB_HEAD_DIM
    row_id = lax.broadcasted_iota(jnp.int32, (blk, blk), 0)
    col_id = lax.broadcasted_iota(jnp.int32, (blk, blk), 1)
    below_diag = col_id < row_id
    tri = tri_ref[...]
    heads = range(2)

    def split_heads(q):
        zero = jnp.zeros_like(q)
        return (jnp.where(lo, q, zero), jnp.where(lo, zero, q))

    def key_blocks(j_a):
        j_b = jnp.maximum(j_a - 1, 0)
        return j_a > 0, j_b

    def values(j):
        return v_ref[pl.ds(pl.multiple_of(j * blk, blk), blk), :]

    def diagonal_pairs(blocks, b_rows):
        chains = [(n, h) for n, _ in blocks for h in heads]
        j_as = {n: j_a for n, j_a in blocks}
        top = slice(0, b_rows)
        has_b, kt_a, kt_b, v_a, v_b = {}, {}, {}, {}, {}
        for n, j_a in blocks:
            has_b[n], j_b = key_blocks(j_a)
            kt_a[n] = kt_ref[j_a]
            kt_b[n] = kt_ref[j_b]
            v_a[n] = values(j_a)
            v_b[n] = values(j_b)
        z_a = {(n, h): jnp.dot(q_heads[n][h], kt_a[n], preferred_element_type=F32) for n, h in chains}
        z_b = {(n, h): jnp.dot(q_heads[n][h][top], kt_b[n], preferred_element_type=F32)
               for n, h in chains}
        z_a = {c: jnp.where(below_diag, z, MASKED_SCORE) for c, z in z_a.items()}
        sp_a = {c: _softplus2(z) for c, z in z_a.items()}
        sp_b = {c: _softplus2(z) for c, z in z_b.items()}
        cs_a = {c: jnp.dot(sp.astype(BF16), tri, preferred_element_type=F32) for c, sp in sp_a.items()}
        cs_b = {c: jnp.dot(sp.astype(BF16), tri, preferred_element_type=F32) for c, sp in sp_b.items()}
        tot_a, tot_b, w_a, w_b = {}, {}, {}, {}
        for n, h in chains:
            c = (n, h)
            tot_a[c] = jnp.sum(sp_a[c], axis=-1, keepdims=True)
            tot_b[c] = jnp.where(has_b[n], jnp.sum(sp_b[c], axis=-1, keepdims=True), 0.0)
            before_b = jnp.where(has_b[n], tot_a[c][top], -MASKED_SCORE)
            w_a[c] = jnp.exp2(z_a[c] - cs_a[c])
            w_b[c] = jnp.exp2(z_b[c] - (cs_b[c] + before_b))
        for n, h in chains:
            c = (n, h)
            pv_a = jnp.dot(w_a[c].astype(BF16), v_a[n], preferred_element_type=F32)
            pv_b = jnp.dot(w_b[c].astype(BF16), v_b[n], preferred_element_type=F32)
            acc_ref[n, h, top] = pv_a[top] + pv_b
            carry_ref[n, h, top] = tot_a[c][top] + tot_b[c]
            if b_rows < blk:
                acc_ref[n, h, b_rows:] = pv_a[b_rows:]
                carry_ref[n, h, b_rows:] = tot_a[c][b_rows:]

    first = step * per_step
    q_heads = [split_heads(q_ref[n * blk:(n + 1) * blk, :]) for n in range(per_step)]
    diagonal_pairs([(n, first + n) for n in range(per_step)], min(SB_RECENT_ROWS, blk))

    def older_pair(n, j_a):
        has_b, j_b = key_blocks(j_a)
        kt_a, kt_b = kt_ref[j_a], kt_ref[j_b]
        v_a, v_b = values(j_a), values(j_b)
        z_a = [jnp.dot(q_heads[n][h], kt_a, preferred_element_type=F32) for h in heads]
        z_b = [jnp.dot(q_heads[n][h], kt_b, preferred_element_type=F32) for h in heads]
        sp_a = [_softplus2(z) for z in z_a]
        sp_b = [_softplus2(z) for z in z_b]
        cs_a = [jnp.dot(sp.astype(BF16), tri, preferred_element_type=F32) for sp in sp_a]
        cs_b = [jnp.dot(sp.astype(BF16), tri, preferred_element_type=F32) for sp in sp_b]
        for h in heads:
            carry = carry_ref[n, h]
            after_a = carry + jnp.sum(sp_a[h], axis=-1, keepdims=True)
            tot_b = jnp.where(has_b, jnp.sum(sp_b[h], axis=-1, keepdims=True), 0.0)
            before_b = jnp.where(has_b, after_a, -MASKED_SCORE)
            w_a = jnp.exp2(z_a[h] - (cs_a[h] + carry))
            w_b = jnp.exp2(z_b[h] - (cs_b[h] + before_b))
            acc_ref[n, h] += (jnp.dot(w_a.astype(BF16), v_a, preferred_element_type=F32)
                              + jnp.dot(w_b.astype(BF16), v_b, preferred_element_type=F32))
            carry_ref[n, h] = after_a + tot_b

    def live(n):
        return jnp.minimum(jnp.min(carry_ref[n, 0]), jnp.min(carry_ref[n, 1])) <= dead_above

    def write_out(n):
        o_ref[n * blk:(n + 1) * blk, :] = jnp.where(lo, acc_ref[n, 0], acc_ref[n, 1]).astype(o_ref.dtype)

    least = carry_ref[0, 0]
    for n in range(per_step):
        for h in heads:
            least = jnp.minimum(least, carry_ref[n, h])
    any_live = jnp.min(least) <= dead_above

    @pl.when(jnp.logical_not(any_live))
    def _():
        for n in range(per_step):
            write_out(n)

    @pl.when(any_live)
    def _():
        for n in range(per_step):
            diagonal_pairs([(n, first + n)], blk)

            def body(c, n=n):
                older_pair(n, c[0])
                return c[0] - 2, live(n)

            lax.while_loop(lambda c: jnp.logical_and(c[0] >= 0, c[1]), body, (first + n - 2, live(n)))
            write_out(n)


def _sb_attention(q, kt, v, tri, qg, kg):
    b, s, width = q.shape
    blk = kt.shape[-1]
    n_blocks = s // blk
    per_step = min(SB_QBLOCKS_PER_STEP, n_blocks)
    assert n_blocks % per_step == 0
    rows = per_step * blk
    return pl.pallas_call(
        _sb_kernel,
        grid=(b, width // LANES, n_blocks // per_step),
        in_specs=[
            pl.BlockSpec((None, rows, LANES), lambda bi, hp, i: (bi, i, hp)),
            pl.BlockSpec((None, None, n_blocks, LANES, blk), lambda bi, hp, i: (bi, hp, 0, 0, 0)),
            pl.BlockSpec((None, s, LANES), lambda bi, hp, i: (bi, 0, hp)),
            _const_spec(tri.shape),
            _const_spec((1, LANES)),
            _const_spec((1, LANES)),
        ],
        out_specs=pl.BlockSpec((None, rows, LANES), lambda bi, hp, i: (bi, i, hp)),
        out_shape=jax.ShapeDtypeStruct((b, s, width), BF16),
        scratch_shapes=[pltpu.VMEM((per_step, 2, blk, LANES), F32),
                        pltpu.VMEM((per_step, 2, blk, 1), F32)],
        compiler_params=_params("parallel", "parallel", "arbitrary"),
        name="sb_attention",
    )(q, kt, v, tri, qg, kg)


def _ret_kernel(q_ref, kt_ref, v_ref, g_ref, gain_ref, decay_ref, zeta_ref, xi_ref, gc_ref,
                o_ref, state_ref):
    @pl.when(pl.program_id(2) == 0)
    def _():
        state_ref[...] = jnp.zeros_like(state_ref)

    c = RET_CHUNK
    d = RET_HEAD_DIM
    span = decay_ref.shape[-1]
    rows = q_ref.shape[0]
    gain = gain_ref[...]
    for hh in range(state_ref.shape[0]):
        lanes = slice(hh * d, (hh + 1) * d)
        decay = decay_ref[hh]
        zeta = zeta_ref[hh]
        xi = xi_ref[hh]
        g_chunk = gc_ref[hh]
        q = [q_ref[m * span:(m + 1) * span, lanes] for m in range(rows // span)]
        kt = [kt_ref[hh, :, m * span:(m + 1) * span] for m in range(rows // span)]
        v = [v_ref[m * span:(m + 1) * span, lanes] for m in range(rows // span)]
        scores = [jnp.dot(q[m], kt[m], preferred_element_type=F32) * decay
                  for m in range(rows // span)]
        inner = [jnp.dot(scores[m].astype(BF16), v[m], preferred_element_type=F32)
                 for m in range(rows // span)]
        contrib = []
        for n in range(rows // c):
            m, sub = divmod(n * c, span)
            kz = (kt[m][:, sub:sub + c].astype(F32) * zeta).astype(BF16)
            contrib.append(jnp.dot(kz, v[m][sub:sub + c], preferred_element_type=F32))
        state = state_ref[hh]
        for n in range(rows // c):
            m, sub = divmod(n * c, span)
            cross = jnp.dot(q[m][sub:sub + c], state.astype(BF16), preferred_element_type=F32) * xi
            state = g_chunk * state + contrib[n]
            ret = _rms_rows(inner[m][sub:sub + c] + cross, gain)
            gate = g_ref[n * c:(n + 1) * c, lanes].astype(F32)
            o_ref[n * c:(n + 1) * c, lanes] = (ret * gate).astype(o_ref.dtype)
        state_ref[hh] = state


def _retention(q, kt, v, g, gain, decay, zeta, xi, gc):
    b, s, width = q.shape
    rows = min(RET_GROUP, s)
    span = decay.shape[-1]
    hps = RET_HEADS_PER_STEP
    blk = pl.BlockSpec((None, rows, hps * RET_HEAD_DIM), lambda bi, h, n: (bi, n, h))
    head = lambda shape: pl.BlockSpec((hps,) + shape, lambda bi, h, n: (h, 0, 0))
    return pl.pallas_call(
        _ret_kernel,
        grid=(b, RET_HEADS // hps, s // rows),
        in_specs=[blk,
                  pl.BlockSpec((None, hps, RET_HEAD_DIM, rows), lambda bi, h, n: (bi, h, 0, n)),
                  blk, blk, _const_spec((1, RET_HEAD_DIM)),
                  head((span, span)), head((1, RET_CHUNK)), head((RET_CHUNK, 1)), head((1, 1))],
        out_specs=blk,
        out_shape=jax.ShapeDtypeStruct((b, s, width), BF16),
        scratch_shapes=[pltpu.VMEM((hps, RET_HEAD_DIM, RET_HEAD_DIM), F32)],
        compiler_params=_params("parallel", "parallel", "arbitrary"),
        name="retention",
    )(q, kt, v, g, gain, decay, zeta, xi, gc)


def _ffn_kernel(*refs, n_mix, tiles_per_seq):
    h_ref = refs[0]
    mix_refs = refs[1:1 + n_mix]
    (wo_ref, g_ref, wup_ref, cw_ref, cb_ref, wd_ref, o_ref,
     ubuf_ref, prev_ref, gate_ref) = refs[1 + n_mix:]
    tm = h_ref.shape[0]
    d_ff = wd_ref.shape[0]
    halo = SUBLANES

    @pl.when(pl.program_id(0) % tiles_per_seq == 0)
    def _():
        prev_ref[...] = jnp.zeros_like(prev_ref)

    h = h_ref[...]
    row0 = 0
    for a_ref in mix_refs:
        width = a_ref.shape[1]
        h = h + jnp.dot(a_ref[...], wo_ref[row0:row0 + width, :], preferred_element_type=F32)
        row0 += width
    hn = _rms_rows(h, g_ref[...]).astype(BF16)
    for c0 in range(0, d_ff, FF_CHUNK):
        cols = slice(c0, c0 + FF_CHUNK)
        u = jnp.dot(hn, wup_ref[:, cols], preferred_element_type=F32)
        v = jnp.dot(hn, wup_ref[:, d_ff + c0:d_ff + c0 + FF_CHUNK], preferred_element_type=F32)
        ubuf_ref[0:halo, :] = prev_ref[:, cols]
        ubuf_ref[halo:halo + tm, :] = u
        prev_ref[:, cols] = u[tm - halo:, :]
        last_tap = CONV_WIDTH - 1
        uc = cb_ref[:, cols] + cw_ref[last_tap:CONV_WIDTH, cols] * u
        for tap in range(last_tap):
            back = last_tap - tap
            uc = uc + cw_ref[tap:tap + 1, cols] * ubuf_ref[halo - back:halo - back + tm, :]
        gate_ref[:, cols] = (_silu(uc) * v).astype(BF16)
    o_ref[...] = h + jnp.dot(gate_ref[...], wd_ref[...], preferred_element_type=F32)


def _ffn(h2, mix, wo_stack, wo_layer, g, wup_stack, cw, cb, wd_stack, layer, seq):
    t, d = h2.shape
    d_ff = wd_stack.shape[1]
    tm = min(FFN_ROW_TILE, seq)
    row = lambda i: (i, 0)
    return pl.pallas_call(
        functools.partial(_ffn_kernel, n_mix=len(mix), tiles_per_seq=seq // tm),
        grid=(t // tm,),
        in_specs=([pl.BlockSpec((tm, d), row)]
                  + [pl.BlockSpec((tm, a.shape[1]), row) for a in mix]
                  + [_layer_spec(wo_stack.shape, wo_layer), _const_spec((1, d)),
                     _layer_spec(wup_stack.shape, layer), _const_spec(cw.shape),
                     _const_spec(cb.shape), _layer_spec(wd_stack.shape, layer)]),
        out_specs=pl.BlockSpec((tm, d), row),
        out_shape=jax.ShapeDtypeStruct((t, d), F32),
        scratch_shapes=[pltpu.VMEM((tm + SUBLANES, FF_CHUNK), F32),
                        pltpu.VMEM((SUBLANES, d_ff), F32),
                        pltpu.VMEM((tm, d_ff), BF16)],
        compiler_params=_params("arbitrary"),
        name="conv_ffn",
    )(h2, *mix, wo_stack, g, wup_stack, cw, cb, wd_stack)


def _log_sigmoid(x):
    return jnp.minimum(x, 0.0) - jnp.log(1.0 + jnp.exp(-jnp.abs(x)))


def _odd_in_kernel(x_ref, g_ref, w_ref, wa_ref, walpha_ref, balpha_ref, tri_ref,
                   qt_ref, ktt_ref, kdt_ref, v_ref, r_ref, dec_ref):
    tm = x_ref.shape[0]
    span = tri_ref.shape[0]
    hn = _rms_rows(x_ref[...], g_ref[...]).astype(BF16)

    def proj(c0, width):
        return jnp.dot(hn, w_ref[:, c0:c0 + width], preferred_element_type=F32)

    ga = jnp.dot(hn, wa_ref[...], preferred_element_type=F32)
    v_ref[...] = proj(2 * GLA_K_WIDTH, GLA_V_WIDTH).astype(BF16)
    r_ref[...] = _silu(proj(2 * GLA_K_WIDTH + GLA_V_WIDTH, GLA_V_WIDTH)).astype(BF16)
    ga_hi = ga.astype(BF16)
    ga_lo = (ga - ga_hi.astype(F32)).astype(BF16)
    lane = lax.broadcasted_iota(jnp.int32, ga.shape, 1)
    middle = jnp.logical_and(lane >= GLA_GATE_RANK, lane < 2 * GLA_GATE_RANK)
    pre = jnp.dot(jnp.where(middle, ga_lo, ga_hi), walpha_ref[...],
                  preferred_element_type=F32) + balpha_ref[...]
    la = _log_sigmoid(pre) * (LOG2E / GLA_GATE_TAU)
    la_hi = la.astype(BF16)
    la_lo = (la - la_hi.astype(F32)).astype(BF16)
    tri = tri_ref[...]
    cum = jnp.concatenate(
        [jnp.dot(tri, la_hi[r0:r0 + span], preferred_element_type=F32)
         + jnp.dot(tri, la_lo[r0:r0 + span], preferred_element_type=F32)
         for r0 in range(0, tm, span)], axis=0)
    chunks = tm // GLA_CHUNK
    last = cum.reshape(chunks, GLA_CHUNK, GLA_K_WIDTH)[:, GLA_CHUNK - 1:GLA_CHUNK, :]
    dec_ref[...] = jnp.exp2(last).reshape(chunks, GLA_K_WIDTH)
    to_end = (jnp.broadcast_to(last, (chunks, GLA_CHUNK, GLA_K_WIDTH)).reshape(tm, GLA_K_WIDTH)
              - cum)

    k = proj(GLA_K_WIDTH, GLA_K_WIDTH)
    kt = k * jnp.exp2(-cum)
    kd = k * jnp.exp2(to_end)
    for h in range(GLA_HEADS):
        sl = slice(h * GLA_DK, (h + 1) * GLA_DK)
        ktt_ref[h] = kt[:, sl].T.astype(BF16)
        kdt_ref[h] = kd[:, sl].T.astype(BF16)
    qt_ref[...] = (proj(0, GLA_K_WIDTH) * GLA_DK ** -0.5 * jnp.exp2(cum)).astype(BF16)


def _odd_in(x2, g, w_stack, layer, wa, walpha, balpha, tri, batch, seq):
    t, d = x2.shape
    tm = min(ROW_TILE, seq)
    tps = seq // tm
    row = lambda i: (i, 0)
    kt_shape = jax.ShapeDtypeStruct((batch, GLA_HEADS, GLA_DK, seq), BF16)
    kt_spec = pl.BlockSpec((None, GLA_HEADS, GLA_DK, tm), lambda i: (i // tps, 0, 0, i % tps))
    vout = jax.ShapeDtypeStruct((t, GLA_V_WIDTH), BF16)
    return pl.pallas_call(
        _odd_in_kernel,
        grid=(t // tm,),
        in_specs=[pl.BlockSpec((tm, d), row), _const_spec((1, d)), _layer_spec(w_stack.shape, layer),
                  _const_spec(wa.shape), _const_spec(walpha.shape), _const_spec(balpha.shape),
                  _const_spec(tri.shape)],
        out_specs=[pl.BlockSpec((tm, GLA_K_WIDTH), row), kt_spec, kt_spec,
                   pl.BlockSpec((tm, GLA_V_WIDTH), row), pl.BlockSpec((tm, GLA_V_WIDTH), row),
                   pl.BlockSpec((tm // GLA_CHUNK, GLA_K_WIDTH), row)],
        out_shape=[jax.ShapeDtypeStruct((t, GLA_K_WIDTH), BF16), kt_shape, kt_shape, vout, vout,
                   jax.ShapeDtypeStruct((t // GLA_CHUNK, GLA_K_WIDTH), F32)],
        compiler_params=_params("parallel"),
        name="odd_in",
    )(x2, g, w_stack, wa, walpha, balpha, tri)


def _gla_kernel(qt_ref, ktt_ref, kdt_ref, v_ref, r_ref, dec_ref, gain_ref, o_ref, state_ref):
    @pl.when(pl.program_id(2) == 0)
    def _():
        state_ref[...] = jnp.zeros_like(state_ref)

    c = GLA_CHUNK
    rows = qt_ref.shape[0]
    span = min(GLA_SPAN, rows)
    ri = lax.broadcasted_iota(jnp.int32, (span, span), 0)
    ci = lax.broadcasted_iota(jnp.int32, (span, span), 1)
    causal = jnp.logical_and(ci <= ri, ci // c == ri // c)
    col_chunk = lax.broadcasted_iota(jnp.int32, (1, span), 1) // c
    gain = gain_ref[...]
    for hh in range(state_ref.shape[0]):
        klanes = slice(hh * GLA_DK, (hh + 1) * GLA_DK)
        vlanes = slice(hh * GLA_DV, (hh + 1) * GLA_DV)
        dec_t = dec_ref[:, klanes].T
        spans = range(rows // span)
        qt = [qt_ref[m * span:(m + 1) * span, klanes] for m in spans]
        ktt = [ktt_ref[hh, :, m * span:(m + 1) * span] for m in spans]
        kdt = [kdt_ref[hh, :, m * span:(m + 1) * span] for m in spans]
        v = [v_ref[m * span:(m + 1) * span, vlanes] for m in spans]
        a = [jnp.where(causal, jnp.dot(qt[m], ktt[m], preferred_element_type=F32), 0.0)
             for m in spans]
        intra = [jnp.dot(a[m].astype(BF16), v[m], preferred_element_type=F32) for m in spans]
        contrib = []
        for n in range(rows // c):
            m, sub = divmod(n * c, span)
            kd_n = jnp.where(col_chunk == sub // c, kdt[m], jnp.zeros_like(kdt[m]))
            contrib.append(jnp.dot(kd_n, v[m], preferred_element_type=F32))
        state = state_ref[hh]
        for n in range(rows // c):
            m, sub = divmod(n * c, span)
            inter = jnp.dot(qt[m][sub:sub + c], state.astype(BF16), preferred_element_type=F32)
            state = dec_t[:, n:n + 1] * state + contrib[n]
            o = _rms_rows(intra[m][sub:sub + c] + inter, gain)
            gate = r_ref[n * c:(n + 1) * c, vlanes].astype(F32)
            o_ref[n * c:(n + 1) * c, vlanes] = (o * gate).astype(o_ref.dtype)
        state_ref[hh] = state


def _gla(qt, ktt, kdt, v, r, dec, gain):
    b, s, kw = qt.shape
    vw = v.shape[-1]
    rows = min(GLA_GROUP, s)
    hps = GLA_HEADS_PER_STEP
    chunks = rows // GLA_CHUNK
    dec = dec.reshape(b, s // rows, chunks, kw)
    qblk = pl.BlockSpec((None, rows, hps * GLA_DK), lambda bi, h, n: (bi, n, h))
    tblk = pl.BlockSpec((None, hps, GLA_DK, rows), lambda bi, h, n: (bi, h, 0, n))
    vblk = pl.BlockSpec((None, rows, hps * GLA_DV), lambda bi, h, n: (bi, n, h))
    dblk = pl.BlockSpec((None, None, chunks, hps * GLA_DK), lambda bi, h, n: (bi, n, 0, h))
    return pl.pallas_call(
        _gla_kernel,
        grid=(b, GLA_HEADS // hps, s // rows),
        in_specs=[qblk, tblk, tblk, vblk, vblk, dblk, _const_spec((1, GLA_DV))],
        out_specs=vblk,
        out_shape=jax.ShapeDtypeStruct((b, s, vw), BF16),
        scratch_shapes=[pltpu.VMEM((hps, GLA_DK, GLA_DV), F32)],
        compiler_params=_params("parallel", "parallel", "arbitrary"),
        name="gla",
    )(qt, ktt, kdt, v, r, dec, gain)


def _rope_tables(seq, tile):
    half = RET_HEAD_DIM // 2
    inv = ROPE_BASE ** (-np.arange(half, dtype=np.float64) / half)

    def both(pos):
        ang = pos[:, None] * inv[None, :]
        dup = lambda t: np.concatenate([t, t], axis=-1).astype(np.float32)
        return dup(np.cos(ang)), dup(np.sin(ang))

    cos_a, sin_a = both(np.arange(0, seq, tile, dtype=np.float64))
    cos_b, sin_b = both(np.arange(tile, dtype=np.float64))
    return (jnp.asarray(cos_a[:, None, :]), jnp.asarray(sin_a[:, None, :]),
            jnp.asarray(cos_b), jnp.asarray(sin_b))


def _retention_tables(span):
    c = RET_CHUNK
    lg = np.log(1.0 - np.exp2(-5.0 - np.arange(RET_HEADS, dtype=np.float64)))[:, None]
    pos = np.arange(span, dtype=np.float64)
    diff = pos[:, None] - pos[None, :]
    same = (pos[:, None] // c) == (pos[None, :] // c)
    decay = np.where((diff >= 0) & same, np.exp(lg[:, :, None] * np.maximum(diff, 0.0)), 0.0)
    pos = np.arange(c, dtype=np.float64)
    zeta = np.exp(lg * (c - 1 - pos))[:, None, :]
    xi = np.exp(lg * (pos + 1.0))[:, :, None]
    g_chunk = np.exp(lg * c)[:, :, None]
    return tuple(jnp.asarray(t.astype(np.float32)) for t in (decay, zeta, xi, g_chunk))


def _suffix_tri(n):
    idx = np.arange(n)
    return jnp.asarray(idx[:, None] >= idx[None, :], dtype=BF16)


def _chunk_prefix_tri(rows, chunk):
    idx = np.arange(rows)
    same = (idx[:, None] // chunk) == (idx[None, :] // chunk)
    return jnp.asarray(same & (idx[None, :] <= idx[:, None]), dtype=BF16)


def _gate_operands(w_gate_in, w_alpha):
    rank = GLA_GATE_RANK
    wa = jnp.pad(jnp.tile(w_gate_in, (1, 3)), ((0, 0), (0, LANES - 3 * rank))).astype(BF16)
    w_hi = w_alpha.astype(BF16)
    w_lo = (w_alpha - w_hi.astype(F32)).astype(BF16)
    walpha = jnp.pad(jnp.concatenate([w_hi, w_hi, w_lo], axis=0), ((0, LANES - 3 * rank), (0, 0)))
    return wa, walpha


def kernel(x, mix_norm_g, even_w_in, sb_q_gain, sb_k_gain, ret_out_gain, even_w_out,
           odd_w_in, gla_w_alpha, gla_b_alpha, gla_out_gain, odd_w_out,
           ffn_norm_g, ffn_w_up, ffn_conv_w, ffn_conv_b, ffn_w_down):
    b, s, d = x.shape
    t = b * s
    depth = mix_norm_g.shape[0]
    h = x.reshape(t, d)

    rope_tables = _rope_tables(s, min(ROW_TILE, s))
    decay, zeta, xi, g_chunk = _retention_tables(min(RET_SPAN, s))
    sb_blk = min(SB_BLOCK, s)
    sb_tri = _suffix_tri(sb_blk)
    gla_tri = _chunk_prefix_tri(min(GLA_SPAN, s), GLA_CHUNK)

    even_w_in, even_w_out, odd_w_in_b, odd_w_out, ffn_w_up, ffn_w_down = (
        w.astype(BF16) for w in (even_w_in, even_w_out, odd_w_in, odd_w_out, ffn_w_up, ffn_w_down))

    for layer in range(depth):
        g_mix = mix_norm_g[layer][None, :]
        if layer % 2 == 0:
            e = layer // 2
            qg = jnp.tile(sb_q_gain[e], 2)[None, :]
            kg = jnp.tile(sb_k_gain[e], 2)[None, :]
            sbq, sbkt, sbv, rq, rkt, rv, rg = _even_in(
                h, g_mix, even_w_in, e, qg, kg, rope_tables, b, s, sb_blk)
            out_a = _sb_attention(sbq.reshape(b, s, SB_WIDTH), sbkt, sbv.reshape(b, s, SB_WIDTH),
                                  sb_tri, qg, kg)
            shp = (b, s, RET_WIDTH)
            out_b = _retention(rq.reshape(shp), rkt, rv.reshape(shp), rg.reshape(shp),
                               ret_out_gain[e][None, :], decay, zeta, xi, g_chunk)
            mix = (out_a.reshape(t, SB_WIDTH), out_b.reshape(t, RET_WIDTH))
            w_out, w_out_layer = even_w_out, e
        else:
            o = layer // 2
            n_main = 2 * GLA_K_WIDTH + 2 * GLA_V_WIDTH
            wa, walpha = _gate_operands(odd_w_in[o, :, n_main:], gla_w_alpha[o])
            qt, ktt, kdt, gv, gr, dec = _odd_in(h, g_mix, odd_w_in_b, o, wa, walpha,
                                                gla_b_alpha[o][None, :], gla_tri, b, s)
            og = _gla(qt.reshape(b, s, GLA_K_WIDTH), ktt, kdt,
                      gv.reshape(b, s, GLA_V_WIDTH), gr.reshape(b, s, GLA_V_WIDTH),
                      dec.reshape(b, s // GLA_CHUNK, GLA_K_WIDTH), gla_out_gain[o][None, :])
            mix = (og.reshape(t, GLA_V_WIDTH),)
            w_out, w_out_layer = odd_w_out, o
        h = _ffn(h, mix, w_out, w_out_layer, ffn_norm_g[layer][None, :], ffn_w_up,
                 ffn_conv_w[layer], ffn_conv_b[layer][None, :], ffn_w_down, layer, s)
    return h.reshape(b, s, d)
```

```python
import functools

import numpy as np

import jax
import jax.numpy as jnp
from jax import lax
from jax.experimental import pallas as pl
from jax.experimental.pallas import tpu as pltpu

F32 = jnp.float32
BF16 = jnp.bfloat16

EPS = 1e-6
LOG2E = 1.4426950408889634
ROPE_BASE = 10000.0
EXP2_UNDERFLOW = 150.0
SB_NORM_SLACK = 1.05
SOFTPLUS2_LINEAR_ABOVE = 32.0
MASKED_SCORE = -1e30

SB_HEADS = 8
SB_HEAD_DIM = 64
SB_WIDTH = SB_HEADS * SB_HEAD_DIM
RET_HEADS = 4
RET_HEAD_DIM = 128
RET_WIDTH = RET_HEADS * RET_HEAD_DIM
RET_CHUNK = 128
GLA_HEADS = 4
GLA_DK = 128
GLA_DV = 256
GLA_K_WIDTH = GLA_HEADS * GLA_DK
GLA_V_WIDTH = GLA_HEADS * GLA_DV
GLA_GATE_RANK = 16
GLA_GATE_TAU = 16.0
GLA_CHUNK = 64
CONV_WIDTH = 3

LANES = 128
SUBLANES = 8
MXU_DIM = 256
VMEM_LIMIT = 56 * 1024 * 1024

ROW_TILE = 1024
FFN_ROW_TILE = 1024
SB_BLOCK = MXU_DIM
SB_QBLOCKS_PER_STEP = 4
SB_RECENT_ROWS = 176
RET_HEADS_PER_STEP = 4
RET_GROUP = 1024
RET_SPAN = 2 * RET_CHUNK
GLA_GROUP = 512
GLA_HEADS_PER_STEP = 4
GLA_SPAN = 4 * GLA_CHUNK
FF_CHUNK = 256


def _params(*sem):
    return pltpu.CompilerParams(dimension_semantics=sem, vmem_limit_bytes=VMEM_LIMIT)


def _const_spec(shape):
    n = len(shape)
    return pl.BlockSpec(shape, lambda *_: (0,) * n)


def _layer_spec(stacked_shape, layer):
    zeros = (0,) * (len(stacked_shape) - 1)
    return pl.BlockSpec((None,) + tuple(stacked_shape[1:]), lambda *_: (layer,) + zeros,
                        pipeline_mode=pl.Buffered(1))


def _rms_rows(x, g):
    ms = jnp.mean(x * x, axis=-1, keepdims=True)
    return x * lax.rsqrt(ms + EPS) * g


def _silu(x):
    return x / (1.0 + jnp.exp2(x * -LOG2E))


def _half_head_rms(blk):
    sq = blk * blk
    lane = lax.broadcasted_iota(jnp.int32, blk.shape, 1)
    lo = lane < SB_HEAD_DIM
    s_lo = jnp.sum(jnp.where(lo, sq, 0.0), axis=-1, keepdims=True)
    s_hi = jnp.sum(jnp.where(lo, 0.0, sq), axis=-1, keepdims=True)
    ms = jnp.where(lo, s_lo, s_hi) * (1.0 / SB_HEAD_DIM)
    return blk * lax.rsqrt(ms + EPS)


def _even_in_kernel(x_ref, g_ref, w_ref, qg_ref, kg_ref, ca_ref, sa_ref, cb_ref, sb_ref,
                    sbq_ref, sbkt_ref, sbv_ref, rq_ref, rkt_ref, rv_ref, rg_ref):
    tm = x_ref.shape[0]
    hn = _rms_rows(x_ref[...], g_ref[...]).astype(BF16)

    def proj(seg):
        return jnp.dot(hn, w_ref[:, seg * SB_WIDTH:(seg + 1) * SB_WIDTH],
                       preferred_element_type=F32)

    p = proj(1)
    blk = sbkt_ref.shape[-1]
    for hp in range(SB_WIDTH // LANES):
        sl = slice(hp * LANES, (hp + 1) * LANES)
        kt = (_half_head_rms(p[:, sl]) * kg_ref[...]).T
        for c in range(tm // blk):
            sbkt_ref[hp, c] = kt[:, c * blk:(c + 1) * blk].astype(BF16)

    cos_a, sin_a, cos_b, sin_b = ca_ref[...], sa_ref[...], cb_ref[...], sb_ref[...]
    sign = jnp.where(lax.broadcasted_iota(jnp.int32, (1, LANES), 1) < RET_HEAD_DIM // 2, -1.0, 1.0)
    cos2 = cos_a * cos_b - sin_a * sin_b
    sin2 = sign * (sin_a * cos_b + cos_a * sin_b)

    def rope(blk):
        return blk * cos2 + pltpu.roll(blk, RET_HEAD_DIM // 2, 1) * sin2

    p = proj(4)
    for h in range(RET_HEADS):
        sl = slice(h * LANES, (h + 1) * LANES)
        rkt_ref[h] = (rope(p[:, sl]) * RET_HEAD_DIM ** -0.5).T.astype(BF16)
    q_scale = SB_HEAD_DIM ** -0.5 * LOG2E
    p = proj(0)
    for hp in range(SB_WIDTH // LANES):
        sl = slice(hp * LANES, (hp + 1) * LANES)
        sbq_ref[:, sl] = (_half_head_rms(p[:, sl]) * (qg_ref[...] * q_scale)).astype(BF16)
    p = proj(3)
    for h in range(RET_HEADS):
        sl = slice(h * LANES, (h + 1) * LANES)
        rq_ref[:, sl] = rope(p[:, sl]).astype(BF16)
    rg_ref[...] = _silu(proj(6)).astype(BF16)
    sbv_ref[...] = proj(2).astype(BF16)
    rv_ref[...] = proj(5).astype(BF16)


def _even_in(x2, g, w_stack, layer, qg, kg, rope_tables, batch, seq, sb_blk):
    t, d = x2.shape
    tm = min(ROW_TILE, seq)
    tps = seq // tm
    tile_start = pl.BlockSpec((None, 1, LANES), lambda i: (i % tps, 0, 0))
    row = lambda i: (i, 0)
    pairs = SB_WIDTH // LANES
    flat = jax.ShapeDtypeStruct((t, SB_WIDTH), BF16)
    flat_spec = pl.BlockSpec((tm, SB_WIDTH), row)
    sbkt = jax.ShapeDtypeStruct((batch, pairs, seq // sb_blk, LANES, sb_blk), BF16)
    sbkt_spec = pl.BlockSpec((None, pairs, tm // sb_blk, LANES, sb_blk),
                             lambda i: (i // tps, 0, i % tps, 0, 0))
    rkt = jax.ShapeDtypeStruct((batch, RET_HEADS, RET_HEAD_DIM, seq), BF16)
    rkt_spec = pl.BlockSpec((None, RET_HEADS, RET_HEAD_DIM, tm), lambda i: (i // tps, 0, 0, i % tps))
    return pl.pallas_call(
        _even_in_kernel,
        grid=(t // tm,),
        in_specs=[
            pl.BlockSpec((tm, d), row),
            _const_spec((1, d)),
            _layer_spec(w_stack.shape, layer),
            _const_spec((1, LANES)),
            _const_spec((1, LANES)),
            tile_start,
            tile_start,
            _const_spec((tm, LANES)),
            _const_spec((tm, LANES)),
        ],
        out_specs=[flat_spec, sbkt_spec, flat_spec, flat_spec, rkt_spec, flat_spec, flat_spec],
        out_shape=[flat, sbkt, flat, flat, rkt, flat, flat],
        compiler_params=_params("parallel"),
        name="even_in",
    )(x2, g, w_stack, qg, kg, *rope_tables)


def _softplus2(z):
    return jnp.where(z > SOFTPLUS2_LINEAR_ABOVE, z, jnp.log2(1.0 + jnp.exp2(z)))


def _sb_kernel(q_ref, kt_ref, v_ref, tri_ref, qg_ref, kg_ref, o_ref, acc_ref, carry_ref):
    step = pl.program_id(2)
    blk = kt_ref.shape[-1]
    per_step = q_ref.shape[0] // blk
    z_bound = (SB_NORM_SLACK * SB_HEAD_DIM * SB_HEAD_DIM ** -0.5 * LOG2E
               * jnp.max(jnp.abs(qg_ref[...])) * jnp.max(jnp.abs(kg_ref[...])))
    dead_above = z_bound + EXP2_UNDERFLOW
    lane = lax.broadcasted_iota(jnp.int32, (blk, LANES), 1)
    lo = lane < SB_HEAD_DIM
    row_id = lax.broadcasted_iota(jnp.int32, (blk, blk), 0)
    col_id = lax.broadcasted_iota(jnp.int32, (blk, blk), 1)
    below_diag = col_id < row_id
    tri = tri_ref[...]
    heads = range(2)

    def split_heads(q):
        zero = jnp.zeros_like(q)
        return (jnp.where(lo, q, zero), jnp.where(lo, zero, q))

    def key_blocks(j_a):
        j_b = jnp.maximum(j_a - 1, 0)
        return j_a > 0, j_b

    def values(j):
        return v_ref[pl.ds(pl.multiple_of(j * blk, blk), blk), :]

    def write_out(n):
        o_ref[n * blk:(n + 1) * blk, :] = jnp.where(lo, acc_ref[n, 0], acc_ref[n, 1]).astype(o_ref.dtype)

    def diagonal_pairs(blocks, b_rows, write_outputs):
        chains = [(n, h) for n, _ in blocks for h in heads]
        j_as = {n: j_a for n, j_a in blocks}
        top = slice(0, b_rows)
        has_b, kt_a, kt_b, v_a, v_b = {}, {}, {}, {}, {}
        for n, j_a in blocks:
            has_b[n], j_b = key_blocks(j_a)
            kt_a[n] = kt_ref[j_a]
            kt_b[n] = kt_ref[j_b]
            v_a[n] = values(j_a)
            v_b[n] = values(j_b)
        z_a = {(n, h): jnp.dot(q_heads[n][h], kt_a[n], preferred_element_type=F32) for n, h in chains}
        z_b = {(n, h): jnp.dot(q_heads[n][h][top], kt_b[n], preferred_element_type=F32)
               for n, h in chains}
        z_a = {c: jnp.where(below_diag, z, MASKED_SCORE) for c, z in z_a.items()}
        sp_a = {c: _softplus2(z) for c, z in z_a.items()}
        sp_b = {c: _softplus2(z) for c, z in z_b.items()}
        cs_a = {c: jnp.dot(sp.astype(BF16), tri, preferred_element_type=F32) for c, sp in sp_a.items()}
        cs_b = {c: jnp.dot(sp.astype(BF16), tri, preferred_element_type=F32) for c, sp in sp_b.items()}
        tot_a, tot_b, w_a, w_b = {}, {}, {}, {}
        for n, h in chains:
            c = (n, h)
            tot_a[c] = jnp.sum(sp_a[c], axis=-1, keepdims=True)
            tot_b[c] = jnp.where(has_b[n], jnp.sum(sp_b[c], axis=-1, keepdims=True), 0.0)
            before_b = jnp.where(has_b[n], tot_a[c][top], -MASKED_SCORE)
            w_a[c] = jnp.exp2(z_a[c] - cs_a[c])
            w_b[c] = jnp.exp2(z_b[c] - (cs_b[c] + before_b))
        for n, h in chains:
            c = (n, h)
            pv_a = jnp.dot(w_a[c].astype(BF16), v_a[n], preferred_element_type=F32)
            pv_b = jnp.dot(w_b[c].astype(BF16), v_b[n], preferred_element_type=F32)
            acc_ref[n, h, top] = pv_a[top] + pv_b
            carry_ref[n, h, top] = tot_a[c][top] + tot_b[c]
            if b_rows < blk:
                acc_ref[n, h, b_rows:] = pv_a[b_rows:]
                carry_ref[n, h, b_rows:] = tot_a[c][b_rows:]
        if write_outputs:
            for n, _ in blocks:
                write_out(n)

    first = step * per_step
    q_heads = [split_heads(q_ref[n * blk:(n + 1) * blk, :]) for n in range(per_step)]
    diagonal_pairs([(n, first + n) for n in range(per_step)], min(SB_RECENT_ROWS, blk), True)

    def older_pair(n, j_a):
        has_b, j_b = key_blocks(j_a)
        kt_a, kt_b = kt_ref[j_a], kt_ref[j_b]
        v_a, v_b = values(j_a), values(j_b)
        z_a = [jnp.dot(q_heads[n][h], kt_a, preferred_element_type=F32) for h in heads]
        z_b = [jnp.dot(q_heads[n][h], kt_b, preferred_element_type=F32) for h in heads]
        sp_a = [_softplus2(z) for z in z_a]
        sp_b = [_softplus2(z) for z in z_b]
        cs_a = [jnp.dot(sp.astype(BF16), tri, preferred_element_type=F32) for sp in sp_a]
        cs_b = [jnp.dot(sp.astype(BF16), tri, preferred_element_type=F32) for sp in sp_b]
        for h in heads:
            carry = carry_ref[n, h]
            after_a = carry + jnp.sum(sp_a[h], axis=-1, keepdims=True)
            tot_b = jnp.where(has_b, jnp.sum(sp_b[h], axis=-1, keepdims=True), 0.0)
            before_b = jnp.where(has_b, after_a, -MASKED_SCORE)
            w_a = jnp.exp2(z_a[h] - (cs_a[h] + carry))
            w_b = jnp.exp2(z_b[h] - (cs_b[h] + before_b))
            acc_ref[n, h] += (jnp.dot(w_a.astype(BF16), v_a, preferred_element_type=F32)
                              + jnp.dot(w_b.astype(BF16), v_b, preferred_element_type=F32))
            carry_ref[n, h] = after_a + tot_b

    def live(n):
        return jnp.minimum(jnp.min(carry_ref[n, 0]), jnp.min(carry_ref[n, 1])) <= dead_above

    least = carry_ref[0, 0]
    for n in range(per_step):
        for h in heads:
            least = jnp.minimum(least, carry_ref[n, h])
    any_live = jnp.min(least) <= dead_above

    @pl.when(any_live)
    def _():
        for n in range(per_step):
            diagonal_pairs([(n, first + n)], blk, False)

            def body(c, n=n):
                older_pair(n, c[0])
                return c[0] - 2, live(n)

            lax.while_loop(lambda c: jnp.logical_and(c[0] >= 0, c[1]), body, (first + n - 2, live(n)))
            write_out(n)


def _sb_attention(q, kt, v, tri, qg, kg):
    b, s, width = q.shape
    blk = kt.shape[-1]
    n_blocks = s // blk
    per_step = min(SB_QBLOCKS_PER_STEP, n_blocks)
    assert n_blocks % per_step == 0
    rows = per_step * blk
    return pl.pallas_call(
        _sb_kernel,
        grid=(b, width // LANES, n_blocks // per_step),
        in_specs=[
            pl.BlockSpec((None, rows, LANES), lambda bi, hp, i: (bi, i, hp)),
            pl.BlockSpec((None, None, n_blocks, LANES, blk), lambda bi, hp, i: (bi, hp, 0, 0, 0)),
            pl.BlockSpec((None, s, LANES), lambda bi, hp, i: (bi, 0, hp)),
            _const_spec(tri.shape),
            _const_spec((1, LANES)),
            _const_spec((1, LANES)),
        ],
        out_specs=pl.BlockSpec((None, rows, LANES), lambda bi, hp, i: (bi, i, hp)),
        out_shape=jax.ShapeDtypeStruct((b, s, width), BF16),
        scratch_shapes=[pltpu.VMEM((per_step, 2, blk, LANES), F32),
                        pltpu.VMEM((per_step, 2, blk, 1), F32)],
        compiler_params=_params("parallel", "parallel", "arbitrary"),
        name="sb_attention",
    )(q, kt, v, tri, qg, kg)


def _ret_kernel(q_ref, kt_ref, v_ref, g_ref, gain_ref, decay_ref, zeta_ref, xi_ref, gc_ref,
                o_ref, state_ref):
    @pl.when(pl.program_id(2) == 0)
    def _():
        state_ref[...] = jnp.zeros_like(state_ref)

    c = RET_CHUNK
    d = RET_HEAD_DIM
    span = decay_ref.shape[-1]
    rows = q_ref.shape[0]
    gain = gain_ref[...]
    for hh in range(state_ref.shape[0]):
        lanes = slice(hh * d, (hh + 1) * d)
        decay = decay_ref[hh]
        zeta = zeta_ref[hh]
        xi = xi_ref[hh]
        g_chunk = gc_ref[hh]
        q = [q_ref[m * span:(m + 1) * span, lanes] for m in range(rows // span)]
        kt = [kt_ref[hh, :, m * span:(m + 1) * span] for m in range(rows // span)]
        v = [v_ref[m * span:(m + 1) * span, lanes] for m in range(rows // span)]
        scores = [jnp.dot(q[m], kt[m], preferred_element_type=F32) * decay
                  for m in range(rows // span)]
        inner = [jnp.dot(scores[m].astype(BF16), v[m], preferred_element_type=F32)
                 for m in range(rows // span)]
        contrib = []
        for n in range(rows // c):
            m, sub = divmod(n * c, span)
            kz = (kt[m][:, sub:sub + c].astype(F32) * zeta).astype(BF16)
            contrib.append(jnp.dot(kz, v[m][sub:sub + c], preferred_element_type=F32))
        state = state_ref[hh]
        for n in range(rows // c):
            m, sub = divmod(n * c, span)
            cross = jnp.dot(q[m][sub:sub + c], state.astype(BF16), preferred_element_type=F32) * xi
            state = g_chunk * state + contrib[n]
            ret = _rms_rows(inner[m][sub:sub + c] + cross, gain)
            gate = g_ref[n * c:(n + 1) * c, lanes].astype(F32)
            o_ref[n * c:(n + 1) * c, lanes] = (ret * gate).astype(o_ref.dtype)
        state_ref[hh] = state


def _retention(q, kt, v, g, gain, decay, zeta, xi, gc):
    b, s, width = q.shape
    rows = min(RET_GROUP, s)
    span = decay.shape[-1]
    hps = RET_HEADS_PER_STEP
    blk = pl.BlockSpec((None, rows, hps * RET_HEAD_DIM), lambda bi, h, n: (bi, n, h))
    head = lambda shape: pl.BlockSpec((hps,) + shape, lambda bi, h, n: (h, 0, 0))
    return pl.pallas_call(
        _ret_kernel,
        grid=(b, RET_HEADS // hps, s // rows),
        in_specs=[blk,
                  pl.BlockSpec((None, hps, RET_HEAD_DIM, rows), lambda bi, h, n: (bi, h, 0, n)),
                  blk, blk, _const_spec((1, RET_HEAD_DIM)),
                  head((span, span)), head((1, RET_CHUNK)), head((RET_CHUNK, 1)), head((1, 1))],
        out_specs=blk,
        out_shape=jax.ShapeDtypeStruct((b, s, width), BF16),
        scratch_shapes=[pltpu.VMEM((hps, RET_HEAD_DIM, RET_HEAD_DIM), F32)],
        compiler_params=_params("parallel", "parallel", "arbitrary"),
        name="retention",
    )(q, kt, v, g, gain, decay, zeta, xi, gc)


def _ffn_kernel(*refs, n_mix, tiles_per_seq):
    h_ref = refs[0]
    mix_refs = refs[1:1 + n_mix]
    (wo_ref, g_ref, wup_ref, cw_ref, cb_ref, wd_ref, o_ref,
     ubuf_ref, prev_ref, gate_ref) = refs[1 + n_mix:]
    tm = h_ref.shape[0]
    d_ff = wd_ref.shape[0]
    halo = SUBLANES

    @pl.when(pl.program_id(0) % tiles_per_seq == 0)
    def _():
        prev_ref[...] = jnp.zeros_like(prev_ref)

    h = h_ref[...]
    row0 = 0
    for a_ref in mix_refs:
        width = a_ref.shape[1]
        h = h + jnp.dot(a_ref[...], wo_ref[row0:row0 + width, :], preferred_element_type=F32)
        row0 += width
    hn = _rms_rows(h, g_ref[...]).astype(BF16)
    for c0 in range(0, d_ff, FF_CHUNK):
        cols = slice(c0, c0 + FF_CHUNK)
        u = jnp.dot(hn, wup_ref[:, cols], preferred_element_type=F32)
        v = jnp.dot(hn, wup_ref[:, d_ff + c0:d_ff + c0 + FF_CHUNK], preferred_element_type=F32)
        ubuf_ref[0:halo, :] = prev_ref[:, cols]
        ubuf_ref[halo:halo + tm, :] = u
        prev_ref[:, cols] = u[tm - halo:, :]
        last_tap = CONV_WIDTH - 1
        uc = cb_ref[:, cols] + cw_ref[last_tap:CONV_WIDTH, cols] * u
        for tap in range(last_tap):
            back = last_tap - tap
            uc = uc + cw_ref[tap:tap + 1, cols] * ubuf_ref[halo - back:halo - back + tm, :]
        gate_ref[:, cols] = (_silu(uc) * v).astype(BF16)
    o_ref[...] = h + jnp.dot(gate_ref[...], wd_ref[...], preferred_element_type=F32)


def _ffn(h2, mix, wo_stack, wo_layer, g, wup_stack, cw, cb, wd_stack, layer, seq):
    t, d = h2.shape
    d_ff = wd_stack.shape[1]
    tm = min(FFN_ROW_TILE, seq)
    row = lambda i: (i, 0)
    return pl.pallas_call(
        functools.partial(_ffn_kernel, n_mix=len(mix), tiles_per_seq=seq // tm),
        grid=(t // tm,),
        in_specs=([pl.BlockSpec((tm, d), row)]
                  + [pl.BlockSpec((tm, a.shape[1]), row) for a in mix]
                  + [_layer_spec(wo_stack.shape, wo_layer), _const_spec((1, d)),
                     _layer_spec(wup_stack.shape, layer), _const_spec(cw.shape),
                     _const_spec(cb.shape), _layer_spec(wd_stack.shape, layer)]),
        out_specs=pl.BlockSpec((tm, d), row),
        out_shape=jax.ShapeDtypeStruct((t, d), F32),
        scratch_shapes=[pltpu.VMEM((tm + SUBLANES, FF_CHUNK), F32),
                        pltpu.VMEM((SUBLANES, d_ff), F32),
                        pltpu.VMEM((tm, d_ff), BF16)],
        compiler_params=_params("arbitrary"),
        name="conv_ffn",
    )(h2, *mix, wo_stack, g, wup_stack, cw, cb, wd_stack)


def _log_sigmoid(x):
    return jnp.minimum(x, 0.0) - jnp.log(1.0 + jnp.exp(-jnp.abs(x)))


def _odd_in_kernel(x_ref, g_ref, w_ref, wa_ref, walpha_ref, balpha_ref, tri_ref,
                   qt_ref, ktt_ref, kdt_ref, v_ref, r_ref, dec_ref):
    tm = x_ref.shape[0]
    span = tri_ref.shape[0]
    hn = _rms_rows(x_ref[...], g_ref[...]).astype(BF16)

    def proj(c0, width):
        return jnp.dot(hn, w_ref[:, c0:c0 + width], preferred_element_type=F32)

    ga = jnp.dot(hn, wa_ref[...], preferred_element_type=F32)
    v_ref[...] = proj(2 * GLA_K_WIDTH, GLA_V_WIDTH).astype(BF16)
    r_ref[...] = _silu(proj(2 * GLA_K_WIDTH + GLA_V_WIDTH, GLA_V_WIDTH)).astype(BF16)
    ga_hi = ga.astype(BF16)
    ga_lo = (ga - ga_hi.astype(F32)).astype(BF16)
    lane = lax.broadcasted_iota(jnp.int32, ga.shape, 1)
    middle = jnp.logical_and(lane >= GLA_GATE_RANK, lane < 2 * GLA_GATE_RANK)
    pre = jnp.dot(jnp.where(middle, ga_lo, ga_hi), walpha_ref[...],
                  preferred_element_type=F32) + balpha_ref[...]
    la = _log_sigmoid(pre) * (LOG2E / GLA_GATE_TAU)
    la_hi = la.astype(BF16)
    la_lo = (la - la_hi.astype(F32)).astype(BF16)
    tri = tri_ref[...]
    cum = jnp.concatenate(
        [jnp.dot(tri, la_hi[r0:r0 + span], preferred_element_type=F32)
         + jnp.dot(tri, la_lo[r0:r0 + span], preferred_element_type=F32)
         for r0 in range(0, tm, span)], axis=0)
    chunks = tm // GLA_CHUNK
    last = cum.reshape(chunks, GLA_CHUNK, GLA_K_WIDTH)[:, GLA_CHUNK - 1:GLA_CHUNK, :]
    dec_ref[...] = jnp.exp2(last).reshape(chunks, GLA_K_WIDTH)
    to_end = (jnp.broadcast_to(last, (chunks, GLA_CHUNK, GLA_K_WIDTH)).reshape(tm, GLA_K_WIDTH)
              - cum)

    k = proj(GLA_K_WIDTH, GLA_K_WIDTH)
    kt = k * jnp.exp2(-cum)
    kd = k * jnp.exp2(to_end)
    for h in range(GLA_HEADS):
        sl = slice(h * GLA_DK, (h + 1) * GLA_DK)
        ktt_ref[h] = kt[:, sl].T.astype(BF16)
        kdt_ref[h] = kd[:, sl].T.astype(BF16)
    qt_ref[...] = (proj(0, GLA_K_WIDTH) * GLA_DK ** -0.5 * jnp.exp2(cum)).astype(BF16)


def _odd_in(x2, g, w_stack, layer, wa, walpha, balpha, tri, batch, seq):
    t, d = x2.shape
    tm = min(ROW_TILE, seq)
    tps = seq // tm
    row = lambda i: (i, 0)
    kt_shape = jax.ShapeDtypeStruct((batch, GLA_HEADS, GLA_DK, seq), BF16)
    kt_spec = pl.BlockSpec((None, GLA_HEADS, GLA_DK, tm), lambda i: (i // tps, 0, 0, i % tps))
    vout = jax.ShapeDtypeStruct((t, GLA_V_WIDTH), BF16)
    return pl.pallas_call(
        _odd_in_kernel,
        grid=(t // tm,),
        in_specs=[pl.BlockSpec((tm, d), row), _const_spec((1, d)), _layer_spec(w_stack.shape, layer),
                  _const_spec(wa.shape), _const_spec(walpha.shape), _const_spec(balpha.shape),
                  _const_spec(tri.shape)],
        out_specs=[pl.BlockSpec((tm, GLA_K_WIDTH), row), kt_spec, kt_spec,
                   pl.BlockSpec((tm, GLA_V_WIDTH), row), pl.BlockSpec((tm, GLA_V_WIDTH), row),
                   pl.BlockSpec((tm // GLA_CHUNK, GLA_K_WIDTH), row)],
        out_shape=[jax.ShapeDtypeStruct((t, GLA_K_WIDTH), BF16), kt_shape, kt_shape, vout, vout,
                   jax.ShapeDtypeStruct((t // GLA_CHUNK, GLA_K_WIDTH), F32)],
        compiler_params=_params("parallel"),
        name="odd_in",
    )(x2, g, w_stack, wa, walpha, balpha, tri)


def _gla_kernel(qt_ref, ktt_ref, kdt_ref, v_ref, r_ref, dec_ref, gain_ref, o_ref, state_ref):
    @pl.when(pl.program_id(2) == 0)
    def _():
        state_ref[...] = jnp.zeros_like(state_ref)

    c = GLA_CHUNK
    rows = qt_ref.shape[0]
    span = min(GLA_SPAN, rows)
    ri = lax.broadcasted_iota(jnp.int32, (span, span), 0)
    ci = lax.broadcasted_iota(jnp.int32, (span, span), 1)
    causal = jnp.logical_and(ci <= ri, ci // c == ri // c)
    col_chunk = lax.broadcasted_iota(jnp.int32, (1, span), 1) // c
    gain = gain_ref[...]
    for hh in range(state_ref.shape[0]):
        klanes = slice(hh * GLA_DK, (hh + 1) * GLA_DK)
        vlanes = slice(hh * GLA_DV, (hh + 1) * GLA_DV)
        dec_t = dec_ref[:, klanes].T
        spans = range(rows // span)
        qt = [qt_ref[m * span:(m + 1) * span, klanes] for m in spans]
        ktt = [ktt_ref[hh, :, m * span:(m + 1) * span] for m in spans]
        kdt = [kdt_ref[hh, :, m * span:(m + 1) * span] for m in spans]
        v = [v_ref[m * span:(m + 1) * span, vlanes] for m in spans]
        a = [jnp.where(causal, jnp.dot(qt[m], ktt[m], preferred_element_type=F32), 0.0)
             for m in spans]
        intra = [jnp.dot(a[m].astype(BF16), v[m], preferred_element_type=F32) for m in spans]
        contrib = []
        for n in range(rows // c):
            m, sub = divmod(n * c, span)
            kd_n = jnp.where(col_chunk == sub // c, kdt[m], jnp.zeros_like(kdt[m]))
            contrib.append(jnp.dot(kd_n, v[m], preferred_element_type=F32))
        state = state_ref[hh]
        for n in range(rows // c):
            m, sub = divmod(n * c, span)
            inter = jnp.dot(qt[m][sub:sub + c], state.astype(BF16), preferred_element_type=F32)
            state = dec_t[:, n:n + 1] * state + contrib[n]
            o = _rms_rows(intra[m][sub:sub + c] + inter, gain)
            gate = r_ref[n * c:(n + 1) * c, vlanes].astype(F32)
            o_ref[n * c:(n + 1) * c, vlanes] = (o * gate).astype(o_ref.dtype)
        state_ref[hh] = state


def _gla(qt, ktt, kdt, v, r, dec, gain):
    b, s, kw = qt.shape
    vw = v.shape[-1]
    rows = min(GLA_GROUP, s)
    hps = GLA_HEADS_PER_STEP
    chunks = rows // GLA_CHUNK
    dec = dec.reshape(b, s // rows, chunks, kw)
    qblk = pl.BlockSpec((None, rows, hps * GLA_DK), lambda bi, h, n: (bi, n, h))
    tblk = pl.BlockSpec((None, hps, GLA_DK, rows), lambda bi, h, n: (bi, h, 0, n))
    vblk = pl.BlockSpec((None, rows, hps * GLA_DV), lambda bi, h, n: (bi, n, h))
    dblk = pl.BlockSpec((None, None, chunks, hps * GLA_DK), lambda bi, h, n: (bi, n, 0, h))
    return pl.pallas_call(
        _gla_kernel,
        grid=(b, GLA_HEADS // hps, s // rows),
        in_specs=[qblk, tblk, tblk, vblk, vblk, dblk, _const_spec((1, GLA_DV))],
        out_specs=vblk,
        out_shape=jax.ShapeDtypeStruct((b, s, vw), BF16),
        scratch_shapes=[pltpu.VMEM((hps, GLA_DK, GLA_DV), F32)],
        compiler_params=_params("parallel", "parallel", "arbitrary"),
        name="gla",
    )(qt, ktt, kdt, v, r, dec, gain)


def _rope_tables(seq, tile):
    half = RET_HEAD_DIM // 2
    inv = ROPE_BASE ** (-np.arange(half, dtype=np.float64) / half)

    def both(pos):
        ang = pos[:, None] * inv[None, :]
        dup = lambda t: np.concatenate([t, t], axis=-1).astype(np.float32)
        return dup(np.cos(ang)), dup(np.sin(ang))

    cos_a, sin_a = both(np.arange(0, seq, tile, dtype=np.float64))
    cos_b, sin_b = both(np.arange(tile, dtype=np.float64))
    return (jnp.asarray(cos_a[:, None, :]), jnp.asarray(sin_a[:, None, :]),
            jnp.asarray(cos_b), jnp.asarray(sin_b))


def _retention_tables(span):
    c = RET_CHUNK
    lg = np.log(1.0 - np.exp2(-5.0 - np.arange(RET_HEADS, dtype=np.float64)))[:, None]
    pos = np.arange(span, dtype=np.float64)
    diff = pos[:, None] - pos[None, :]
    same = (pos[:, None] // c) == (pos[None, :] // c)
    decay = np.where((diff >= 0) & same, np.exp(lg[:, :, None] * np.maximum(diff, 0.0)), 0.0)
    pos = np.arange(c, dtype=np.float64)
    zeta = np.exp(lg * (c - 1 - pos))[:, None, :]
    xi = np.exp(lg * (pos + 1.0))[:, :, None]
    g_chunk = np.exp(lg * c)[:, :, None]
    return tuple(jnp.asarray(t.astype(np.float32)) for t in (decay, zeta, xi, g_chunk))


def _suffix_tri(n):
    idx = np.arange(n)
    return jnp.asarray(idx[:, None] >= idx[None, :], dtype=BF16)


def _chunk_prefix_tri(rows, chunk):
    idx = np.arange(rows)
    same = (idx[:, None] // chunk) == (idx[None, :] // chunk)
    return jnp.asarray(same & (idx[None, :] <= idx[:, None]), dtype=BF16)


def _gate_operands(w_gate_in, w_alpha):
    rank = GLA_GATE_RANK
    wa = jnp.pad(jnp.tile(w_gate_in, (1, 3)), ((0, 0), (0, LANES - 3 * rank))).astype(BF16)
    w_hi = w_alpha.astype(BF16)
    w_lo = (w_alpha - w_hi.astype(F32)).astype(BF16)
    walpha = jnp.pad(jnp.concatenate([w_hi, w_hi, w_lo], axis=0), ((0, LANES - 3 * rank), (0, 0)))
    return wa, walpha


def kernel(x, mix_norm_g, even_w_in, sb_q_gain, sb_k_gain, ret_out_gain, even_w_out,
           odd_w_in, gla_w_alpha, gla_b_alpha, gla_out_gain, odd_w_out,
           ffn_norm_g, ffn_w_up, ffn_conv_w, ffn_conv_b, ffn_w_down):
    b, s, d = x.shape
    t = b * s
    depth = mix_norm_g.shape[0]
    h = x.reshape(t, d)

    rope_tables = _rope_tables(s, min(ROW_TILE, s))
    decay, zeta, xi, g_chunk = _retention_tables(min(RET_SPAN, s))
    sb_blk = min(SB_BLOCK, s)
    sb_tri = _suffix_tri(sb_blk)
    gla_tri = _chunk_prefix_tri(min(GLA_SPAN, s), GLA_CHUNK)

    even_w_in, even_w_out, odd_w_in_b, odd_w_out, ffn_w_up, ffn_w_down = (
        w.astype(BF16) for w in (even_w_in, even_w_out, odd_w_in, odd_w_out, ffn_w_up, ffn_w_down))

    for layer in range(depth):
        g_mix = mix_norm_g[layer][None, :]
        if layer % 2 == 0:
            e = layer // 2
            qg = jnp.tile(sb_q_gain[e], 2)[None, :]
            kg = jnp.tile(sb_k_gain[e], 2)[None, :]
            sbq, sbkt, sbv, rq, rkt, rv, rg = _even_in(
                h, g_mix, even_w_in, e, qg, kg, rope_tables, b, s, sb_blk)
            out_a = _sb_attention(sbq.reshape(b, s, SB_WIDTH), sbkt, sbv.reshape(b, s, SB_WIDTH),
                                  sb_tri, qg, kg)
            shp = (b, s, RET_WIDTH)
            out_b = _retention(rq.reshape(shp), rkt, rv.reshape(shp), rg.reshape(shp),
                               ret_out_gain[e][None, :], decay, zeta, xi, g_chunk)
            mix = (out_a.reshape(t, SB_WIDTH), out_b.reshape(t, RET_WIDTH))
            w_out, w_out_layer = even_w_out, e
        else:
            o = layer // 2
            n_main = 2 * GLA_K_WIDTH + 2 * GLA_V_WIDTH
            wa, walpha = _gate_operands(odd_w_in[o, :, n_main:], gla_w_alpha[o])
            qt, ktt, kdt, gv, gr, dec = _odd_in(h, g_mix, odd_w_in_b, o, wa, walpha,
                                                gla_b_alpha[o][None, :], gla_tri, b, s)
            og = _gla(qt.reshape(b, s, GLA_K_WIDTH), ktt, kdt,
                      gv.reshape(b, s, GLA_V_WIDTH), gr.reshape(b, s, GLA_V_WIDTH),
                      dec.reshape(b, s // GLA_CHUNK, GLA_K_WIDTH), gla_out_gain[o][None, :])
            mix = (og.reshape(t, GLA_V_WIDTH),)
            w_out, w_out_layer = odd_w_out, o
        h = _ffn(h, mix, w_out, w_out_layer, ffn_norm_g[layer][None, :], ffn_w_up,
                 ffn_conv_w[layer], ffn_conv_b[layer][None, :], ffn_w_down, layer, s)
    return h.reshape(b, s, d)
```

```python
import functools

import numpy as np

import jax
import jax.numpy as jnp
from jax import lax
from jax.experimental import pallas as pl
from jax.experimental.pallas import tpu as pltpu

F32 = jnp.float32
BF16 = jnp.bfloat16

EPS = 1e-6
LOG2E = 1.4426950408889634
ROPE_BASE = 10000.0
EXP2_UNDERFLOW = 150.0
SB_NORM_SLACK = 1.05
SOFTPLUS2_LINEAR_ABOVE = 32.0
MASKED_SCORE = -1e30

SB_HEADS = 8
SB_HEAD_DIM = 64
SB_WIDTH = SB_HEADS * SB_HEAD_DIM
RET_HEADS = 4
RET_HEAD_DIM = 128
RET_WIDTH = RET_HEADS * RET_HEAD_DIM
RET_CHUNK = 128
GLA_HEADS = 4
GLA_DK = 128
GLA_DV = 256
GLA_K_WIDTH = GLA_HEADS * GLA_DK
GLA_V_WIDTH = GLA_HEADS * GLA_DV
GLA_GATE_RANK = 16
GLA_GATE_TAU = 16.0
GLA_CHUNK = 64
CONV_WIDTH = 3

LANES = 128
SUBLANES = 8
MXU_DIM = 256
VMEM_LIMIT = 56 * 1024 * 1024

ROW_TILE = 1024
FFN_ROW_TILE = 1024
SB_BLOCK = MXU_DIM
SB_QBLOCKS_PER_STEP = 4
SB_RECENT_ROWS = 176
RET_HEADS_PER_STEP = 4
RET_GROUP = 1024
RET_SPAN = 2 * RET_CHUNK
GLA_GROUP = 512
GLA_HEADS_PER_STEP = 4
GLA_SPAN = 4 * GLA_CHUNK
FF_CHUNK = 256


def _params(*sem):
    return pltpu.CompilerParams(dimension_semantics=sem, vmem_limit_bytes=VMEM_LIMIT)


def _const_spec(shape):
    n = len(shape)
    return pl.BlockSpec(shape, lambda *_: (0,) * n)


def _layer_spec(stacked_shape, layer):
    zeros = (0,) * (len(stacked_shape) - 1)
    return pl.BlockSpec((None,) + tuple(stacked_shape[1:]), lambda *_: (layer,) + zeros,
                        pipeline_mode=pl.Buffered(1))


def _rms_rows(x, g):
    ms = jnp.mean(x * x, axis=-1, keepdims=True)
    return x * lax.rsqrt(ms + EPS) * g


def _silu(x):
    return x / (1.0 + jnp.exp2(x * -LOG2E))


def _half_head_rms(blk):
    sq = blk * blk
    lane = lax.broadcasted_iota(jnp.int32, blk.shape, 1)
    lo = lane < SB_HEAD_DIM
    s_lo = jnp.sum(jnp.where(lo, sq, 0.0), axis=-1, keepdims=True)
    s_hi = jnp.sum(jnp.where(lo, 0.0, sq), axis=-1, keepdims=True)
    ms = jnp.where(lo, s_lo, s_hi) * (1.0 / SB_HEAD_DIM)
    return blk * lax.rsqrt(ms + EPS)


def _even_in_kernel(x_ref, g_ref, w_ref, qg_ref, kg_ref, ca_ref, sa_ref, cb_ref, sb_ref,
                    sbq_ref, sbkt_ref, sbv_ref, rq_ref, rkt_ref, rv_ref, rg_ref):
    tm = x_ref.shape[0]
    hn = _rms_rows(x_ref[...], g_ref[...]).astype(BF16)

    def proj(seg):
        return jnp.dot(hn, w_ref[:, seg * SB_WIDTH:(seg + 1) * SB_WIDTH],
                       preferred_element_type=F32)

    p = proj(1)
    blk = sbkt_ref.shape[-1]
    for hp in range(SB_WIDTH // LANES):
        sl = slice(hp * LANES, (hp + 1) * LANES)
        kt = (_half_head_rms(p[:, sl]) * kg_ref[...]).T
        for c in range(tm // blk):
            sbkt_ref[hp, c] = kt[:, c * blk:(c + 1) * blk].astype(BF16)

    cos_a, sin_a, cos_b, sin_b = ca_ref[...], sa_ref[...], cb_ref[...], sb_ref[...]
    sign = jnp.where(lax.broadcasted_iota(jnp.int32, (1, LANES), 1) < RET_HEAD_DIM // 2, -1.0, 1.0)
    cos2 = cos_a * cos_b - sin_a * sin_b
    sin2 = sign * (sin_a * cos_b + cos_a * sin_b)

    def rope(blk):
        return blk * cos2 + pltpu.roll(blk, RET_HEAD_DIM // 2, 1) * sin2

    p = proj(4)
    for h in range(RET_HEADS):
        sl = slice(h * LANES, (h + 1) * LANES)
        rkt_ref[h] = (rope(p[:, sl]) * RET_HEAD_DIM ** -0.5).T.astype(BF16)
    q_scale = SB_HEAD_DIM ** -0.5 * LOG2E
    p = proj(0)
    for hp in range(SB_WIDTH // LANES):
        sl = slice(hp * LANES, (hp + 1) * LANES)
        sbq_ref[:, sl] = (_half_head_rms(p[:, sl]) * (qg_ref[...] * q_scale)).astype(BF16)
    p = proj(3)
    for h in range(RET_HEADS):
        sl = slice(h * LANES, (h + 1) * LANES)
        rq_ref[:, sl] = rope(p[:, sl]).astype(BF16)
    rg_ref[...] = _silu(proj(6)).astype(BF16)
    sbv_ref[...] = proj(2).astype(BF16)
    rv_ref[...] = proj(5).astype(BF16)


def _even_in(x2, g, w_stack, layer, qg, kg, rope_tables, batch, seq, sb_blk):
    t, d = x2.shape
    tm = min(ROW_TILE, seq)
    tps = seq // tm
    tile_start = pl.BlockSpec((None, 1, LANES), lambda i: (i % tps, 0, 0))
    row = lambda i: (i, 0)
    pairs = SB_WIDTH // LANES
    flat = jax.ShapeDtypeStruct((t, SB_WIDTH), BF16)
    flat_spec = pl.BlockSpec((tm, SB_WIDTH), row)
    sbkt = jax.ShapeDtypeStruct((batch, pairs, seq // sb_blk, LANES, sb_blk), BF16)
    sbkt_spec = pl.BlockSpec((None, pairs, tm // sb_blk, LANES, sb_blk),
                             lambda i: (i // tps, 0, i % tps, 0, 0))
    rkt = jax.ShapeDtypeStruct((batch, RET_HEADS, RET_HEAD_DIM, seq), BF16)
    rkt_spec = pl.BlockSpec((None, RET_HEADS, RET_HEAD_DIM, tm), lambda i: (i // tps, 0, 0, i % tps))
    return pl.pallas_call(
        _even_in_kernel,
        grid=(t // tm,),
        in_specs=[
            pl.BlockSpec((tm, d), row),
            _const_spec((1, d)),
            _layer_spec(w_stack.shape, layer),
            _const_spec((1, LANES)),
            _const_spec((1, LANES)),
            tile_start,
            tile_start,
            _const_spec((tm, LANES)),
            _const_spec((tm, LANES)),
        ],
        out_specs=[flat_spec, sbkt_spec, flat_spec, flat_spec, rkt_spec, flat_spec, flat_spec],
        out_shape=[flat, sbkt, flat, flat, rkt, flat, flat],
        compiler_params=_params("parallel"),
        name="even_in",
    )(x2, g, w_stack, qg, kg, *rope_tables)


def _softplus2(z):
    return jnp.where(z > SOFTPLUS2_LINEAR_ABOVE, z, jnp.log2(1.0 + jnp.exp2(z)))


def _sb_kernel(q_ref, kt_ref, v_ref, tri_ref, qg_ref, kg_ref, o_ref, acc_ref, carry_ref):
    step = pl.program_id(2)
    blk = kt_ref.shape[-1]
    per_step = q_ref.shape[0] // blk
    z_bound = (SB_NORM_SLACK * SB_HEAD_DIM * SB_HEAD_DIM ** -0.5 * LOG2E
               * jnp.max(jnp.abs(qg_ref[...])) * jnp.max(jnp.abs(kg_ref[...])))
    dead_above = z_bound + EXP2_UNDERFLOW
    lane = lax.broadcasted_iota(jnp.int32, (blk, LANES), 1)
    lo = lane < SB_HEAD_DIM
    row_id = lax.broadcasted_iota(jnp.int32, (blk, blk), 0)
    col_id = lax.broadcasted_iota(jnp.int32, (blk, blk), 1)
    below_diag = col_id < row_id
    tri = tri_ref[...]
    heads = range(2)

    def split_heads(q):
        zero = jnp.zeros_like(q)
        return (jnp.where(lo, q, zero), jnp.where(lo, zero, q))

    def key_blocks(j_a):
        j_b = jnp.maximum(j_a - 1, 0)
        return j_a > 0, j_b

    def values(j):
        return v_ref[pl.ds(pl.multiple_of(j * blk, blk), blk), :]

    def write_out(n):
        o_ref[n * blk:(n + 1) * blk, :] = jnp.where(lo, acc_ref[n, 0], acc_ref[n, 1]).astype(o_ref.dtype)

    def diagonal_pairs(blocks, b_rows, write_outputs):
        chains = [(n, h) for n, _ in blocks for h in heads]
        j_as = {n: j_a for n, j_a in blocks}
        top = slice(0, b_rows)
        has_b, kt_a, kt_b, v_a, v_b = {}, {}, {}, {}, {}
        for n, j_a in blocks:
            has_b[n], j_b = key_blocks(j_a)
            kt_a[n] = kt_ref[j_a]
            kt_b[n] = kt_ref[j_b]
            v_a[n] = values(j_a)
            v_b[n] = values(j_b)
        z_a = {(n, h): jnp.dot(q_heads[n][h], kt_a[n], preferred_element_type=F32) for n, h in chains}
        z_b = {(n, h): jnp.dot(q_heads[n][h][top], kt_b[n], preferred_element_type=F32)
               for n, h in chains}
        z_a = {c: jnp.where(below_diag, z, MASKED_SCORE) for c, z in z_a.items()}
        sp_a = {c: _softplus2(z) for c, z in z_a.items()}
        sp_b = {c: _softplus2(z) for c, z in z_b.items()}
        cs_a = {c: jnp.dot(sp.astype(BF16), tri, preferred_element_type=F32) for c, sp in sp_a.items()}
        cs_b = {c: jnp.dot(sp.astype(BF16), tri, preferred_element_type=F32) for c, sp in sp_b.items()}
        tot_a, tot_b, w_a, w_b = {}, {}, {}, {}
        for n, h in chains:
            c = (n, h)
            tot_a[c] = jnp.sum(sp_a[c], axis=-1, keepdims=True)
            tot_b[c] = jnp.where(has_b[n], jnp.sum(sp_b[c], axis=-1, keepdims=True), 0.0)
            before_b = jnp.where(has_b[n], tot_a[c][top], -MASKED_SCORE)
            w_a[c] = jnp.exp2(z_a[c] - cs_a[c])
            w_b[c] = jnp.exp2(z_b[c] - (cs_b[c] + before_b))
        for n, h in chains:
            c = (n, h)
            pv_a = jnp.dot(w_a[c].astype(BF16), v_a[n], preferred_element_type=F32)
            pv_b = jnp.dot(w_b[c].astype(BF16), v_b[n], preferred_element_type=F32)
            acc_ref[n, h, top] = pv_a[top] + pv_b
            carry_ref[n, h, top] = tot_a[c][top] + tot_b[c]
            if b_rows < blk:
                acc_ref[n, h, b_rows:] = pv_a[b_rows:]
                carry_ref[n, h, b_rows:] = tot_a[c][b_rows:]
        if write_outputs:
            for n, _ in blocks:
                write_out(n)

    first = step * per_step
    q_heads = [split_heads(q_ref[n * blk:(n + 1) * blk, :]) for n in range(per_step)]
    recent_rows = min(SB_RECENT_ROWS, blk)
    diagonal_pairs([(n, first + n) for n in range(per_step)], recent_rows, True)

    def older_pair(n, j_a):
        has_b, j_b = key_blocks(j_a)
        kt_a, kt_b = kt_ref[j_a], kt_ref[j_b]
        v_a, v_b = values(j_a), values(j_b)
        z_a = [jnp.dot(q_heads[n][h], kt_a, preferred_element_type=F32) for h in heads]
        z_b = [jnp.dot(q_heads[n][h], kt_b, preferred_element_type=F32) for h in heads]
        sp_a = [_softplus2(z) for z in z_a]
        sp_b = [_softplus2(z) for z in z_b]
        cs_a = [jnp.dot(sp.astype(BF16), tri, preferred_element_type=F32) for sp in sp_a]
        cs_b = [jnp.dot(sp.astype(BF16), tri, preferred_element_type=F32) for sp in sp_b]
        for h in heads:
            carry = carry_ref[n, h]
            after_a = carry + jnp.sum(sp_a[h], axis=-1, keepdims=True)
            tot_b = jnp.where(has_b, jnp.sum(sp_b[h], axis=-1, keepdims=True), 0.0)
            before_b = jnp.where(has_b, after_a, -MASKED_SCORE)
            w_a = jnp.exp2(z_a[h] - (cs_a[h] + carry))
            w_b = jnp.exp2(z_b[h] - (cs_b[h] + before_b))
            acc_ref[n, h] += (jnp.dot(w_a.astype(BF16), v_a, preferred_element_type=F32)
                              + jnp.dot(w_b.astype(BF16), v_b, preferred_element_type=F32))
            carry_ref[n, h] = after_a + tot_b

    def live(n):
        return jnp.minimum(jnp.min(carry_ref[n, 0]), jnp.min(carry_ref[n, 1])) <= dead_above

    row = lax.broadcasted_iota(jnp.int32, (blk, 1), 0)
    first_unfinished_block = jnp.where(row >= recent_rows, 1, 2)
    least = None
    for n in range(per_step):
        has_unvisited = first + n >= first_unfinished_block
        for h in heads:
            candidate = jnp.where(has_unvisited, carry_ref[n, h], -MASKED_SCORE)
            least = candidate if least is None else jnp.minimum(least, candidate)
    any_live = jnp.min(least) <= dead_above

    @pl.when(any_live)
    def _():
        for n in range(per_step):
            diagonal_pairs([(n, first + n)], blk, False)

            def body(c, n=n):
                older_pair(n, c[0])
                return c[0] - 2, live(n)

            lax.while_loop(lambda c: jnp.logical_and(c[0] >= 0, c[1]), body, (first + n - 2, live(n)))
            write_out(n)


def _sb_attention(q, kt, v, tri, qg, kg):
    b, s, width = q.shape
    blk = kt.shape[-1]
    n_blocks = s // blk
    per_step = min(SB_QBLOCKS_PER_STEP, n_blocks)
    assert n_blocks % per_step == 0
    rows = per_step * blk
    return pl.pallas_call(
        _sb_kernel,
        grid=(b, width // LANES, n_blocks // per_step),
        in_specs=[
            pl.BlockSpec((None, rows, LANES), lambda bi, hp, i: (bi, i, hp)),
            pl.BlockSpec((None, None, n_blocks, LANES, blk), lambda bi, hp, i: (bi, hp, 0, 0, 0)),
            pl.BlockSpec((None, s, LANES), lambda bi, hp, i: (bi, 0, hp)),
            _const_spec(tri.shape),
            _const_spec((1, LANES)),
            _const_spec((1, LANES)),
        ],
        out_specs=pl.BlockSpec((None, rows, LANES), lambda bi, hp, i: (bi, i, hp)),
        out_shape=jax.ShapeDtypeStruct((b, s, width), BF16),
        scratch_shapes=[pltpu.VMEM((per_step, 2, blk, LANES), F32),
                        pltpu.VMEM((per_step, 2, blk, 1), F32)],
        compiler_params=_params("parallel", "parallel", "arbitrary"),
        name="sb_attention",
    )(q, kt, v, tri, qg, kg)


def _ret_kernel(q_ref, kt_ref, v_ref, g_ref, gain_ref, decay_ref, zeta_ref, xi_ref, gc_ref,
                o_ref, state_ref):
    @pl.when(pl.program_id(2) == 0)
    def _():
        state_ref[...] = jnp.zeros_like(state_ref)

    c = RET_CHUNK
    d = RET_HEAD_DIM
    span = decay_ref.shape[-1]
    rows = q_ref.shape[0]
    gain = gain_ref[...]
    for hh in range(state_ref.shape[0]):
        lanes = slice(hh * d, (hh + 1) * d)
        decay = decay_ref[hh]
        zeta = zeta_ref[hh]
        xi = xi_ref[hh]
        g_chunk = gc_ref[hh]
        q = [q_ref[m * span:(m + 1) * span, lanes] for m in range(rows // span)]
        kt = [kt_ref[hh, :, m * span:(m + 1) * span] for m in range(rows // span)]
        v = [v_ref[m * span:(m + 1) * span, lanes] for m in range(rows // span)]
        scores = [jnp.dot(q[m], kt[m], preferred_element_type=F32) * decay
                  for m in range(rows // span)]
        inner = [jnp.dot(scores[m].astype(BF16), v[m], preferred_element_type=F32)
                 for m in range(rows // span)]
        contrib = []
        for n in range(rows // c):
            m, sub = divmod(n * c, span)
            kz = (kt[m][:, sub:sub + c].astype(F32) * zeta).astype(BF16)
            contrib.append(jnp.dot(kz, v[m][sub:sub + c], preferred_element_type=F32))
        state = state_ref[hh]
        for n in range(rows // c):
            m, sub = divmod(n * c, span)
            cross = jnp.dot(q[m][sub:sub + c], state.astype(BF16), preferred_element_type=F32) * xi
            state = g_chunk * state + contrib[n]
            ret = _rms_rows(inner[m][sub:sub + c] + cross, gain)
            gate = g_ref[n * c:(n + 1) * c, lanes].astype(F32)
            o_ref[n * c:(n + 1) * c, lanes] = (ret * gate).astype(o_ref.dtype)
        state_ref[hh] = state


def _retention(q, kt, v, g, gain, decay, zeta, xi, gc):
    b, s, width = q.shape
    rows = min(RET_GROUP, s)
    span = decay.shape[-1]
    hps = RET_HEADS_PER_STEP
    blk = pl.BlockSpec((None, rows, hps * RET_HEAD_DIM), lambda bi, h, n: (bi, n, h))
    head = lambda shape: pl.BlockSpec((hps,) + shape, lambda bi, h, n: (h, 0, 0))
    return pl.pallas_call(
        _ret_kernel,
        grid=(b, RET_HEADS // hps, s // rows),
        in_specs=[blk,
                  pl.BlockSpec((None, hps, RET_HEAD_DIM, rows), lambda bi, h, n: (bi, h, 0, n)),
                  blk, blk, _const_spec((1, RET_HEAD_DIM)),
                  head((span, span)), head((1, RET_CHUNK)), head((RET_CHUNK, 1)), head((1, 1))],
        out_specs=blk,
        out_shape=jax.ShapeDtypeStruct((b, s, width), BF16),
        scratch_shapes=[pltpu.VMEM((hps, RET_HEAD_DIM, RET_HEAD_DIM), F32)],
        compiler_params=_params("parallel", "parallel", "arbitrary"),
        name="retention",
    )(q, kt, v, g, gain, decay, zeta, xi, gc)


def _ffn_kernel(*refs, n_mix, tiles_per_seq):
    h_ref = refs[0]
    mix_refs = refs[1:1 + n_mix]
    (wo_ref, g_ref, wup_ref, cw_ref, cb_ref, wd_ref, o_ref,
     ubuf_ref, prev_ref, gate_ref) = refs[1 + n_mix:]
    tm = h_ref.shape[0]
    d_ff = wd_ref.shape[0]
    halo = SUBLANES

    @pl.when(pl.program_id(0) % tiles_per_seq == 0)
    def _():
        prev_ref[...] = jnp.zeros_like(prev_ref)

    h = h_ref[...]
    row0 = 0
    for a_ref in mix_refs:
        width = a_ref.shape[1]
        h = h + jnp.dot(a_ref[...], wo_ref[row0:row0 + width, :], preferred_element_type=F32)
        row0 += width
    hn = _rms_rows(h, g_ref[...]).astype(BF16)
    for c0 in range(0, d_ff, FF_CHUNK):
        cols = slice(c0, c0 + FF_CHUNK)
        u = jnp.dot(hn, wup_ref[:, cols], preferred_element_type=F32)
        v = jnp.dot(hn, wup_ref[:, d_ff + c0:d_ff + c0 + FF_CHUNK], preferred_element_type=F32)
        ubuf_ref[0:halo, :] = prev_ref[:, cols]
        ubuf_ref[halo:halo + tm, :] = u
        prev_ref[:, cols] = u[tm - halo:, :]
        last_tap = CONV_WIDTH - 1
        uc = cb_ref[:, cols] + cw_ref[last_tap:CONV_WIDTH, cols] * u
        for tap in range(last_tap):
            back = last_tap - tap
            uc = uc + cw_ref[tap:tap + 1, cols] * ubuf_ref[halo - back:halo - back + tm, :]
        gate_ref[:, cols] = (_silu(uc) * v).astype(BF16)
    o_ref[...] = h + jnp.dot(gate_ref[...], wd_ref[...], preferred_element_type=F32)


def _ffn(h2, mix, wo_stack, wo_layer, g, wup_stack, cw, cb, wd_stack, layer, seq):
    t, d = h2.shape
    d_ff = wd_stack.shape[1]
    tm = min(FFN_ROW_TILE, seq)
    row = lambda i: (i, 0)
    return pl.pallas_call(
        functools.partial(_ffn_kernel, n_mix=len(mix), tiles_per_seq=seq // tm),
        grid=(t // tm,),
        in_specs=([pl.BlockSpec((tm, d), row)]
                  + [pl.BlockSpec((tm, a.shape[1]), row) for a in mix]
                  + [_layer_spec(wo_stack.shape, wo_layer), _const_spec((1, d)),
                     _layer_spec(wup_stack.shape, layer), _const_spec(cw.shape),
                     _const_spec(cb.shape), _layer_spec(wd_stack.shape, layer)]),
        out_specs=pl.BlockSpec((tm, d), row),
        out_shape=jax.ShapeDtypeStruct((t, d), F32),
        scratch_shapes=[pltpu.VMEM((tm + SUBLANES, FF_CHUNK), F32),
                        pltpu.VMEM((SUBLANES, d_ff), F32),
                        pltpu.VMEM((tm, d_ff), BF16)],
        compiler_params=_params("arbitrary"),
        name="conv_ffn",
    )(h2, *mix, wo_stack, g, wup_stack, cw, cb, wd_stack)


def _log_sigmoid(x):
    return jnp.minimum(x, 0.0) - jnp.log(1.0 + jnp.exp(-jnp.abs(x)))


def _odd_in_kernel(x_ref, g_ref, w_ref, wa_ref, walpha_ref, balpha_ref, tri_ref,
                   qt_ref, ktt_ref, kdt_ref, v_ref, r_ref, dec_ref):
    tm = x_ref.shape[0]
    span = tri_ref.shape[0]
    hn = _rms_rows(x_ref[...], g_ref[...]).astype(BF16)

    def proj(c0, width):
        return jnp.dot(hn, w_ref[:, c0:c0 + width], preferred_element_type=F32)

    ga = jnp.dot(hn, wa_ref[...], preferred_element_type=F32)
    v_ref[...] = proj(2 * GLA_K_WIDTH, GLA_V_WIDTH).astype(BF16)
    r_ref[...] = _silu(proj(2 * GLA_K_WIDTH + GLA_V_WIDTH, GLA_V_WIDTH)).astype(BF16)
    ga_hi = ga.astype(BF16)
    ga_lo = (ga - ga_hi.astype(F32)).astype(BF16)
    lane = lax.broadcasted_iota(jnp.int32, ga.shape, 1)
    middle = jnp.logical_and(lane >= GLA_GATE_RANK, lane < 2 * GLA_GATE_RANK)
    pre = jnp.dot(jnp.where(middle, ga_lo, ga_hi), walpha_ref[...],
                  preferred_element_type=F32) + balpha_ref[...]
    la = _log_sigmoid(pre) * (LOG2E / GLA_GATE_TAU)
    la_hi = la.astype(BF16)
    la_lo = (la - la_hi.astype(F32)).astype(BF16)
    tri = tri_ref[...]
    cum = jnp.concatenate(
        [jnp.dot(tri, la_hi[r0:r0 + span], preferred_element_type=F32)
         + jnp.dot(tri, la_lo[r0:r0 + span], preferred_element_type=F32)
         for r0 in range(0, tm, span)], axis=0)
    chunks = tm // GLA_CHUNK
    last = cum.reshape(chunks, GLA_CHUNK, GLA_K_WIDTH)[:, GLA_CHUNK - 1:GLA_CHUNK, :]
    dec_ref[...] = jnp.exp2(last).reshape(chunks, GLA_K_WIDTH)
    to_end = (jnp.broadcast_to(last, (chunks, GLA_CHUNK, GLA_K_WIDTH)).reshape(tm, GLA_K_WIDTH)
              - cum)

    k = proj(GLA_K_WIDTH, GLA_K_WIDTH)
    kt = k * jnp.exp2(-cum)
    kd = k * jnp.exp2(to_end)
    for h in range(GLA_HEADS):
        sl = slice(h * GLA_DK, (h + 1) * GLA_DK)
        ktt_ref[h] = kt[:, sl].T.astype(BF16)
        kdt_ref[h] = kd[:, sl].T.astype(BF16)
    qt_ref[...] = (proj(0, GLA_K_WIDTH) * GLA_DK ** -0.5 * jnp.exp2(cum)).astype(BF16)


def _odd_in(x2, g, w_stack, layer, wa, walpha, balpha, tri, batch, seq):
    t, d = x2.shape
    tm = min(ROW_TILE, seq)
    tps = seq // tm
    row = lambda i: (i, 0)
    kt_shape = jax.ShapeDtypeStruct((batch, GLA_HEADS, GLA_DK, seq), BF16)
    kt_spec = pl.BlockSpec((None, GLA_HEADS, GLA_DK, tm), lambda i: (i // tps, 0, 0, i % tps))
    vout = jax.ShapeDtypeStruct((t, GLA_V_WIDTH), BF16)
    return pl.pallas_call(
        _odd_in_kernel,
        grid=(t // tm,),
        in_specs=[pl.BlockSpec((tm, d), row), _const_spec((1, d)), _layer_spec(w_stack.shape, layer),
                  _const_spec(wa.shape), _const_spec(walpha.shape), _const_spec(balpha.shape),
                  _const_spec(tri.shape)],
        out_specs=[pl.BlockSpec((tm, GLA_K_WIDTH), row), kt_spec, kt_spec,
                   pl.BlockSpec((tm, GLA_V_WIDTH), row), pl.BlockSpec((tm, GLA_V_WIDTH), row),
                   pl.BlockSpec((tm // GLA_CHUNK, GLA_K_WIDTH), row)],
        out_shape=[jax.ShapeDtypeStruct((t, GLA_K_WIDTH), BF16), kt_shape, kt_shape, vout, vout,
                   jax.ShapeDtypeStruct((t // GLA_CHUNK, GLA_K_WIDTH), F32)],
        compiler_params=_params("parallel"),
        name="odd_in",
    )(x2, g, w_stack, wa, walpha, balpha, tri)


def _gla_kernel(qt_ref, ktt_ref, kdt_ref, v_ref, r_ref, dec_ref, gain_ref, o_ref, state_ref):
    @pl.when(pl.program_id(2) == 0)
    def _():
        state_ref[...] = jnp.zeros_like(state_ref)

    c = GLA_CHUNK
    rows = qt_ref.shape[0]
    span = min(GLA_SPAN, rows)
    ri = lax.broadcasted_iota(jnp.int32, (span, span), 0)
    ci = lax.broadcasted_iota(jnp.int32, (span, span), 1)
    causal = jnp.logical_and(ci <= ri, ci // c == ri // c)
    col_chunk = lax.broadcasted_iota(jnp.int32, (1, span), 1) // c
    gain = gain_ref[...]
    for hh in range(state_ref.shape[0]):
        klanes = slice(hh * GLA_DK, (hh + 1) * GLA_DK)
        vlanes = slice(hh * GLA_DV, (hh + 1) * GLA_DV)
        dec_t = dec_ref[:, klanes].T
        spans = range(rows // span)
        qt = [qt_ref[m * span:(m + 1) * span, klanes] for m in spans]
        ktt = [ktt_ref[hh, :, m * span:(m + 1) * span] for m in spans]
        kdt = [kdt_ref[hh, :, m * span:(m + 1) * span] for m in spans]
        v = [v_ref[m * span:(m + 1) * span, vlanes] for m in spans]
        a = [jnp.where(causal, jnp.dot(qt[m], ktt[m], preferred_element_type=F32), 0.0)
             for m in spans]
        intra = [jnp.dot(a[m].astype(BF16), v[m], preferred_element_type=F32) for m in spans]
        contrib = []
        for n in range(rows // c):
            m, sub = divmod(n * c, span)
            kd_n = jnp.where(col_chunk == sub // c, kdt[m], jnp.zeros_like(kdt[m]))
            contrib.append(jnp.dot(kd_n, v[m], preferred_element_type=F32))
        state = state_ref[hh]
        for n in range(rows // c):
            m, sub = divmod(n * c, span)
            inter = jnp.dot(qt[m][sub:sub + c], state.astype(BF16), preferred_element_type=F32)
            state = dec_t[:, n:n + 1] * state + contrib[n]
            o = _rms_rows(intra[m][sub:sub + c] + inter, gain)
            gate = r_ref[n * c:(n + 1) * c, vlanes].astype(F32)
            o_ref[n * c:(n + 1) * c, vlanes] = (o * gate).astype(o_ref.dtype)
        state_ref[hh] = state


def _gla(qt, ktt, kdt, v, r, dec, gain):
    b, s, kw = qt.shape
    vw = v.shape[-1]
    rows = min(GLA_GROUP, s)
    hps = GLA_HEADS_PER_STEP
    chunks = rows // GLA_CHUNK
    dec = dec.reshape(b, s // rows, chunks, kw)
    qblk = pl.BlockSpec((None, rows, hps * GLA_DK), lambda bi, h, n: (bi, n, h))
    tblk = pl.BlockSpec((None, hps, GLA_DK, rows), lambda bi, h, n: (bi, h, 0, n))
    vblk = pl.BlockSpec((None, rows, hps * GLA_DV), lambda bi, h, n: (bi, n, h))
    dblk = pl.BlockSpec((None, None, chunks, hps * GLA_DK), lambda bi, h, n: (bi, n, 0, h))
    return pl.pallas_call(
        _gla_kernel,
        grid=(b, GLA_HEADS // hps, s // rows),
        in_specs=[qblk, tblk, tblk, vblk, vblk, dblk, _const_spec((1, GLA_DV))],
        out_specs=vblk,
        out_shape=jax.ShapeDtypeStruct((b, s, vw), BF16),
        scratch_shapes=[pltpu.VMEM((hps, GLA_DK, GLA_DV), F32)],
        compiler_params=_params("parallel", "parallel", "arbitrary"),
        name="gla",
    )(qt, ktt, kdt, v, r, dec, gain)


def _rope_tables(seq, tile):
    half = RET_HEAD_DIM // 2
    inv = ROPE_BASE ** (-np.arange(half, dtype=np.float64) / half)

    def both(pos):
        ang = pos[:, None] * inv[None, :]
        dup = lambda t: np.concatenate([t, t], axis=-1).astype(np.float32)
        return dup(np.cos(ang)), dup(np.sin(ang))

    cos_a, sin_a = both(np.arange(0, seq, tile, dtype=np.float64))
    cos_b, sin_b = both(np.arange(tile, dtype=np.float64))
    return (jnp.asarray(cos_a[:, None, :]), jnp.asarray(sin_a[:, None, :]),
            jnp.asarray(cos_b), jnp.asarray(sin_b))


def _retention_tables(span):
    c = RET_CHUNK
    lg = np.log(1.0 - np.exp2(-5.0 - np.arange(RET_HEADS, dtype=np.float64)))[:, None]
    pos = np.arange(span, dtype=np.float64)
    diff = pos[:, None] - pos[None, :]
    same = (pos[:, None] // c) == (pos[None, :] // c)
    decay = np.where((diff >= 0) & same, np.exp(lg[:, :, None] * np.maximum(diff, 0.0)), 0.0)
    pos = np.arange(c, dtype=np.float64)
    zeta = np.exp(lg * (c - 1 - pos))[:, None, :]
    xi = np.exp(lg * (pos + 1.0))[:, :, None]
    g_chunk = np.exp(lg * c)[:, :, None]
    return tuple(jnp.asarray(t.astype(np.float32)) for t in (decay, zeta, xi, g_chunk))


def _suffix_tri(n):
    idx = np.arange(n)
    return jnp.asarray(idx[:, None] >= idx[None, :], dtype=BF16)


def _chunk_prefix_tri(rows, chunk):
    idx = np.arange(rows)
    same = (idx[:, None] // chunk) == (idx[None, :] // chunk)
    return jnp.asarray(same & (idx[None, :] <= idx[:, None]), dtype=BF16)


def _gate_operands(w_gate_in, w_alpha):
    rank = GLA_GATE_RANK
    wa = jnp.pad(jnp.tile(w_gate_in, (1, 3)), ((0, 0), (0, LANES - 3 * rank))).astype(BF16)
    w_hi = w_alpha.astype(BF16)
    w_lo = (w_alpha - w_hi.astype(F32)).astype(BF16)
    walpha = jnp.pad(jnp.concatenate([w_hi, w_hi, w_lo], axis=0), ((0, LANES - 3 * rank), (0, 0)))
    return wa, walpha


def kernel(x, mix_norm_g, even_w_in, sb_q_gain, sb_k_gain, ret_out_gain, even_w_out,
           odd_w_in, gla_w_alpha, gla_b_alpha, gla_out_gain, odd_w_out,
           ffn_norm_g, ffn_w_up, ffn_conv_w, ffn_conv_b, ffn_w_down):
    b, s, d = x.shape
    t = b * s
    depth = mix_norm_g.shape[0]
    h = x.reshape(t, d)

    rope_tables = _rope_tables(s, min(ROW_TILE, s))
    decay, zeta, xi, g_chunk = _retention_tables(min(RET_SPAN, s))
    sb_blk = min(SB_BLOCK, s)
    sb_tri = _suffix_tri(sb_blk)
    gla_tri = _chunk_prefix_tri(min(GLA_SPAN, s), GLA_CHUNK)

    even_w_in, even_w_out, odd_w_in_b, odd_w_out, ffn_w_up, ffn_w_down = (
        w.astype(BF16) for w in (even_w_in, even_w_out, odd_w_in, odd_w_out, ffn_w_up, ffn_w_down))

    for layer in range(depth):
        g_mix = mix_norm_g[layer][None, :]
        if layer % 2 == 0:
            e = layer // 2
            qg = jnp.tile(sb_q_gain[e], 2)[None, :]
            kg = jnp.tile(sb_k_gain[e], 2)[None, :]
            sbq, sbkt, sbv, rq, rkt, rv, rg = _even_in(
                h, g_mix, even_w_in, e, qg, kg, rope_tables, b, s, sb_blk)
            out_a = _sb_attention(sbq.reshape(b, s, SB_WIDTH), sbkt, sbv.reshape(b, s, SB_WIDTH),
                                  sb_tri, qg, kg)
            shp = (b, s, RET_WIDTH)
            out_b = _retention(rq.reshape(shp), rkt, rv.reshape(shp), rg.reshape(shp),
                               ret_out_gain[e][None, :], decay, zeta, xi, g_chunk)
            mix = (out_a.reshape(t, SB_WIDTH), out_b.reshape(t, RET_WIDTH))
            w_out, w_out_layer = even_w_out, e
        else:
            o = layer // 2
            n_main = 2 * GLA_K_WIDTH + 2 * GLA_V_WIDTH
            wa, walpha = _gate_operands(odd_w_in[o, :, n_main:], gla_w_alpha[o])
            qt, ktt, kdt, gv, gr, dec = _odd_in(h, g_mix, odd_w_in_b, o, wa, walpha,
                                                gla_b_alpha[o][None, :], gla_tri, b, s)
            og = _gla(qt.reshape(b, s, GLA_K_WIDTH), ktt, kdt,
                      gv.reshape(b, s, GLA_V_WIDTH), gr.reshape(b, s, GLA_V_WIDTH),
                      dec.reshape(b, s // GLA_CHUNK, GLA_K_WIDTH), gla_out_gain[o][None, :])
            mix = (og.reshape(t, GLA_V_WIDTH),)
            w_out, w_out_layer = odd_w_out, o
        h = _ffn(h, mix, w_out, w_out_layer, ffn_norm_g[layer][None, :], ffn_w_up,
                 ffn_conv_w[layer], ffn_conv_b[layer][None, :], ffn_w_down, layer, s)
    return h.reshape(b, s, d)
```

```python
import functools

import numpy as np

import jax
import jax.numpy as jnp
from jax import lax
from jax.experimental import pallas as pl
from jax.experimental.pallas import tpu as pltpu

F32 = jnp.float32
BF16 = jnp.bfloat16

EPS = 1e-6
LOG2E = 1.4426950408889634
ROPE_BASE = 10000.0
EXP2_UNDERFLOW = 150.0
SB_NORM_SLACK = 1.05
SOFTPLUS2_LINEAR_ABOVE = 32.0
MASKED_SCORE = -1e30

SB_HEADS = 8
SB_HEAD_DIM = 64
SB_WIDTH = SB_HEADS * SB_HEAD_DIM
RET_HEADS = 4
RET_HEAD_DIM = 128
RET_WIDTH = RET_HEADS * RET_HEAD_DIM
RET_CHUNK = 128
GLA_HEADS = 4
GLA_DK = 128
GLA_DV = 256
GLA_K_WIDTH = GLA_HEADS * GLA_DK
GLA_V_WIDTH = GLA_HEADS * GLA_DV
GLA_GATE_RANK = 16
GLA_GATE_TAU = 16.0
GLA_CHUNK = 64
CONV_WIDTH = 3

LANES = 128
SUBLANES = 8
MXU_DIM = 256
VMEM_LIMIT = 56 * 1024 * 1024

ROW_TILE = 1024
FFN_ROW_TILE = 1024
SB_BLOCK = MXU_DIM
SB_QBLOCKS_PER_STEP = 8
SB_RECENT_ROWS = 176
RET_HEADS_PER_STEP = 4
RET_GROUP = 1024
RET_SPAN = 2 * RET_CHUNK
GLA_GROUP = 512
GLA_HEADS_PER_STEP = 4
GLA_SPAN = 4 * GLA_CHUNK
FF_CHUNK = 256


def _params(*sem):
    return pltpu.CompilerParams(dimension_semantics=sem, vmem_limit_bytes=VMEM_LIMIT)


def _const_spec(shape):
    n = len(shape)
    return pl.BlockSpec(shape, lambda *_: (0,) * n)


def _layer_spec(stacked_shape, layer):
    zeros = (0,) * (len(stacked_shape) - 1)
    return pl.BlockSpec((None,) + tuple(stacked_shape[1:]), lambda *_: (layer,) + zeros,
                        pipeline_mode=pl.Buffered(1))


def _rms_rows(x, g):
    ms = jnp.mean(x * x, axis=-1, keepdims=True)
    return x * lax.rsqrt(ms + EPS) * g


def _silu(x):
    return x / (1.0 + jnp.exp2(x * -LOG2E))


def _half_head_rms(blk):
    sq = blk * blk
    lane = lax.broadcasted_iota(jnp.int32, blk.shape, 1)
    lo = lane < SB_HEAD_DIM
    s_lo = jnp.sum(jnp.where(lo, sq, 0.0), axis=-1, keepdims=True)
    s_hi = jnp.sum(jnp.where(lo, 0.0, sq), axis=-1, keepdims=True)
    ms = jnp.where(lo, s_lo, s_hi) * (1.0 / SB_HEAD_DIM)
    return blk * lax.rsqrt(ms + EPS)


def _even_in_kernel(x_ref, g_ref, w_ref, qg_ref, kg_ref, ca_ref, sa_ref, cb_ref, sb_ref,
                    sbq_ref, sbkt_ref, sbv_ref, rq_ref, rkt_ref, rv_ref, rg_ref):
    tm = x_ref.shape[0]
    hn = _rms_rows(x_ref[...], g_ref[...]).astype(BF16)

    def proj(seg):
        return jnp.dot(hn, w_ref[:, seg * SB_WIDTH:(seg + 1) * SB_WIDTH],
                       preferred_element_type=F32)

    p = proj(1)
    blk = sbkt_ref.shape[-1]
    for hp in range(SB_WIDTH // LANES):
        sl = slice(hp * LANES, (hp + 1) * LANES)
        kt = (_half_head_rms(p[:, sl]) * kg_ref[...]).T
        for c in range(tm // blk):
            sbkt_ref[hp, c] = kt[:, c * blk:(c + 1) * blk].astype(BF16)

    cos_a, sin_a, cos_b, sin_b = ca_ref[...], sa_ref[...], cb_ref[...], sb_ref[...]
    sign = jnp.where(lax.broadcasted_iota(jnp.int32, (1, LANES), 1) < RET_HEAD_DIM // 2, -1.0, 1.0)
    cos2 = cos_a * cos_b - sin_a * sin_b
    sin2 = sign * (sin_a * cos_b + cos_a * sin_b)

    def rope(blk):
        return blk * cos2 + pltpu.roll(blk, RET_HEAD_DIM // 2, 1) * sin2

    p = proj(4)
    for h in range(RET_HEADS):
        sl = slice(h * LANES, (h + 1) * LANES)
        rkt_ref[h] = (rope(p[:, sl]) * RET_HEAD_DIM ** -0.5).T.astype(BF16)
    q_scale = SB_HEAD_DIM ** -0.5 * LOG2E
    p = proj(0)
    for hp in range(SB_WIDTH // LANES):
        sl = slice(hp * LANES, (hp + 1) * LANES)
        sbq_ref[:, sl] = (_half_head_rms(p[:, sl]) * (qg_ref[...] * q_scale)).astype(BF16)
    p = proj(3)
    for h in range(RET_HEADS):
        sl = slice(h * LANES, (h + 1) * LANES)
        rq_ref[:, sl] = rope(p[:, sl]).astype(BF16)
    rg_ref[...] = _silu(proj(6)).astype(BF16)
    sbv_ref[...] = proj(2).astype(BF16)
    rv_ref[...] = proj(5).astype(BF16)


def _even_in(x2, g, w_stack, layer, qg, kg, rope_tables, batch, seq, sb_blk):
    t, d = x2.shape
    tm = min(ROW_TILE, seq)
    tps = seq // tm
    tile_start = pl.BlockSpec((None, 1, LANES), lambda i: (i % tps, 0, 0))
    row = lambda i: (i, 0)
    pairs = SB_WIDTH // LANES
    flat = jax.ShapeDtypeStruct((t, SB_WIDTH), BF16)
    flat_spec = pl.BlockSpec((tm, SB_WIDTH), row)
    sbkt = jax.ShapeDtypeStruct((batch, pairs, seq // sb_blk, LANES, sb_blk), BF16)
    sbkt_spec = pl.BlockSpec((None, pairs, tm // sb_blk, LANES, sb_blk),
                             lambda i: (i // tps, 0, i % tps, 0, 0))
    rkt = jax.ShapeDtypeStruct((batch, RET_HEADS, RET_HEAD_DIM, seq), BF16)
    rkt_spec = pl.BlockSpec((None, RET_HEADS, RET_HEAD_DIM, tm), lambda i: (i // tps, 0, 0, i % tps))
    return pl.pallas_call(
        _even_in_kernel,
        grid=(t // tm,),
        in_specs=[
            pl.BlockSpec((tm, d), row),
            _const_spec((1, d)),
            _layer_spec(w_stack.shape, layer),
            _const_spec((1, LANES)),
            _const_spec((1, LANES)),
            tile_start,
            tile_start,
            _const_spec((tm, LANES)),
            _const_spec((tm, LANES)),
        ],
        out_specs=[flat_spec, sbkt_spec, flat_spec, flat_spec, rkt_spec, flat_spec, flat_spec],
        out_shape=[flat, sbkt, flat, flat, rkt, flat, flat],
        compiler_params=_params("parallel"),
        name="even_in",
    )(x2, g, w_stack, qg, kg, *rope_tables)


def _softplus2(z):
    return jnp.where(z > SOFTPLUS2_LINEAR_ABOVE, z, jnp.log2(1.0 + jnp.exp2(z)))


def _sb_kernel(q_ref, kt_ref, v_ref, tri_ref, qg_ref, kg_ref, o_ref, acc_ref, carry_ref):
    step = pl.program_id(2)
    blk = kt_ref.shape[-1]
    per_step = q_ref.shape[0] // blk
    z_bound = (SB_NORM_SLACK * SB_HEAD_DIM * SB_HEAD_DIM ** -0.5 * LOG2E
               * jnp.max(jnp.abs(qg_ref[...])) * jnp.max(jnp.abs(kg_ref[...])))
    dead_above = z_bound + EXP2_UNDERFLOW
    lane = lax.broadcasted_iota(jnp.int32, (blk, LANES), 1)
    lo = lane < SB_HEAD_DIM
    row_id = lax.broadcasted_iota(jnp.int32, (blk, blk), 0)
    col_id = lax.broadcasted_iota(jnp.int32, (blk, blk), 1)
    below_diag = col_id < row_id
    tri = tri_ref[...]
    heads = range(2)

    def split_heads(q):
        zero = jnp.zeros_like(q)
        return (jnp.where(lo, q, zero), jnp.where(lo, zero, q))

    def key_blocks(j_a):
        j_b = jnp.maximum(j_a - 1, 0)
        return j_a > 0, j_b

    def values(j):
        return v_ref[pl.ds(pl.multiple_of(j * blk, blk), blk), :]

    def write_out(n):
        o_ref[n * blk:(n + 1) * blk, :] = jnp.where(lo, acc_ref[n, 0], acc_ref[n, 1]).astype(o_ref.dtype)

    def diagonal_pairs(blocks, b_rows, write_outputs):
        chains = [(n, h) for n, _ in blocks for h in heads]
        j_as = {n: j_a for n, j_a in blocks}
        top = slice(0, b_rows)
        has_b, kt_a, kt_b, v_a, v_b = {}, {}, {}, {}, {}
        for n, j_a in blocks:
            has_b[n], j_b = key_blocks(j_a)
            kt_a[n] = kt_ref[j_a]
            kt_b[n] = kt_ref[j_b]
            v_a[n] = values(j_a)
            v_b[n] = values(j_b)
        z_a = {(n, h): jnp.dot(q_heads[n][h], kt_a[n], preferred_element_type=F32) for n, h in chains}
        z_b = {(n, h): jnp.dot(q_heads[n][h][top], kt_b[n], preferred_element_type=F32)
               for n, h in chains}
        z_a = {c: jnp.where(below_diag, z, MASKED_SCORE) for c, z in z_a.items()}
        sp_a = {c: _softplus2(z) for c, z in z_a.items()}
        sp_b = {c: _softplus2(z) for c, z in z_b.items()}
        cs_a = {c: jnp.dot(sp.astype(BF16), tri, preferred_element_type=F32) for c, sp in sp_a.items()}
        cs_b = {c: jnp.dot(sp.astype(BF16), tri, preferred_element_type=F32) for c, sp in sp_b.items()}
        tot_a, tot_b, w_a, w_b = {}, {}, {}, {}
        for n, h in chains:
            c = (n, h)
            tot_a[c] = jnp.sum(sp_a[c], axis=-1, keepdims=True)
            tot_b[c] = jnp.where(has_b[n], jnp.sum(sp_b[c], axis=-1, keepdims=True), 0.0)
            before_b = jnp.where(has_b[n], tot_a[c][top], -MASKED_SCORE)
            w_a[c] = jnp.exp2(z_a[c] - cs_a[c])
            w_b[c] = jnp.exp2(z_b[c] - (cs_b[c] + before_b))
        for n, h in chains:
            c = (n, h)
            pv_a = jnp.dot(w_a[c].astype(BF16), v_a[n], preferred_element_type=F32)
            pv_b = jnp.dot(w_b[c].astype(BF16), v_b[n], preferred_element_type=F32)
            acc_ref[n, h, top] = pv_a[top] + pv_b
            carry_ref[n, h, top] = tot_a[c][top] + tot_b[c]
            if b_rows < blk:
                acc_ref[n, h, b_rows:] = pv_a[b_rows:]
                carry_ref[n, h, b_rows:] = tot_a[c][b_rows:]
        if write_outputs:
            for n, _ in blocks:
                write_out(n)

    first = step * per_step
    q_heads = [split_heads(q_ref[n * blk:(n + 1) * blk, :]) for n in range(per_step)]
    recent_rows = min(SB_RECENT_ROWS, blk)
    diagonal_pairs([(n, first + n) for n in range(per_step)], recent_rows, True)

    def older_pair(n, j_a):
        has_b, j_b = key_blocks(j_a)
        kt_a, kt_b = kt_ref[j_a], kt_ref[j_b]
        v_a, v_b = values(j_a), values(j_b)
        z_a = [jnp.dot(q_heads[n][h], kt_a, preferred_element_type=F32) for h in heads]
        z_b = [jnp.dot(q_heads[n][h], kt_b, preferred_element_type=F32) for h in heads]
        sp_a = [_softplus2(z) for z in z_a]
        sp_b = [_softplus2(z) for z in z_b]
        cs_a = [jnp.dot(sp.astype(BF16), tri, preferred_element_type=F32) for sp in sp_a]
        cs_b = [jnp.dot(sp.astype(BF16), tri, preferred_element_type=F32) for sp in sp_b]
        for h in heads:
            carry = carry_ref[n, h]
            after_a = carry + jnp.sum(sp_a[h], axis=-1, keepdims=True)
            tot_b = jnp.where(has_b, jnp.sum(sp_b[h], axis=-1, keepdims=True), 0.0)
            before_b = jnp.where(has_b, after_a, -MASKED_SCORE)
            w_a = jnp.exp2(z_a[h] - (cs_a[h] + carry))
            w_b = jnp.exp2(z_b[h] - (cs_b[h] + before_b))
            acc_ref[n, h] += (jnp.dot(w_a.astype(BF16), v_a, preferred_element_type=F32)
                              + jnp.dot(w_b.astype(BF16), v_b, preferred_element_type=F32))
            carry_ref[n, h] = after_a + tot_b

    def live(n):
        return jnp.minimum(jnp.min(carry_ref[n, 0]), jnp.min(carry_ref[n, 1])) <= dead_above

    row = lax.broadcasted_iota(jnp.int32, (blk, 1), 0)
    first_unfinished_block = jnp.where(row >= recent_rows, 1, 2)
    least = None
    for n in range(per_step):
        has_unvisited = first + n >= first_unfinished_block
        for h in heads:
            candidate = jnp.where(has_unvisited, carry_ref[n, h], -MASKED_SCORE)
            least = candidate if least is None else jnp.minimum(least, candidate)
    any_live = jnp.min(least) <= dead_above

    @pl.when(any_live)
    def _():
        for n in range(per_step):
            diagonal_pairs([(n, first + n)], blk, False)

            def body(c, n=n):
                older_pair(n, c[0])
                return c[0] - 2, live(n)

            lax.while_loop(lambda c: jnp.logical_and(c[0] >= 0, c[1]), body, (first + n - 2, live(n)))
            write_out(n)


def _sb_attention(q, kt, v, tri, qg, kg):
    b, s, width = q.shape
    blk = kt.shape[-1]
    n_blocks = s // blk
    per_step = min(SB_QBLOCKS_PER_STEP, n_blocks)
    assert n_blocks % per_step == 0
    rows = per_step * blk
    return pl.pallas_call(
        _sb_kernel,
        grid=(b, width // LANES, n_blocks // per_step),
        in_specs=[
            pl.BlockSpec((None, rows, LANES), lambda bi, hp, i: (bi, i, hp)),
            pl.BlockSpec((None, None, n_blocks, LANES, blk), lambda bi, hp, i: (bi, hp, 0, 0, 0)),
            pl.BlockSpec((None, s, LANES), lambda bi, hp, i: (bi, 0, hp)),
            _const_spec(tri.shape),
            _const_spec((1, LANES)),
            _const_spec((1, LANES)),
        ],
        out_specs=pl.BlockSpec((None, rows, LANES), lambda bi, hp, i: (bi, i, hp)),
        out_shape=jax.ShapeDtypeStruct((b, s, width), BF16),
        scratch_shapes=[pltpu.VMEM((per_step, 2, blk, LANES), F32),
                        pltpu.VMEM((per_step, 2, blk, 1), F32)],
        compiler_params=_params("parallel", "parallel", "arbitrary"),
        name="sb_attention",
    )(q, kt, v, tri, qg, kg)


def _ret_kernel(q_ref, kt_ref, v_ref, g_ref, gain_ref, decay_ref, zeta_ref, xi_ref, gc_ref,
                o_ref, state_ref):
    @pl.when(pl.program_id(2) == 0)
    def _():
        state_ref[...] = jnp.zeros_like(state_ref)

    c = RET_CHUNK
    d = RET_HEAD_DIM
    span = decay_ref.shape[-1]
    rows = q_ref.shape[0]
    gain = gain_ref[...]
    for hh in range(state_ref.shape[0]):
        lanes = slice(hh * d, (hh + 1) * d)
        decay = decay_ref[hh]
        zeta = zeta_ref[hh]
        xi = xi_ref[hh]
        g_chunk = gc_ref[hh]
        q = [q_ref[m * span:(m + 1) * span, lanes] for m in range(rows // span)]
        kt = [kt_ref[hh, :, m * span:(m + 1) * span] for m in range(rows // span)]
        v = [v_ref[m * span:(m + 1) * span, lanes] for m in range(rows // span)]
        scores = [jnp.dot(q[m], kt[m], preferred_element_type=F32) * decay
                  for m in range(rows // span)]
        inner = [jnp.dot(scores[m].astype(BF16), v[m], preferred_element_type=F32)
                 for m in range(rows // span)]
        contrib = []
        for n in range(rows // c):
            m, sub = divmod(n * c, span)
            kz = (kt[m][:, sub:sub + c].astype(F32) * zeta).astype(BF16)
            contrib.append(jnp.dot(kz, v[m][sub:sub + c], preferred_element_type=F32))
        state = state_ref[hh]
        for n in range(rows // c):
            m, sub = divmod(n * c, span)
            cross = jnp.dot(q[m][sub:sub + c], state.astype(BF16), preferred_element_type=F32) * xi
            state = g_chunk * state + contrib[n]
            ret = _rms_rows(inner[m][sub:sub + c] + cross, gain)
            gate = g_ref[n * c:(n + 1) * c, lanes].astype(F32)
            o_ref[n * c:(n + 1) * c, lanes] = (ret * gate).astype(o_ref.dtype)
        state_ref[hh] = state


def _retention(q, kt, v, g, gain, decay, zeta, xi, gc):
    b, s, width = q.shape
    rows = min(RET_GROUP, s)
    span = decay.shape[-1]
    hps = RET_HEADS_PER_STEP
    blk = pl.BlockSpec((None, rows, hps * RET_HEAD_DIM), lambda bi, h, n: (bi, n, h))
    head = lambda shape: pl.BlockSpec((hps,) + shape, lambda bi, h, n: (h, 0, 0))
    return pl.pallas_call(
        _ret_kernel,
        grid=(b, RET_HEADS // hps, s // rows),
        in_specs=[blk,
                  pl.BlockSpec((None, hps, RET_HEAD_DIM, rows), lambda bi, h, n: (bi, h, 0, n)),
                  blk, blk, _const_spec((1, RET_HEAD_DIM)),
                  head((span, span)), head((1, RET_CHUNK)), head((RET_CHUNK, 1)), head((1, 1))],
        out_specs=blk,
        out_shape=jax.ShapeDtypeStruct((b, s, width), BF16),
        scratch_shapes=[pltpu.VMEM((hps, RET_HEAD_DIM, RET_HEAD_DIM), F32)],
        compiler_params=_params("parallel", "parallel", "arbitrary"),
        name="retention",
    )(q, kt, v, g, gain, decay, zeta, xi, gc)


def _ffn_kernel(*refs, n_mix, tiles_per_seq):
    h_ref = refs[0]
    mix_refs = refs[1:1 + n_mix]
    (wo_ref, g_ref, wup_ref, cw_ref, cb_ref, wd_ref, o_ref,
     ubuf_ref, prev_ref, gate_ref) = refs[1 + n_mix:]
    tm = h_ref.shape[0]
    d_ff = wd_ref.shape[0]
    halo = SUBLANES

    @pl.when(pl.program_id(0) % tiles_per_seq == 0)
    def _():
        prev_ref[...] = jnp.zeros_like(prev_ref)

    h = h_ref[...]
    row0 = 0
    for a_ref in mix_refs:
        width = a_ref.shape[1]
        h = h + jnp.dot(a_ref[...], wo_ref[row0:row0 + width, :], preferred_element_type=F32)
        row0 += width
    hn = _rms_rows(h, g_ref[...]).astype(BF16)
    for c0 in range(0, d_ff, FF_CHUNK):
        cols = slice(c0, c0 + FF_CHUNK)
        u = jnp.dot(hn, wup_ref[:, cols], preferred_element_type=F32)
        v = jnp.dot(hn, wup_ref[:, d_ff + c0:d_ff + c0 + FF_CHUNK], preferred_element_type=F32)
        ubuf_ref[0:halo, :] = prev_ref[:, cols]
        ubuf_ref[halo:halo + tm, :] = u
        prev_ref[:, cols] = u[tm - halo:, :]
        last_tap = CONV_WIDTH - 1
        uc = cb_ref[:, cols] + cw_ref[last_tap:CONV_WIDTH, cols] * u
        for tap in range(last_tap):
            back = last_tap - tap
            uc = uc + cw_ref[tap:tap + 1, cols] * ubuf_ref[halo - back:halo - back + tm, :]
        gate_ref[:, cols] = (_silu(uc) * v).astype(BF16)
    o_ref[...] = h + jnp.dot(gate_ref[...], wd_ref[...], preferred_element_type=F32)


def _ffn(h2, mix, wo_stack, wo_layer, g, wup_stack, cw, cb, wd_stack, layer, seq):
    t, d = h2.shape
    d_ff = wd_stack.shape[1]
    tm = min(FFN_ROW_TILE, seq)
    row = lambda i: (i, 0)
    return pl.pallas_call(
        functools.partial(_ffn_kernel, n_mix=len(mix), tiles_per_seq=seq // tm),
        grid=(t // tm,),
        in_specs=([pl.BlockSpec((tm, d), row)]
                  + [pl.BlockSpec((tm, a.shape[1]), row) for a in mix]
                  + [_layer_spec(wo_stack.shape, wo_layer), _const_spec((1, d)),
                     _layer_spec(wup_stack.shape, layer), _const_spec(cw.shape),
                     _const_spec(cb.shape), _layer_spec(wd_stack.shape, layer)]),
        out_specs=pl.BlockSpec((tm, d), row),
        out_shape=jax.ShapeDtypeStruct((t, d), F32),
        scratch_shapes=[pltpu.VMEM((tm + SUBLANES, FF_CHUNK), F32),
                        pltpu.VMEM((SUBLANES, d_ff), F32),
                        pltpu.VMEM((tm, d_ff), BF16)],
        compiler_params=_params("arbitrary"),
        name="conv_ffn",
    )(h2, *mix, wo_stack, g, wup_stack, cw, cb, wd_stack)


def _log_sigmoid(x):
    return jnp.minimum(x, 0.0) - jnp.log(1.0 + jnp.exp(-jnp.abs(x)))


def _odd_in_kernel(x_ref, g_ref, w_ref, wa_ref, walpha_ref, balpha_ref, tri_ref,
                   qt_ref, ktt_ref, kdt_ref, v_ref, r_ref, dec_ref):
    tm = x_ref.shape[0]
    span = tri_ref.shape[0]
    hn = _rms_rows(x_ref[...], g_ref[...]).astype(BF16)

    def proj(c0, width):
        return jnp.dot(hn, w_ref[:, c0:c0 + width], preferred_element_type=F32)

    ga = jnp.dot(hn, wa_ref[...], preferred_element_type=F32)
    v_ref[...] = proj(2 * GLA_K_WIDTH, GLA_V_WIDTH).astype(BF16)
    r_ref[...] = _silu(proj(2 * GLA_K_WIDTH + GLA_V_WIDTH, GLA_V_WIDTH)).astype(BF16)
    ga_hi = ga.astype(BF16)
    ga_lo = (ga - ga_hi.astype(F32)).astype(BF16)
    lane = lax.broadcasted_iota(jnp.int32, ga.shape, 1)
    middle = jnp.logical_and(lane >= GLA_GATE_RANK, lane < 2 * GLA_GATE_RANK)
    pre = jnp.dot(jnp.where(middle, ga_lo, ga_hi), walpha_ref[...],
                  preferred_element_type=F32) + balpha_ref[...]
    la = _log_sigmoid(pre) * (LOG2E / GLA_GATE_TAU)
    la_hi = la.astype(BF16)
    la_lo = (la - la_hi.astype(F32)).astype(BF16)
    tri = tri_ref[...]
    cum = jnp.concatenate(
        [jnp.dot(tri, la_hi[r0:r0 + span], preferred_element_type=F32)
         + jnp.dot(tri, la_lo[r0:r0 + span], preferred_element_type=F32)
         for r0 in range(0, tm, span)], axis=0)
    chunks = tm // GLA_CHUNK
    last = cum.reshape(chunks, GLA_CHUNK, GLA_K_WIDTH)[:, GLA_CHUNK - 1:GLA_CHUNK, :]
    dec_ref[...] = jnp.exp2(last).reshape(chunks, GLA_K_WIDTH)
    to_end = (jnp.broadcast_to(last, (chunks, GLA_CHUNK, GLA_K_WIDTH)).reshape(tm, GLA_K_WIDTH)
              - cum)

    k = proj(GLA_K_WIDTH, GLA_K_WIDTH)
    kt = k * jnp.exp2(-cum)
    kd = k * jnp.exp2(to_end)
    for h in range(GLA_HEADS):
        sl = slice(h * GLA_DK, (h + 1) * GLA_DK)
        ktt_ref[h] = kt[:, sl].T.astype(BF16)
        kdt_ref[h] = kd[:, sl].T.astype(BF16)
    qt_ref[...] = (proj(0, GLA_K_WIDTH) * GLA_DK ** -0.5 * jnp.exp2(cum)).astype(BF16)


def _odd_in(x2, g, w_stack, layer, wa, walpha, balpha, tri, batch, seq):
    t, d = x2.shape
    tm = min(ROW_TILE, seq)
    tps = seq // tm
    row = lambda i: (i, 0)
    kt_shape = jax.ShapeDtypeStruct((batch, GLA_HEADS, GLA_DK, seq), BF16)
    kt_spec = pl.BlockSpec((None, GLA_HEADS, GLA_DK, tm), lambda i: (i // tps, 0, 0, i % tps))
    vout = jax.ShapeDtypeStruct((t, GLA_V_WIDTH), BF16)
    return pl.pallas_call(
        _odd_in_kernel,
        grid=(t // tm,),
        in_specs=[pl.BlockSpec((tm, d), row), _const_spec((1, d)), _layer_spec(w_stack.shape, layer),
                  _const_spec(wa.shape), _const_spec(walpha.shape), _const_spec(balpha.shape),
                  _const_spec(tri.shape)],
        out_specs=[pl.BlockSpec((tm, GLA_K_WIDTH), row), kt_spec, kt_spec,
                   pl.BlockSpec((tm, GLA_V_WIDTH), row), pl.BlockSpec((tm, GLA_V_WIDTH), row),
                   pl.BlockSpec((tm // GLA_CHUNK, GLA_K_WIDTH), row)],
        out_shape=[jax.ShapeDtypeStruct((t, GLA_K_WIDTH), BF16), kt_shape, kt_shape, vout, vout,
                   jax.ShapeDtypeStruct((t // GLA_CHUNK, GLA_K_WIDTH), F32)],
        compiler_params=_params("parallel"),
        name="odd_in",
    )(x2, g, w_stack, wa, walpha, balpha, tri)


def _gla_kernel(qt_ref, ktt_ref, kdt_ref, v_ref, r_ref, dec_ref, gain_ref, o_ref, state_ref):
    @pl.when(pl.program_id(2) == 0)
    def _():
        state_ref[...] = jnp.zeros_like(state_ref)

    c = GLA_CHUNK
    rows = qt_ref.shape[0]
    span = min(GLA_SPAN, rows)
    ri = lax.broadcasted_iota(jnp.int32, (span, span), 0)
    ci = lax.broadcasted_iota(jnp.int32, (span, span), 1)
    causal = jnp.logical_and(ci <= ri, ci // c == ri // c)
    col_chunk = lax.broadcasted_iota(jnp.int32, (1, span), 1) // c
    gain = gain_ref[...]
    for hh in range(state_ref.shape[0]):
        klanes = slice(hh * GLA_DK, (hh + 1) * GLA_DK)
        vlanes = slice(hh * GLA_DV, (hh + 1) * GLA_DV)
        dec_t = dec_ref[:, klanes].T
        spans = range(rows // span)
        qt = [qt_ref[m * span:(m + 1) * span, klanes] for m in spans]
        ktt = [ktt_ref[hh, :, m * span:(m + 1) * span] for m in spans]
        kdt = [kdt_ref[hh, :, m * span:(m + 1) * span] for m in spans]
        v = [v_ref[m * span:(m + 1) * span, vlanes] for m in spans]
        a = [jnp.where(causal, jnp.dot(qt[m], ktt[m], preferred_element_type=F32), 0.0)
             for m in spans]
        intra = [jnp.dot(a[m].astype(BF16), v[m], preferred_element_type=F32) for m in spans]
        contrib = []
        for n in range(rows // c):
            m, sub = divmod(n * c, span)
            kd_n = jnp.where(col_chunk == sub // c, kdt[m], jnp.zeros_like(kdt[m]))
            contrib.append(jnp.dot(kd_n, v[m], preferred_element_type=F32))
        state = state_ref[hh]
        for n in range(rows // c):
            m, sub = divmod(n * c, span)
            inter = jnp.dot(qt[m][sub:sub + c], state.astype(BF16), preferred_element_type=F32)
            state = dec_t[:, n:n + 1] * state + contrib[n]
            o = _rms_rows(intra[m][sub:sub + c] + inter, gain)
            gate = r_ref[n * c:(n + 1) * c, vlanes].astype(F32)
            o_ref[n * c:(n + 1) * c, vlanes] = (o * gate).astype(o_ref.dtype)
        state_ref[hh] = state


def _gla(qt, ktt, kdt, v, r, dec, gain):
    b, s, kw = qt.shape
    vw = v.shape[-1]
    rows = min(GLA_GROUP, s)
    hps = GLA_HEADS_PER_STEP
    chunks = rows // GLA_CHUNK
    dec = dec.reshape(b, s // rows, chunks, kw)
    qblk = pl.BlockSpec((None, rows, hps * GLA_DK), lambda bi, h, n: (bi, n, h))
    tblk = pl.BlockSpec((None, hps, GLA_DK, rows), lambda bi, h, n: (bi, h, 0, n))
    vblk = pl.BlockSpec((None, rows, hps * GLA_DV), lambda bi, h, n: (bi, n, h))
    dblk = pl.BlockSpec((None, None, chunks, hps * GLA_DK), lambda bi, h, n: (bi, n, 0, h))
    return pl.pallas_call(
        _gla_kernel,
        grid=(b, GLA_HEADS // hps, s // rows),
        in_specs=[qblk, tblk, tblk, vblk, vblk, dblk, _const_spec((1, GLA_DV))],
        out_specs=vblk,
        out_shape=jax.ShapeDtypeStruct((b, s, vw), BF16),
        scratch_shapes=[pltpu.VMEM((hps, GLA_DK, GLA_DV), F32)],
        compiler_params=_params("parallel", "parallel", "arbitrary"),
        name="gla",
    )(qt, ktt, kdt, v, r, dec, gain)


def _rope_tables(seq, tile):
    half = RET_HEAD_DIM // 2
    inv = ROPE_BASE ** (-np.arange(half, dtype=np.float64) / half)

    def both(pos):
        ang = pos[:, None] * inv[None, :]
        dup = lambda t: np.concatenate([t, t], axis=-1).astype(np.float32)
        return dup(np.cos(ang)), dup(np.sin(ang))

    cos_a, sin_a = both(np.arange(0, seq, tile, dtype=np.float64))
    cos_b, sin_b = both(np.arange(tile, dtype=np.float64))
    return (jnp.asarray(cos_a[:, None, :]), jnp.asarray(sin_a[:, None, :]),
            jnp.asarray(cos_b), jnp.asarray(sin_b))


def _retention_tables(span):
    c = RET_CHUNK
    lg = np.log(1.0 - np.exp2(-5.0 - np.arange(RET_HEADS, dtype=np.float64)))[:, None]
    pos = np.arange(span, dtype=np.float64)
    diff = pos[:, None] - pos[None, :]
    same = (pos[:, None] // c) == (pos[None, :] // c)
    decay = np.where((diff >= 0) & same, np.exp(lg[:, :, None] * np.maximum(diff, 0.0)), 0.0)
    pos = np.arange(c, dtype=np.float64)
    zeta = np.exp(lg * (c - 1 - pos))[:, None, :]
    xi = np.exp(lg * (pos + 1.0))[:, :, None]
    g_chunk = np.exp(lg * c)[:, :, None]
    return tuple(jnp.asarray(t.astype(np.float32)) for t in (decay, zeta, xi, g_chunk))


def _suffix_tri(n):
    idx = np.arange(n)
    return jnp.asarray(idx[:, None] >= idx[None, :], dtype=BF16)


def _chunk_prefix_tri(rows, chunk):
    idx = np.arange(rows)
    same = (idx[:, None] // chunk) == (idx[None, :] // chunk)
    return jnp.asarray(same & (idx[None, :] <= idx[:, None]), dtype=BF16)


def _gate_operands(w_gate_in, w_alpha):
    rank = GLA_GATE_RANK
    wa = jnp.pad(jnp.tile(w_gate_in, (1, 3)), ((0, 0), (0, LANES - 3 * rank))).astype(BF16)
    w_hi = w_alpha.astype(BF16)
    w_lo = (w_alpha - w_hi.astype(F32)).astype(BF16)
    walpha = jnp.pad(jnp.concatenate([w_hi, w_hi, w_lo], axis=0), ((0, LANES - 3 * rank), (0, 0)))
    return wa, walpha


def kernel(x, mix_norm_g, even_w_in, sb_q_gain, sb_k_gain, ret_out_gain, even_w_out,
           odd_w_in, gla_w_alpha, gla_b_alpha, gla_out_gain, odd_w_out,
           ffn_norm_g, ffn_w_up, ffn_conv_w, ffn_conv_b, ffn_w_down):
    b, s, d = x.shape
    t = b * s
    depth = mix_norm_g.shape[0]
    h = x.reshape(t, d)

    rope_tables = _rope_tables(s, min(ROW_TILE, s))
    decay, zeta, xi, g_chunk = _retention_tables(min(RET_SPAN, s))
    sb_blk = min(SB_BLOCK, s)
    sb_tri = _suffix_tri(sb_blk)
    gla_tri = _chunk_prefix_tri(min(GLA_SPAN, s), GLA_CHUNK)

    even_w_in, even_w_out, odd_w_in_b, odd_w_out, ffn_w_up, ffn_w_down = (
        w.astype(BF16) for w in (even_w_in, even_w_out, odd_w_in, odd_w_out, ffn_w_up, ffn_w_down))

    for layer in range(depth):
        g_mix = mix_norm_g[layer][None, :]
        if layer % 2 == 0:
            e = layer // 2
            qg = jnp.tile(sb_q_gain[e], 2)[None, :]
            kg = jnp.tile(sb_k_gain[e], 2)[None, :]
            sbq, sbkt, sbv, rq, rkt, rv, rg = _even_in(
                h, g_mix, even_w_in, e, qg, kg, rope_tables, b, s, sb_blk)
            out_a = _sb_attention(sbq.reshape(b, s, SB_WIDTH), sbkt, sbv.reshape(b, s, SB_WIDTH),
                                  sb_tri, qg, kg)
            shp = (b, s, RET_WIDTH)
            out_b = _retention(rq.reshape(shp), rkt, rv.reshape(shp), rg.reshape(shp),
                               ret_out_gain[e][None, :], decay, zeta, xi, g_chunk)
            mix = (out_a.reshape(t, SB_WIDTH), out_b.reshape(t, RET_WIDTH))
            w_out, w_out_layer = even_w_out, e
        else:
            o = layer // 2
            n_main = 2 * GLA_K_WIDTH + 2 * GLA_V_WIDTH
            wa, walpha = _gate_operands(odd_w_in[o, :, n_main:], gla_w_alpha[o])
            qt, ktt, kdt, gv, gr, dec = _odd_in(h, g_mix, odd_w_in_b, o, wa, walpha,
                                                gla_b_alpha[o][None, :], gla_tri, b, s)
            og = _gla(qt.reshape(b, s, GLA_K_WIDTH), ktt, kdt,
                      gv.reshape(b, s, GLA_V_WIDTH), gr.reshape(b, s, GLA_V_WIDTH),
                      dec.reshape(b, s // GLA_CHUNK, GLA_K_WIDTH), gla_out_gain[o][None, :])
            mix = (og.reshape(t, GLA_V_WIDTH),)
            w_out, w_out_layer = odd_w_out, o
        h = _ffn(h, mix, w_out, w_out_layer, ffn_norm_g[layer][None, :], ffn_w_up,
                 ffn_conv_w[layer], ffn_conv_b[layer][None, :], ffn_w_down, layer, s)
    return h.reshape(b, s, d)
```

```python
import functools

import numpy as np

import jax
import jax.numpy as jnp
from jax import lax
from jax.experimental import pallas as pl
from jax.experimental.pallas import tpu as pltpu

F32 = jnp.float32
BF16 = jnp.bfloat16

EPS = 1e-6
LOG2E = 1.4426950408889634
ROPE_BASE = 10000.0
EXP2_UNDERFLOW = 150.0
SB_NORM_SLACK = 1.05
SOFTPLUS2_LINEAR_ABOVE = 32.0
MASKED_SCORE = -1e30

SB_HEADS = 8
SB_HEAD_DIM = 64
SB_WIDTH = SB_HEADS * SB_HEAD_DIM
RET_HEADS = 4
RET_HEAD_DIM = 128
RET_WIDTH = RET_HEADS * RET_HEAD_DIM
RET_CHUNK = 128
GLA_HEADS = 4
GLA_DK = 128
GLA_DV = 256
GLA_K_WIDTH = GLA_HEADS * GLA_DK
GLA_V_WIDTH = GLA_HEADS * GLA_DV
GLA_GATE_RANK = 16
GLA_GATE_TAU = 16.0
GLA_CHUNK = 64
CONV_WIDTH = 3

LANES = 128
SUBLANES = 8
MXU_DIM = 256
VMEM_LIMIT = 56 * 1024 * 1024

ROW_TILE = 1024
FFN_ROW_TILE = 1024
SB_BLOCK = MXU_DIM
SB_QBLOCKS_PER_STEP = 8
SB_RECENT_ROWS = 176
RET_HEADS_PER_STEP = 4
RET_GROUP = 2048
RET_SPAN = 2 * RET_CHUNK
GLA_GROUP = 512
GLA_HEADS_PER_STEP = 4
GLA_SPAN = 4 * GLA_CHUNK
FF_CHUNK = 256


def _params(*sem):
    return pltpu.CompilerParams(dimension_semantics=sem, vmem_limit_bytes=VMEM_LIMIT)


def _const_spec(shape):
    n = len(shape)
    return pl.BlockSpec(shape, lambda *_: (0,) * n)


def _layer_spec(stacked_shape, layer):
    zeros = (0,) * (len(stacked_shape) - 1)
    return pl.BlockSpec((None,) + tuple(stacked_shape[1:]), lambda *_: (layer,) + zeros,
                        pipeline_mode=pl.Buffered(1))


def _rms_rows(x, g):
    ms = jnp.mean(x * x, axis=-1, keepdims=True)
    return x * lax.rsqrt(ms + EPS) * g


def _silu(x):
    return x / (1.0 + jnp.exp2(x * -LOG2E))


def _half_head_rms(blk):
    sq = blk * blk
    lane = lax.broadcasted_iota(jnp.int32, blk.shape, 1)
    lo = lane < SB_HEAD_DIM
    s_lo = jnp.sum(jnp.where(lo, sq, 0.0), axis=-1, keepdims=True)
    s_hi = jnp.sum(jnp.where(lo, 0.0, sq), axis=-1, keepdims=True)
    ms = jnp.where(lo, s_lo, s_hi) * (1.0 / SB_HEAD_DIM)
    return blk * lax.rsqrt(ms + EPS)


def _even_in_kernel(x_ref, g_ref, w_ref, qg_ref, kg_ref, ca_ref, sa_ref, cb_ref, sb_ref,
                    sbq_ref, sbkt_ref, sbv_ref, rq_ref, rkt_ref, rv_ref, rg_ref):
    tm = x_ref.shape[0]
    hn = _rms_rows(x_ref[...], g_ref[...]).astype(BF16)

    def proj(seg):
        return jnp.dot(hn, w_ref[:, seg * SB_WIDTH:(seg + 1) * SB_WIDTH],
                       preferred_element_type=F32)

    p = proj(1)
    blk = sbkt_ref.shape[-1]
    for hp in range(SB_WIDTH // LANES):
        sl = slice(hp * LANES, (hp + 1) * LANES)
        kt = (_half_head_rms(p[:, sl]) * kg_ref[...]).T
        for c in range(tm // blk):
            sbkt_ref[hp, c] = kt[:, c * blk:(c + 1) * blk].astype(BF16)

    cos_a, sin_a, cos_b, sin_b = ca_ref[...], sa_ref[...], cb_ref[...], sb_ref[...]
    sign = jnp.where(lax.broadcasted_iota(jnp.int32, (1, LANES), 1) < RET_HEAD_DIM // 2, -1.0, 1.0)
    cos2 = cos_a * cos_b - sin_a * sin_b
    sin2 = sign * (sin_a * cos_b + cos_a * sin_b)

    def rope(blk):
        return blk * cos2 + pltpu.roll(blk, RET_HEAD_DIM // 2, 1) * sin2

    p = proj(4)
    for h in range(RET_HEADS):
        sl = slice(h * LANES, (h + 1) * LANES)
        rkt_ref[h] = (rope(p[:, sl]) * RET_HEAD_DIM ** -0.5).T.astype(BF16)
    q_scale = SB_HEAD_DIM ** -0.5 * LOG2E
    p = proj(0)
    for hp in range(SB_WIDTH // LANES):
        sl = slice(hp * LANES, (hp + 1) * LANES)
        sbq_ref[:, sl] = (_half_head_rms(p[:, sl]) * (qg_ref[...] * q_scale)).astype(BF16)
    p = proj(3)
    for h in range(RET_HEADS):
        sl = slice(h * LANES, (h + 1) * LANES)
        rq_ref[:, sl] = rope(p[:, sl]).astype(BF16)
    rg_ref[...] = _silu(proj(6)).astype(BF16)
    sbv_ref[...] = proj(2).astype(BF16)
    rv_ref[...] = proj(5).astype(BF16)


def _even_in(x2, g, w_stack, layer, qg, kg, rope_tables, batch, seq, sb_blk):
    t, d = x2.shape
    tm = min(ROW_TILE, seq)
    tps = seq // tm
    tile_start = pl.BlockSpec((None, 1, LANES), lambda i: (i % tps, 0, 0))
    row = lambda i: (i, 0)
    pairs = SB_WIDTH // LANES
    flat = jax.ShapeDtypeStruct((t, SB_WIDTH), BF16)
    flat_spec = pl.BlockSpec((tm, SB_WIDTH), row)
    sbkt = jax.ShapeDtypeStruct((batch, pairs, seq // sb_blk, LANES, sb_blk), BF16)
    sbkt_spec = pl.BlockSpec((None, pairs, tm // sb_blk, LANES, sb_blk),
                             lambda i: (i // tps, 0, i % tps, 0, 0))
    rkt = jax.ShapeDtypeStruct((batch, RET_HEADS, RET_HEAD_DIM, seq), BF16)
    rkt_spec = pl.BlockSpec((None, RET_HEADS, RET_HEAD_DIM, tm), lambda i: (i // tps, 0, 0, i % tps))
    return pl.pallas_call(
        _even_in_kernel,
        grid=(t // tm,),
        in_specs=[
            pl.BlockSpec((tm, d), row),
            _const_spec((1, d)),
            _layer_spec(w_stack.shape, layer),
            _const_spec((1, LANES)),
            _const_spec((1, LANES)),
            tile_start,
            tile_start,
            _const_spec((tm, LANES)),
            _const_spec((tm, LANES)),
        ],
        out_specs=[flat_spec, sbkt_spec, flat_spec, flat_spec, rkt_spec, flat_spec, flat_spec],
        out_shape=[flat, sbkt, flat, flat, rkt, flat, flat],
        compiler_params=_params("parallel"),
        name="even_in",
    )(x2, g, w_stack, qg, kg, *rope_tables)


def _softplus2(z):
    return jnp.where(z > SOFTPLUS2_LINEAR_ABOVE, z, jnp.log2(1.0 + jnp.exp2(z)))


def _sb_kernel(q_ref, kt_ref, v_ref, tri_ref, qg_ref, kg_ref, o_ref, acc_ref, carry_ref):
    step = pl.program_id(2)
    blk = kt_ref.shape[-1]
    per_step = q_ref.shape[0] // blk
    z_bound = (SB_NORM_SLACK * SB_HEAD_DIM * SB_HEAD_DIM ** -0.5 * LOG2E
               * jnp.max(jnp.abs(qg_ref[...])) * jnp.max(jnp.abs(kg_ref[...])))
    dead_above = z_bound + EXP2_UNDERFLOW
    lane = lax.broadcasted_iota(jnp.int32, (blk, LANES), 1)
    lo = lane < SB_HEAD_DIM
    row_id = lax.broadcasted_iota(jnp.int32, (blk, blk), 0)
    col_id = lax.broadcasted_iota(jnp.int32, (blk, blk), 1)
    below_diag = col_id < row_id
    tri = tri_ref[...]
    heads = range(2)

    def split_heads(q):
        zero = jnp.zeros_like(q)
        return (jnp.where(lo, q, zero), jnp.where(lo, zero, q))

    def key_blocks(j_a):
        j_b = jnp.maximum(j_a - 1, 0)
        return j_a > 0, j_b

    def values(j):
        return v_ref[pl.ds(pl.multiple_of(j * blk, blk), blk), :]

    def write_out(n):
        o_ref[n * blk:(n + 1) * blk, :] = jnp.where(lo, acc_ref[n, 0], acc_ref[n, 1]).astype(o_ref.dtype)

    def diagonal_pairs(blocks, b_rows, write_outputs):
        chains = [(n, h) for n, _ in blocks for h in heads]
        j_as = {n: j_a for n, j_a in blocks}
        top = slice(0, b_rows)
        has_b, kt_a, kt_b, v_a, v_b = {}, {}, {}, {}, {}
        for n, j_a in blocks:
            has_b[n], j_b = key_blocks(j_a)
            kt_a[n] = kt_ref[j_a]
            kt_b[n] = kt_ref[j_b]
            v_a[n] = values(j_a)
            v_b[n] = values(j_b)
        z_a = {(n, h): jnp.dot(q_heads[n][h], kt_a[n], preferred_element_type=F32) for n, h in chains}
        z_b = {(n, h): jnp.dot(q_heads[n][h][top], kt_b[n], preferred_element_type=F32)
               for n, h in chains}
        z_a = {c: jnp.where(below_diag, z, MASKED_SCORE) for c, z in z_a.items()}
        sp_a = {c: _softplus2(z) for c, z in z_a.items()}
        sp_b = {c: _softplus2(z) for c, z in z_b.items()}
        cs_a = {c: jnp.dot(sp.astype(BF16), tri, preferred_element_type=F32) for c, sp in sp_a.items()}
        cs_b = {c: jnp.dot(sp.astype(BF16), tri, preferred_element_type=F32) for c, sp in sp_b.items()}
        tot_a, tot_b, w_a, w_b = {}, {}, {}, {}
        for n, h in chains:
            c = (n, h)
            tot_a[c] = jnp.sum(sp_a[c], axis=-1, keepdims=True)
            tot_b[c] = jnp.where(has_b[n], jnp.sum(sp_b[c], axis=-1, keepdims=True), 0.0)
            before_b = jnp.where(has_b[n], tot_a[c][top], -MASKED_SCORE)
            w_a[c] = jnp.exp2(z_a[c] - cs_a[c])
            w_b[c] = jnp.exp2(z_b[c] - (cs_b[c] + before_b))
        for n, h in chains:
            c = (n, h)
            pv_a = jnp.dot(w_a[c].astype(BF16), v_a[n], preferred_element_type=F32)
            pv_b = jnp.dot(w_b[c].astype(BF16), v_b[n], preferred_element_type=F32)
            acc_ref[n, h, top] = pv_a[top] + pv_b
            carry_ref[n, h, top] = tot_a[c][top] + tot_b[c]
            if b_rows < blk:
                acc_ref[n, h, b_rows:] = pv_a[b_rows:]
                carry_ref[n, h, b_rows:] = tot_a[c][b_rows:]
        if write_outputs:
            for n, _ in blocks:
                write_out(n)

    first = step * per_step
    q_heads = [split_heads(q_ref[n * blk:(n + 1) * blk, :]) for n in range(per_step)]
    recent_rows = min(SB_RECENT_ROWS, blk)
    diagonal_pairs([(n, first + n) for n in range(per_step)], recent_rows, True)

    def older_pair(n, j_a):
        has_b, j_b = key_blocks(j_a)
        kt_a, kt_b = kt_ref[j_a], kt_ref[j_b]
        v_a, v_b = values(j_a), values(j_b)
        z_a = [jnp.dot(q_heads[n][h], kt_a, preferred_element_type=F32) for h in heads]
        z_b = [jnp.dot(q_heads[n][h], kt_b, preferred_element_type=F32) for h in heads]
        sp_a = [_softplus2(z) for z in z_a]
        sp_b = [_softplus2(z) for z in z_b]
        cs_a = [jnp.dot(sp.astype(BF16), tri, preferred_element_type=F32) for sp in sp_a]
        cs_b = [jnp.dot(sp.astype(BF16), tri, preferred_element_type=F32) for sp in sp_b]
        for h in heads:
            carry = carry_ref[n, h]
            after_a = carry + jnp.sum(sp_a[h], axis=-1, keepdims=True)
            tot_b = jnp.where(has_b, jnp.sum(sp_b[h], axis=-1, keepdims=True), 0.0)
            before_b = jnp.where(has_b, after_a, -MASKED_SCORE)
            w_a = jnp.exp2(z_a[h] - (cs_a[h] + carry))
            w_b = jnp.exp2(z_b[h] - (cs_b[h] + before_b))
            acc_ref[n, h] += (jnp.dot(w_a.astype(BF16), v_a, preferred_element_type=F32)
                              + jnp.dot(w_b.astype(BF16), v_b, preferred_element_type=F32))
            carry_ref[n, h] = after_a + tot_b

    def live(n):
        return jnp.minimum(jnp.min(carry_ref[n, 0]), jnp.min(carry_ref[n, 1])) <= dead_above

    row = lax.broadcasted_iota(jnp.int32, (blk, 1), 0)
    first_unfinished_block = jnp.where(row >= recent_rows, 1, 2)
    least = None
    for n in range(per_step):
        has_unvisited = first + n >= first_unfinished_block
        for h in heads:
            candidate = jnp.where(has_unvisited, carry_ref[n, h], -MASKED_SCORE)
            least = candidate if least is None else jnp.minimum(least, candidate)
    any_live = jnp.min(least) <= dead_above

    @pl.when(any_live)
    def _():
        for n in range(per_step):
            diagonal_pairs([(n, first + n)], blk, False)

            def body(c, n=n):
                older_pair(n, c[0])
                return c[0] - 2, live(n)

            lax.while_loop(lambda c: jnp.logical_and(c[0] >= 0, c[1]), body, (first + n - 2, live(n)))
            write_out(n)


def _sb_attention(q, kt, v, tri, qg, kg):
    b, s, width = q.shape
    blk = kt.shape[-1]
    n_blocks = s // blk
    per_step = min(SB_QBLOCKS_PER_STEP, n_blocks)
    assert n_blocks % per_step == 0
    rows = per_step * blk
    return pl.pallas_call(
        _sb_kernel,
        grid=(b, width // LANES, n_blocks // per_step),
        in_specs=[
            pl.BlockSpec((None, rows, LANES), lambda bi, hp, i: (bi, i, hp)),
            pl.BlockSpec((None, None, n_blocks, LANES, blk), lambda bi, hp, i: (bi, hp, 0, 0, 0)),
            pl.BlockSpec((None, s, LANES), lambda bi, hp, i: (bi, 0, hp)),
            _const_spec(tri.shape),
            _const_spec((1, LANES)),
            _const_spec((1, LANES)),
        ],
        out_specs=pl.BlockSpec((None, rows, LANES), lambda bi, hp, i: (bi, i, hp)),
        out_shape=jax.ShapeDtypeStruct((b, s, width), BF16),
        scratch_shapes=[pltpu.VMEM((per_step, 2, blk, LANES), F32),
                        pltpu.VMEM((per_step, 2, blk, 1), F32)],
        compiler_params=_params("parallel", "parallel", "arbitrary"),
        name="sb_attention",
    )(q, kt, v, tri, qg, kg)


def _ret_kernel(q_ref, kt_ref, v_ref, g_ref, gain_ref, decay_ref, zeta_ref, xi_ref, gc_ref,
                o_ref, state_ref):
    @pl.when(pl.program_id(2) == 0)
    def _():
        state_ref[...] = jnp.zeros_like(state_ref)

    c = RET_CHUNK
    d = RET_HEAD_DIM
    span = decay_ref.shape[-1]
    rows = q_ref.shape[0]
    gain = gain_ref[...]
    for hh in range(state_ref.shape[0]):
        lanes = slice(hh * d, (hh + 1) * d)
        decay = decay_ref[hh]
        zeta = zeta_ref[hh]
        xi = xi_ref[hh]
        g_chunk = gc_ref[hh]
        q = [q_ref[m * span:(m + 1) * span, lanes] for m in range(rows // span)]
        kt = [kt_ref[hh, :, m * span:(m + 1) * span] for m in range(rows // span)]
        v = [v_ref[m * span:(m + 1) * span, lanes] for m in range(rows // span)]
        scores = [jnp.dot(q[m], kt[m], preferred_element_type=F32) * decay
                  for m in range(rows // span)]
        inner = [jnp.dot(scores[m].astype(BF16), v[m], preferred_element_type=F32)
                 for m in range(rows // span)]
        contrib = []
        for n in range(rows // c):
            m, sub = divmod(n * c, span)
            kz = (kt[m][:, sub:sub + c].astype(F32) * zeta).astype(BF16)
            contrib.append(jnp.dot(kz, v[m][sub:sub + c], preferred_element_type=F32))
        state = state_ref[hh]
        for n in range(rows // c):
            m, sub = divmod(n * c, span)
            cross = jnp.dot(q[m][sub:sub + c], state.astype(BF16), preferred_element_type=F32) * xi
            state = g_chunk * state + contrib[n]
            ret = _rms_rows(inner[m][sub:sub + c] + cross, gain)
            gate = g_ref[n * c:(n + 1) * c, lanes].astype(F32)
            o_ref[n * c:(n + 1) * c, lanes] = (ret * gate).astype(o_ref.dtype)
        state_ref[hh] = state


def _retention(q, kt, v, g, gain, decay, zeta, xi, gc):
    b, s, width = q.shape
    rows = min(RET_GROUP, s)
    span = decay.shape[-1]
    hps = RET_HEADS_PER_STEP
    blk = pl.BlockSpec((None, rows, hps * RET_HEAD_DIM), lambda bi, h, n: (bi, n, h))
    head = lambda shape: pl.BlockSpec((hps,) + shape, lambda bi, h, n: (h, 0, 0))
    return pl.pallas_call(
        _ret_kernel,
        grid=(b, RET_HEADS // hps, s // rows),
        in_specs=[blk,
                  pl.BlockSpec((None, hps, RET_HEAD_DIM, rows), lambda bi, h, n: (bi, h, 0, n)),
                  blk, blk, _const_spec((1, RET_HEAD_DIM)),
                  head((span, span)), head((1, RET_CHUNK)), head((RET_CHUNK, 1)), head((1, 1))],
        out_specs=blk,
        out_shape=jax.ShapeDtypeStruct((b, s, width), BF16),
        scratch_shapes=[pltpu.VMEM((hps, RET_HEAD_DIM, RET_HEAD_DIM), F32)],
        compiler_params=_params("parallel", "parallel", "arbitrary"),
        name="retention",
    )(q, kt, v, g, gain, decay, zeta, xi, gc)


def _ffn_kernel(*refs, n_mix, tiles_per_seq):
    h_ref = refs[0]
    mix_refs = refs[1:1 + n_mix]
    (wo_ref, g_ref, wup_ref, cw_ref, cb_ref, wd_ref, o_ref,
     ubuf_ref, prev_ref, gate_ref) = refs[1 + n_mix:]
    tm = h_ref.shape[0]
    d_ff = wd_ref.shape[0]
    halo = SUBLANES

    @pl.when(pl.program_id(0) % tiles_per_seq == 0)
    def _():
        prev_ref[...] = jnp.zeros_like(prev_ref)

    h = h_ref[...]
    row0 = 0
    for a_ref in mix_refs:
        width = a_ref.shape[1]
        h = h + jnp.dot(a_ref[...], wo_ref[row0:row0 + width, :], preferred_element_type=F32)
        row0 += width
    hn = _rms_rows(h, g_ref[...]).astype(BF16)
    for c0 in range(0, d_ff, FF_CHUNK):
        cols = slice(c0, c0 + FF_CHUNK)
        u = jnp.dot(hn, wup_ref[:, cols], preferred_element_type=F32)
        v = jnp.dot(hn, wup_ref[:, d_ff + c0:d_ff + c0 + FF_CHUNK], preferred_element_type=F32)
        ubuf_ref[0:halo, :] = prev_ref[:, cols]
        ubuf_ref[halo:halo + tm, :] = u
        prev_ref[:, cols] = u[tm - halo:, :]
        last_tap = CONV_WIDTH - 1
        uc = cb_ref[:, cols] + cw_ref[last_tap:CONV_WIDTH, cols] * u
        for tap in range(last_tap):
            back = last_tap - tap
            uc = uc + cw_ref[tap:tap + 1, cols] * ubuf_ref[halo - back:halo - back + tm, :]
        gate_ref[:, cols] = (_silu(uc) * v).astype(BF16)
    o_ref[...] = h + jnp.dot(gate_ref[...], wd_ref[...], preferred_element_type=F32)


def _ffn(h2, mix, wo_stack, wo_layer, g, wup_stack, cw, cb, wd_stack, layer, seq):
    t, d = h2.shape
    d_ff = wd_stack.shape[1]
    tm = min(FFN_ROW_TILE, seq)
    row = lambda i: (i, 0)
    return pl.pallas_call(
        functools.partial(_ffn_kernel, n_mix=len(mix), tiles_per_seq=seq // tm),
        grid=(t // tm,),
        in_specs=([pl.BlockSpec((tm, d), row)]
                  + [pl.BlockSpec((tm, a.shape[1]), row) for a in mix]
                  + [_layer_spec(wo_stack.shape, wo_layer), _const_spec((1, d)),
                     _layer_spec(wup_stack.shape, layer), _const_spec(cw.shape),
                     _const_spec(cb.shape), _layer_spec(wd_stack.shape, layer)]),
        out_specs=pl.BlockSpec((tm, d), row),
        out_shape=jax.ShapeDtypeStruct((t, d), F32),
        scratch_shapes=[pltpu.VMEM((tm + SUBLANES, FF_CHUNK), F32),
                        pltpu.VMEM((SUBLANES, d_ff), F32),
                        pltpu.VMEM((tm, d_ff), BF16)],
        compiler_params=_params("arbitrary"),
        name="conv_ffn",
    )(h2, *mix, wo_stack, g, wup_stack, cw, cb, wd_stack)


def _log_sigmoid(x):
    return jnp.minimum(x, 0.0) - jnp.log(1.0 + jnp.exp(-jnp.abs(x)))


def _odd_in_kernel(x_ref, g_ref, w_ref, wa_ref, walpha_ref, balpha_ref, tri_ref,
                   qt_ref, ktt_ref, kdt_ref, v_ref, r_ref, dec_ref):
    tm = x_ref.shape[0]
    span = tri_ref.shape[0]
    hn = _rms_rows(x_ref[...], g_ref[...]).astype(BF16)

    def proj(c0, width):
        return jnp.dot(hn, w_ref[:, c0:c0 + width], preferred_element_type=F32)

    ga = jnp.dot(hn, wa_ref[...], preferred_element_type=F32)
    v_ref[...] = proj(2 * GLA_K_WIDTH, GLA_V_WIDTH).astype(BF16)
    r_ref[...] = _silu(proj(2 * GLA_K_WIDTH + GLA_V_WIDTH, GLA_V_WIDTH)).astype(BF16)
    ga_hi = ga.astype(BF16)
    ga_lo = (ga - ga_hi.astype(F32)).astype(BF16)
    lane = lax.broadcasted_iota(jnp.int32, ga.shape, 1)
    middle = jnp.logical_and(lane >= GLA_GATE_RANK, lane < 2 * GLA_GATE_RANK)
    pre = jnp.dot(jnp.where(middle, ga_lo, ga_hi), walpha_ref[...],
                  preferred_element_type=F32) + balpha_ref[...]
    la = _log_sigmoid(pre) * (LOG2E / GLA_GATE_TAU)
    la_hi = la.astype(BF16)
    la_lo = (la - la_hi.astype(F32)).astype(BF16)
    tri = tri_ref[...]
    cum = jnp.concatenate(
        [jnp.dot(tri, la_hi[r0:r0 + span], preferred_element_type=F32)
         + jnp.dot(tri, la_lo[r0:r0 + span], preferred_element_type=F32)
         for r0 in range(0, tm, span)], axis=0)
    chunks = tm // GLA_CHUNK
    last = cum.reshape(chunks, GLA_CHUNK, GLA_K_WIDTH)[:, GLA_CHUNK - 1:GLA_CHUNK, :]
    dec_ref[...] = jnp.exp2(last).reshape(chunks, GLA_K_WIDTH)
    to_end = (jnp.broadcast_to(last, (chunks, GLA_CHUNK, GLA_K_WIDTH)).reshape(tm, GLA_K_WIDTH)
              - cum)

    k = proj(GLA_K_WIDTH, GLA_K_WIDTH)
    kt = k * jnp.exp2(-cum)
    kd = k * jnp.exp2(to_end)
    for h in range(GLA_HEADS):
        sl = slice(h * GLA_DK, (h + 1) * GLA_DK)
        ktt_ref[h] = kt[:, sl].T.astype(BF16)
        kdt_ref[h] = kd[:, sl].T.astype(BF16)
    qt_ref[...] = (proj(0, GLA_K_WIDTH) * GLA_DK ** -0.5 * jnp.exp2(cum)).astype(BF16)


def _odd_in(x2, g, w_stack, layer, wa, walpha, balpha, tri, batch, seq):
    t, d = x2.shape
    tm = min(ROW_TILE, seq)
    tps = seq // tm
    row = lambda i: (i, 0)
    kt_shape = jax.ShapeDtypeStruct((batch, GLA_HEADS, GLA_DK, seq), BF16)
    kt_spec = pl.BlockSpec((None, GLA_HEADS, GLA_DK, tm), lambda i: (i // tps, 0, 0, i % tps))
    vout = jax.ShapeDtypeStruct((t, GLA_V_WIDTH), BF16)
    return pl.pallas_call(
        _odd_in_kernel,
        grid=(t // tm,),
        in_specs=[pl.BlockSpec((tm, d), row), _const_spec((1, d)), _layer_spec(w_stack.shape, layer),
                  _const_spec(wa.shape), _const_spec(walpha.shape), _const_spec(balpha.shape),
                  _const_spec(tri.shape)],
        out_specs=[pl.BlockSpec((tm, GLA_K_WIDTH), row), kt_spec, kt_spec,
                   pl.BlockSpec((tm, GLA_V_WIDTH), row), pl.BlockSpec((tm, GLA_V_WIDTH), row),
                   pl.BlockSpec((tm // GLA_CHUNK, GLA_K_WIDTH), row)],
        out_shape=[jax.ShapeDtypeStruct((t, GLA_K_WIDTH), BF16), kt_shape, kt_shape, vout, vout,
                   jax.ShapeDtypeStruct((t // GLA_CHUNK, GLA_K_WIDTH), F32)],
        compiler_params=_params("parallel"),
        name="odd_in",
    )(x2, g, w_stack, wa, walpha, balpha, tri)


def _gla_kernel(qt_ref, ktt_ref, kdt_ref, v_ref, r_ref, dec_ref, gain_ref, o_ref, state_ref):
    @pl.when(pl.program_id(2) == 0)
    def _():
        state_ref[...] = jnp.zeros_like(state_ref)

    c = GLA_CHUNK
    rows = qt_ref.shape[0]
    span = min(GLA_SPAN, rows)
    ri = lax.broadcasted_iota(jnp.int32, (span, span), 0)
    ci = lax.broadcasted_iota(jnp.int32, (span, span), 1)
    causal = jnp.logical_and(ci <= ri, ci // c == ri // c)
    col_chunk = lax.broadcasted_iota(jnp.int32, (1, span), 1) // c
    gain = gain_ref[...]
    for hh in range(state_ref.shape[0]):
        klanes = slice(hh * GLA_DK, (hh + 1) * GLA_DK)
        vlanes = slice(hh * GLA_DV, (hh + 1) * GLA_DV)
        dec_t = dec_ref[:, klanes].T
        spans = range(rows // span)
        qt = [qt_ref[m * span:(m + 1) * span, klanes] for m in spans]
        ktt = [ktt_ref[hh, :, m * span:(m + 1) * span] for m in spans]
        kdt = [kdt_ref[hh, :, m * span:(m + 1) * span] for m in spans]
        v = [v_ref[m * span:(m + 1) * span, vlanes] for m in spans]
        a = [jnp.where(causal, jnp.dot(qt[m], ktt[m], preferred_element_type=F32), 0.0)
             for m in spans]
        intra = [jnp.dot(a[m].astype(BF16), v[m], preferred_element_type=F32) for m in spans]
        contrib = []
        for n in range(rows // c):
            m, sub = divmod(n * c, span)
            kd_n = jnp.where(col_chunk == sub // c, kdt[m], jnp.zeros_like(kdt[m]))
            contrib.append(jnp.dot(kd_n, v[m], preferred_element_type=F32))
        state = state_ref[hh]
        for n in range(rows // c):
            m, sub = divmod(n * c, span)
            inter = jnp.dot(qt[m][sub:sub + c], state.astype(BF16), preferred_element_type=F32)
            state = dec_t[:, n:n + 1] * state + contrib[n]
            o = _rms_rows(intra[m][sub:sub + c] + inter, gain)
            gate = r_ref[n * c:(n + 1) * c, vlanes].astype(F32)
            o_ref[n * c:(n + 1) * c, vlanes] = (o * gate).astype(o_ref.dtype)
        state_ref[hh] = state


def _gla(qt, ktt, kdt, v, r, dec, gain):
    b, s, kw = qt.shape
    vw = v.shape[-1]
    rows = min(GLA_GROUP, s)
    hps = GLA_HEADS_PER_STEP
    chunks = rows // GLA_CHUNK
    dec = dec.reshape(b, s // rows, chunks, kw)
    qblk = pl.BlockSpec((None, rows, hps * GLA_DK), lambda bi, h, n: (bi, n, h))
    tblk = pl.BlockSpec((None, hps, GLA_DK, rows), lambda bi, h, n: (bi, h, 0, n))
    vblk = pl.BlockSpec((None, rows, hps * GLA_DV), lambda bi, h, n: (bi, n, h))
    dblk = pl.BlockSpec((None, None, chunks, hps * GLA_DK), lambda bi, h, n: (bi, n, 0, h))
    return pl.pallas_call(
        _gla_kernel,
        grid=(b, GLA_HEADS // hps, s // rows),
        in_specs=[qblk, tblk, tblk, vblk, vblk, dblk, _const_spec((1, GLA_DV))],
        out_specs=vblk,
        out_shape=jax.ShapeDtypeStruct((b, s, vw), BF16),
        scratch_shapes=[pltpu.VMEM((hps, GLA_DK, GLA_DV), F32)],
        compiler_params=_params("parallel", "parallel", "arbitrary"),
        name="gla",
    )(qt, ktt, kdt, v, r, dec, gain)


def _rope_tables(seq, tile):
    half = RET_HEAD_DIM // 2
    inv = ROPE_BASE ** (-np.arange(half, dtype=np.float64) / half)

    def both(pos):
        ang = pos[:, None] * inv[None, :]
        dup = lambda t: np.concatenate([t, t], axis=-1).astype(np.float32)
        return dup(np.cos(ang)), dup(np.sin(ang))

    cos_a, sin_a = both(np.arange(0, seq, tile, dtype=np.float64))
    cos_b, sin_b = both(np.arange(tile, dtype=np.float64))
    return (jnp.asarray(cos_a[:, None, :]), jnp.asarray(sin_a[:, None, :]),
            jnp.asarray(cos_b), jnp.asarray(sin_b))


def _retention_tables(span):
    c = RET_CHUNK
    lg = np.log(1.0 - np.exp2(-5.0 - np.arange(RET_HEADS, dtype=np.float64)))[:, None]
    pos = np.arange(span, dtype=np.float64)
    diff = pos[:, None] - pos[None, :]
    same = (pos[:, None] // c) == (pos[None, :] // c)
    decay = np.where((diff >= 0) & same, np.exp(lg[:, :, None] * np.maximum(diff, 0.0)), 0.0)
    pos = np.arange(c, dtype=np.float64)
    zeta = np.exp(lg * (c - 1 - pos))[:, None, :]
    xi = np.exp(lg * (pos + 1.0))[:, :, None]
    g_chunk = np.exp(lg * c)[:, :, None]
    return tuple(jnp.asarray(t.astype(np.float32)) for t in (decay, zeta, xi, g_chunk))


def _suffix_tri(n):
    idx = np.arange(n)
    return jnp.asarray(idx[:, None] >= idx[None, :], dtype=BF16)


def _chunk_prefix_tri(rows, chunk):
    idx = np.arange(rows)
    same = (idx[:, None] // chunk) == (idx[None, :] // chunk)
    return jnp.asarray(same & (idx[None, :] <= idx[:, None]), dtype=BF16)


def _gate_operands(w_gate_in, w_alpha):
    rank = GLA_GATE_RANK
    wa = jnp.pad(jnp.tile(w_gate_in, (1, 3)), ((0, 0), (0, LANES - 3 * rank))).astype(BF16)
    w_hi = w_alpha.astype(BF16)
    w_lo = (w_alpha - w_hi.astype(F32)).astype(BF16)
    walpha = jnp.pad(jnp.concatenate([w_hi, w_hi, w_lo], axis=0), ((0, LANES - 3 * rank), (0, 0)))
    return wa, walpha


def kernel(x, mix_norm_g, even_w_in, sb_q_gain, sb_k_gain, ret_out_gain, even_w_out,
           odd_w_in, gla_w_alpha, gla_b_alpha, gla_out_gain, odd_w_out,
           ffn_norm_g, ffn_w_up, ffn_conv_w, ffn_conv_b, ffn_w_down):
    b, s, d = x.shape
    t = b * s
    depth = mix_norm_g.shape[0]
    h = x.reshape(t, d)

    rope_tables = _rope_tables(s, min(ROW_TILE, s))
    decay, zeta, xi, g_chunk = _retention_tables(min(RET_SPAN, s))
    sb_blk = min(SB_BLOCK, s)
    sb_tri = _suffix_tri(sb_blk)
    gla_tri = _chunk_prefix_tri(min(GLA_SPAN, s), GLA_CHUNK)

    even_w_in, even_w_out, odd_w_in_b, odd_w_out, ffn_w_up, ffn_w_down = (
        w.astype(BF16) for w in (even_w_in, even_w_out, odd_w_in, odd_w_out, ffn_w_up, ffn_w_down))

    for layer in range(depth):
        g_mix = mix_norm_g[layer][None, :]
        if layer % 2 == 0:
            e = layer // 2
            qg = jnp.tile(sb_q_gain[e], 2)[None, :]
            kg = jnp.tile(sb_k_gain[e], 2)[None, :]
            sbq, sbkt, sbv, rq, rkt, rv, rg = _even_in(
                h, g_mix, even_w_in, e, qg, kg, rope_tables, b, s, sb_blk)
            out_a = _sb_attention(sbq.reshape(b, s, SB_WIDTH), sbkt, sbv.reshape(b, s, SB_WIDTH),
                                  sb_tri, qg, kg)
            shp = (b, s, RET_WIDTH)
            out_b = _retention(rq.reshape(shp), rkt, rv.reshape(shp), rg.reshape(shp),
                               ret_out_gain[e][None, :], decay, zeta, xi, g_chunk)
            mix = (out_a.reshape(t, SB_WIDTH), out_b.reshape(t, RET_WIDTH))
            w_out, w_out_layer = even_w_out, e
        else:
            o = layer // 2
            n_main = 2 * GLA_K_WIDTH + 2 * GLA_V_WIDTH
            wa, walpha = _gate_operands(odd_w_in[o, :, n_main:], gla_w_alpha[o])
            qt, ktt, kdt, gv, gr, dec = _odd_in(h, g_mix, odd_w_in_b, o, wa, walpha,
                                                gla_b_alpha[o][None, :], gla_tri, b, s)
            og = _gla(qt.reshape(b, s, GLA_K_WIDTH), ktt, kdt,
                      gv.reshape(b, s, GLA_V_WIDTH), gr.reshape(b, s, GLA_V_WIDTH),
                      dec.reshape(b, s // GLA_CHUNK, GLA_K_WIDTH), gla_out_gain[o][None, :])
            mix = (og.reshape(t, GLA_V_WIDTH),)
            w_out, w_out_layer = odd_w_out, o
        h = _ffn(h, mix, w_out, w_out_layer, ffn_norm_g[layer][None, :], ffn_w_up,
                 ffn_conv_w[layer], ffn_conv_b[layer][None, :], ffn_w_down, layer, s)
    return h.reshape(b, s, d)
```
